```python
import math
import jax, jax.numpy as jnp
from jax import lax
import numpy as np

D_MODEL = 2048
BATCH = 4
SEQ = 8192
DEPTH = 4

GRID_W = 64
CTX_LEN = 256
N_MIXERS = 4
EPS = 1e-6
ROPE_BASE = 10000.0
Q_BLOCK = 128
NEG_INF = -1e30

MLA_HEADS = 16
MLA_NOPE = 128
MLA_ROPE = 64
MLA_V = 128
MLA_Q_RANK = 512
MLA_KV_RANK = 512
MLA_WIDTH = MLA_HEADS * MLA_V
MLA_IN = MLA_Q_RANK + MLA_KV_RANK + MLA_ROPE + MLA_WIDTH

SWA_HEADS = 32
SWA_KV_HEADS = 4
SWA_HD = 64
SWA_WINDOW = 128
SWA_Q_DIM = SWA_HEADS * SWA_HD
SWA_KV_DIM = SWA_KV_HEADS * SWA_HD
SWA_WIDTH = SWA_Q_DIM
SWA_IN = SWA_Q_DIM + 2 * SWA_KV_DIM + SWA_WIDTH

NA_HEADS = 32
NA_HD = 64
NA_WIN_R = 8
NA_WIN_C = 16
NA_WIDTH = NA_HEADS * NA_HD
NA_IN = 4 * NA_WIDTH

DIFF_HEADS = 8
DIFF_HD = 128
DIFF_VD = 2 * DIFF_HD
DIFF_QK_DIM = DIFF_HEADS * 2 * DIFF_HD
DIFF_WIDTH = DIFF_HEADS * DIFF_VD
DIFF_IN = 2 * DIFF_QK_DIM + 2 * DIFF_WIDTH

kernel_name = "hybrid_interleaved_dit_backbone"

f32 = jnp.float32


def rmsnorm(x, g):
    xf = x.astype(f32)
    y = xf * lax.rsqrt(jnp.mean(xf * xf, axis=-1, keepdims=True) + EPS)
    return (y * g.astype(f32)).astype(x.dtype)


def axial_rope_tables(n, d_rot):
    t = jnp.arange(n)
    row = (t // GRID_W).astype(f32)
    col = (t % GRID_W).astype(f32)
    d_axis = d_rot // 2
    inv = ROPE_BASE ** (-jnp.arange(0, d_axis, 2, dtype=f32) / d_axis)
    ang = jnp.concatenate([row[:, None] * inv, col[:, None] * inv], axis=-1)
    return jnp.cos(ang), jnp.sin(ang)


def apply_rope(x, cos, sin):
    x1, x2 = jnp.split(x, 2, axis=-1)
    c = cos[None, :, None, :]
    s = sin[None, :, None, :]
    return jnp.concatenate([x1 * c - x2 * s, x1 * s + x2 * c], axis=-1).astype(x.dtype)


def add_all(xs):
    out = xs[0]
    for t in xs[1:]:
        out = out + t
    return out


def gated_out(o, gate, w_out):
    b, n = o.shape[:2]
    return (o.reshape(b, n, -1) * jax.nn.silu(gate)) @ w_out


def map_query_blocks(fn, *qs):
    b, n = qs[0].shape[:2]
    nb = n // Q_BLOCK
    blk = tuple(jnp.moveaxis(q.reshape(b, nb, Q_BLOCK, *q.shape[2:]), 1, 0) for q in qs)
    out = lax.map(lambda a: fn(*a), blk)
    out = jnp.moveaxis(out, 0, 1)
    return out.reshape(b, n, *out.shape[3:])


def mla_mixer(h_lat, h_ctx, w_in, q_norm, w_qb, kv_norm, w_kvb, w_out, cos, sin, need_ctx):
    splits = [MLA_Q_RANK, MLA_Q_RANK + MLA_KV_RANK, MLA_Q_RANK + MLA_KV_RANK + MLA_ROPE]
    scale = (MLA_NOPE + MLA_ROPE) ** -0.5

    def project(h, rotary):
        b, n, _ = h.shape
        q_c, kv_c, k_r, gate = jnp.split(h @ w_in, splits, axis=-1)
        q = (rmsnorm(q_c, q_norm) @ w_qb).reshape(b, n, MLA_HEADS, MLA_NOPE + MLA_ROPE)
        kv = (rmsnorm(kv_c, kv_norm) @ w_kvb).reshape(b, n, MLA_HEADS, MLA_NOPE + MLA_V)
        q_n, q_r = q[..., :MLA_NOPE], q[..., MLA_NOPE:]
        k_n, v = kv[..., :MLA_NOPE], kv[..., MLA_NOPE:]
        k_r = k_r[:, :, None, :]
        if rotary:
            q_r = apply_rope(q_r, cos, sin)
            k_r = apply_rope(k_r, cos, sin)
        return q_n, q_r, k_n, k_r[:, :, 0, :], v, gate

    qn_l, qr_l, kn_l, kr_l, v_l, g_l = project(h_lat, True)
    qn_c, qr_c, kn_c, kr_c, v_c, g_c = project(h_ctx, False)

    def logits(qn, qr, kn, kr):
        return (jnp.einsum('bqhd,bkhd->bhqk', qn, kn, preferred_element_type=f32)
                + jnp.einsum('bqhr,bkr->bhqk', qr, kr, preferred_element_type=f32)) * scale

    def attend(qn, qr, kvs):
        p = jax.nn.softmax(jnp.concatenate([logits(qn, qr, kn, kr) for kn, kr, _ in kvs], axis=-1), axis=-1)
        outs, off = [], 0
        for kn, _, vv in kvs:
            n_k = kn.shape[1]
            outs.append(jnp.einsum('bhqk,bkhd->bqhd', p[..., off:off + n_k].astype(vv.dtype), vv))
            off += n_k
        return add_all(outs)

    ctx_kv = [(kn_c, kr_c, v_c)]
    lat_kv = ctx_kv + [(kn_l, kr_l, v_l)]
    o_l = map_query_blocks(lambda qn, qr: attend(qn, qr, lat_kv), qn_l, qr_l)
    y_l = gated_out(o_l, g_l, w_out)
    y_c = gated_out(attend(qn_c, qr_c, ctx_kv), g_c, w_out) if need_ctx else None
    return y_l, y_c


def swa_mixer(h_lat, h_ctx, w_in, sink, w_out, cos, sin, need_ctx):
    groups = SWA_HEADS // SWA_KV_HEADS
    scale = SWA_HD ** -0.5
    sink_col = sink.astype(f32).reshape(1, SWA_KV_HEADS, groups, 1, 1)

    def project(h, rotary):
        b, n, _ = h.shape
        q, k, v, gate = jnp.split(h @ w_in, [SWA_Q_DIM, SWA_Q_DIM + SWA_KV_DIM, SWA_Q_DIM + 2 * SWA_KV_DIM], axis=-1)
        q = q.reshape(b, n, SWA_HEADS, SWA_HD)
        k = k.reshape(b, n, SWA_KV_HEADS, SWA_HD)
        v = v.reshape(b, n, SWA_KV_HEADS, SWA_HD)
        if rotary:
            q = apply_rope(q, cos, sin)
            k = apply_rope(k, cos, sin)
        return q.reshape(b, n, SWA_KV_HEADS, groups, SWA_HD), k, v, gate

    def attend(qb, keys, vals, mask):
        s_list = [jnp.einsum('bqhgd,bshd->bhgqs', qb, kk, preferred_element_type=f32) * scale for kk in keys]
        if mask is not None:
            s_list[-1] = jnp.where(mask, s_list[-1], NEG_INF)
        sk = jnp.broadcast_to(sink_col, s_list[0].shape[:-1] + (1,))
        p = jax.nn.softmax(jnp.concatenate(s_list + [sk], axis=-1), axis=-1)
        outs, off = [], 0
        for kk, vv in zip(keys, vals):
            n_k = kk.shape[1]
            outs.append(jnp.einsum('bhgqs,bshd->bqhgd', p[..., off:off + n_k].astype(vv.dtype), vv))
            off += n_k
        o = add_all(outs)
        return o.reshape(o.shape[0], o.shape[1], SWA_HEADS, SWA_HD)

    q_l, k_l, v_l, g_l = project(h_lat, True)
    q_c, k_c, v_c, g_c = project(h_ctx, False)
    b, n = h_lat.shape[:2]
    nb = n // Q_BLOCK

    def band(t):
        tp = jnp.pad(t, ((0, 0), (Q_BLOCK, Q_BLOCK), (0, 0), (0, 0)))
        tb = tp.reshape(b, nb + 2, Q_BLOCK, *t.shape[2:])
        bnd = jnp.concatenate([tb[:, :-2], tb[:, 1:-1], tb[:, 2:]], axis=2)
        return jnp.moveaxis(bnd, 1, 0)

    q_blk = jnp.moveaxis(q_l.reshape(b, nb, Q_BLOCK, SWA_KV_HEADS, groups, SWA_HD), 1, 0)
    qpos_in = jnp.arange(Q_BLOCK)[:, None]
    kpos_in = jnp.arange(3 * Q_BLOCK)[None, :]

    def block(args):
        qb, kb, vb, i = args
        qpos = i * Q_BLOCK + qpos_in
        kpos = i * Q_BLOCK - Q_BLOCK + kpos_in
        mask = (jnp.abs(qpos - kpos) <= SWA_WINDOW) & (kpos >= 0) & (kpos < n)
        return attend(qb, [k_c, kb], [v_c, vb], mask)

    o_l = lax.map(block, (q_blk, band(k_l), band(v_l), jnp.arange(nb)))
    o_l = jnp.moveaxis(o_l, 0, 1).reshape(b, n, SWA_HEADS, SWA_HD)
    y_l = gated_out(o_l, g_l, w_out)
    y_c = gated_out(attend(q_c, [k_c], [v_c], None), g_c, w_out) if need_ctx else None
    return y_l, y_c


def na_mixer(h_lat, h_ctx, w_in, rpb, w_out, need_ctx):
    scale = NA_HD ** -0.5

    def project(h):
        b, n, _ = h.shape
        q, k, v, gate = jnp.split(h @ w_in, [NA_WIDTH, 2 * NA_WIDTH, 3 * NA_WIDTH], axis=-1)
        shp = (b, n, NA_HEADS, NA_HD)
        return q.reshape(shp), k.reshape(shp), v.reshape(shp), gate

    q_l, k_l, v_l, g_l = project(h_lat)
    q_c, k_c, v_c, g_c = project(h_ctx)
    b, n = h_lat.shape[:2]
    rows = n // GRID_W
    win_r = min(NA_WIN_R, rows)
    n_ctx = k_c.shape[1]

    k_grid = k_l.reshape(b, rows, GRID_W, NA_HEADS, NA_HD)
    v_grid = v_l.reshape(b, rows, GRID_W, NA_HEADS, NA_HD)
    q_rows = jnp.moveaxis(q_l.reshape(b, rows, GRID_W, NA_HEADS, NA_HD), 1, 0)

    cq = jnp.arange(GRID_W)
    cs = jnp.clip(cq - NA_WIN_C // 2, 0, GRID_W - NA_WIN_C)
    ck = jnp.arange(GRID_W)
    col_ok = (ck[None, :] >= cs[:, None]) & (ck[None, :] < cs[:, None] + NA_WIN_C)
    mask = jnp.broadcast_to(col_ok[:, None, :], (GRID_W, win_r, GRID_W)).reshape(GRID_W, win_r * GRID_W)
    dc_idx = jnp.clip(ck[None, :] - cq[:, None], -(NA_WIN_C - 1), NA_WIN_C - 1) + (NA_WIN_C - 1)

    def row_block(args):
        qr, r = args
        rs = jnp.clip(r - win_r // 2, 0, rows - win_r)
        kr = lax.dynamic_slice_in_dim(k_grid, rs, win_r, axis=1)
        vr = lax.dynamic_slice_in_dim(v_grid, rs, win_r, axis=1)
        dr_idx = rs + jnp.arange(win_r) - r + (NA_WIN_R - 1)
        bias = rpb[:, dr_idx[:, None, None], dc_idx[None, :, :]]
        bias = jnp.transpose(bias, (0, 2, 1, 3)).reshape(NA_HEADS, GRID_W, win_r * GRID_W)
        s_lat = jnp.einsum('bqhd,brkhd->bhqrk', qr, kr, preferred_element_type=f32)
        s_lat = s_lat.reshape(b, NA_HEADS, GRID_W, win_r * GRID_W) * scale + bias.astype(f32)
        s_lat = jnp.where(mask, s_lat, NEG_INF)
        s_ctx = jnp.einsum('bqhd,bshd->bhqs', qr, k_c, preferred_element_type=f32) * scale
        p = jax.nn.softmax(jnp.concatenate([s_ctx, s_lat], axis=-1), axis=-1)
        p_lat = p[..., n_ctx:].reshape(b, NA_HEADS, GRID_W, win_r, GRID_W)
        return (jnp.einsum('bhqs,bshd->bqhd', p[..., :n_ctx].astype(v_c.dtype), v_c)
                + jnp.einsum('bhqrk,brkhd->bqhd', p_lat.astype(vr.dtype), vr))

    o_l = lax.map(row_block, (q_rows, jnp.arange(rows)))
    o_l = jnp.moveaxis(o_l, 0, 1).reshape(b, n, NA_HEADS, NA_HD)
    y_l = gated_out(o_l, g_l, w_out)
    y_c = None
    if need_ctx:
        s = jnp.einsum('bqhd,bshd->bhqs', q_c, k_c, preferred_element_type=f32) * scale
        p = jax.nn.softmax(s, axis=-1)
        o_c = jnp.einsum('bhqs,bshd->bqhd', p.astype(v_c.dtype), v_c)
        y_c = gated_out(o_c, g_c, w_out)
    return y_l, y_c


def diff_mixer(h_lat, h_ctx, w_in, lam_q1, lam_k1, lam_q2, lam_k2, subln, w_out, cos, sin, lam_init, need_ctx):
    scale = DIFF_HD ** -0.5
    lam = (jnp.exp(jnp.sum(lam_q1.astype(f32) * lam_k1.astype(f32)))
           - jnp.exp(jnp.sum(lam_q2.astype(f32) * lam_k2.astype(f32))) + lam_init)

    def project(h, rotary):
        b, n, _ = h.shape
        q, k, v, gate = jnp.split(h @ w_in, [DIFF_QK_DIM, 2 * DIFF_QK_DIM, 2 * DIFF_QK_DIM + DIFF_WIDTH], axis=-1)
        q = q.reshape(b, n, 2 * DIFF_HEADS, DIFF_HD)
        k = k.reshape(b, n, 2 * DIFF_HEADS, DIFF_HD)
        if rotary:
            q = apply_rope(q, cos, sin)
            k = apply_rope(k, cos, sin)
        q = q.reshape(b, n, DIFF_HEADS, 2, DIFF_HD)
        k = k.reshape(b, n, DIFF_HEADS, 2, DIFF_HD)
        return q, k, v.reshape(b, n, DIFF_HEADS, DIFF_VD), gate

    def attend(qb, keys, vals):
        s = jnp.concatenate([jnp.einsum('bqhid,bshid->bhiqs', qb, kk, preferred_element_type=f32) for kk in keys],
                            axis=-1) * scale
        p = jax.nn.softmax(s, axis=-1)
        a = p[:, :, 0] - lam * p[:, :, 1]
        outs, off = [], 0
        for kk, vv in zip(keys, vals):
            n_k = kk.shape[1]
            outs.append(jnp.einsum('bhqs,bshd->bqhd', a[..., off:off + n_k].astype(vv.dtype), vv))
            off += n_k
        o = add_all(outs)
        return rmsnorm(o, subln) * (1.0 - lam_init)

    q_l, k_l, v_l, g_l = project(h_lat, True)
    q_c, k_c, v_c, g_c = project(h_ctx, False)
    o_l = map_query_blocks(lambda qb: attend(qb, [k_c, k_l], [v_c, v_l]), q_l)
    y_l = gated_out(o_l, g_l, w_out)
    y_c = gated_out(attend(q_c, [k_c], [v_c]), g_c, w_out) if need_ctx else None
    return y_l, y_c


def setup_inputs(seed: int = 0) -> dict:
    key = jax.random.key(seed)
    ks = list(jax.random.split(key, 48))
    D = D_MODEL

    def nrm(shape, s):
        return jax.random.normal(ks.pop(), shape, f32) * s

    def gain(n):
        return 1.0 + nrm((n,), 0.02)

    inp = {}
    inp["x"] = nrm((BATCH, SEQ, D), 1.0)
    inp["c"] = nrm((BATCH, D), 1.0)
    inp["ctx"] = nrm((BATCH, CTX_LEN, D), 1.0)
    inp["c_ctx"] = nrm((D,), 1.0)
    inp["l0_mod_w"] = nrm((D, 3 * D), D ** -0.5)
    inp["l0_mod_b"] = nrm((3 * D,), 0.02)
    inp["l0_norm"] = gain(D)
    inp["l0_w_in"] = nrm((D, MLA_IN), D ** -0.5)
    inp["l0_q_norm"] = gain(MLA_Q_RANK)
    inp["l0_w_qb"] = nrm((MLA_Q_RANK, MLA_HEADS * (MLA_NOPE + MLA_ROPE)), MLA_Q_RANK ** -0.5)
    inp["l0_kv_norm"] = gain(MLA_KV_RANK)
    inp["l0_w_kvb"] = nrm((MLA_KV_RANK, MLA_HEADS * (MLA_NOPE + MLA_V)), MLA_KV_RANK ** -0.5)
    inp["l0_w_out"] = nrm((MLA_WIDTH, D), MLA_WIDTH ** -0.5)
    inp["l1_mod_w"] = nrm((D, 3 * D), D ** -0.5)
    inp["l1_mod_b"] = nrm((3 * D,), 0.02)
    inp["l1_norm"] = gain(D)
    inp["l1_w_in"] = nrm((D, SWA_IN), D ** -0.5)
    inp["l1_sink"] = nrm((SWA_HEADS,), 0.5)
    inp["l1_w_out"] = nrm((SWA_WIDTH, D), SWA_WIDTH ** -0.5)
    inp["l2_mod_w"] = nrm((D, 3 * D), D ** -0.5)
    inp["l2_mod_b"] = nrm((3 * D,), 0.02)
    inp["l2_norm"] = gain(D)
    inp["l2_w_in"] = nrm((D, NA_IN), D ** -0.5)
    inp["l2_rpb"] = nrm((NA_HEADS, 2 * NA_WIN_R - 1, 2 * NA_WIN_C - 1), 0.1)
    inp["l2_w_out"] = nrm((NA_WIDTH, D), NA_WIDTH ** -0.5)
    inp["l3_mod_w"] = nrm((D, 3 * D), D ** -0.5)
    inp["l3_mod_b"] = nrm((3 * D,), 0.02)
    inp["l3_norm"] = gain(D)
    inp["l3_w_in"] = nrm((D, DIFF_IN), D ** -0.5)
    inp["l3_lam_q1"] = nrm((DIFF_HD,), 0.1)
    inp["l3_lam_k1"] = nrm((DIFF_HD,), 0.1)
    inp["l3_lam_q2"] = nrm((DIFF_HD,), 0.1)
    inp["l3_lam_k2"] = nrm((DIFF_HD,), 0.1)
    inp["l3_subln"] = gain(DIFF_VD)
    inp["l3_w_out"] = nrm((DIFF_WIDTH, D), DIFF_WIDTH ** -0.5)
    inp["final_norm"] = gain(D)
    return inp


def reference(x, c, ctx, c_ctx,
              l0_mod_w, l0_mod_b, l0_norm, l0_w_in, l0_q_norm, l0_w_qb, l0_kv_norm, l0_w_kvb, l0_w_out,
              l1_mod_w, l1_mod_b, l1_norm, l1_w_in, l1_sink, l1_w_out,
              l2_mod_w, l2_mod_b, l2_norm, l2_w_in, l2_rpb, l2_w_out,
              l3_mod_w, l3_mod_b, l3_norm, l3_w_in, l3_lam_q1, l3_lam_k1, l3_lam_q2, l3_lam_k2, l3_subln, l3_w_out,
              final_norm):
    n = x.shape[1]
    cos64, sin64 = axial_rope_tables(n, MLA_ROPE)
    cos128, sin128 = axial_rope_tables(n, DIFF_HD)

    layer_mods = [(l0_mod_w, l0_mod_b, l0_norm), (l1_mod_w, l1_mod_b, l1_norm),
                  (l2_mod_w, l2_mod_b, l2_norm), (l3_mod_w, l3_mod_b, l3_norm)]
    mixer_params = [(l0_w_in, l0_q_norm, l0_w_qb, l0_kv_norm, l0_w_kvb, l0_w_out),
                    (l1_w_in, l1_sink, l1_w_out),
                    (l2_w_in, l2_rpb, l2_w_out),
                    (l3_w_in, l3_lam_q1, l3_lam_k1, l3_lam_q2, l3_lam_k2, l3_subln, l3_w_out)]

    for i in range(DEPTH):
        kind = i % N_MIXERS
        need_ctx = i < DEPTH - 1
        mod_w, mod_b, norm_g = layer_mods[i]
        mod_l = jax.nn.silu(c) @ mod_w + mod_b
        mod_c = jax.nn.silu(c_ctx) @ mod_w + mod_b
        shift_l, scale_l, gate_l = jnp.split(mod_l, 3, axis=-1)
        shift_c, scale_c, gate_c = jnp.split(mod_c, 3, axis=-1)
        h_l = rmsnorm(x, norm_g) * (1.0 + scale_l[:, None, :]) + shift_l[:, None, :]
        h_c = rmsnorm(ctx, norm_g) * (1.0 + scale_c) + shift_c
        p = mixer_params[i]
        if kind == 0:
            y_l, y_c = mla_mixer(h_l, h_c, *p, cos64, sin64, need_ctx)
        elif kind == 1:
            y_l, y_c = swa_mixer(h_l, h_c, *p, cos64, sin64, need_ctx)
        elif kind == 2:
            y_l, y_c = na_mixer(h_l, h_c, *p, need_ctx)
        else:
            lam_init = 0.8 - 0.6 * math.exp(-0.3 * i)
            y_l, y_c = diff_mixer(h_l, h_c, *p, cos128, sin128, lam_init, need_ctx)
        x = x + gate_l[:, None, :] * y_l
        if need_ctx:
            ctx = ctx + gate_c * y_c

    return rmsnorm(x, final_norm)
```

```python
import functools
import math

import jax
import jax.numpy as jnp
from jax import lax
from jax.experimental import pallas as pl
from jax.experimental.pallas import tpu as pltpu

F32 = jnp.float32
BF16 = jnp.bfloat16

LANES = 128
LOG2E = 1.4426950408889634
NEG_INF = -1e30
EPS = 1e-6
ROPE_BASE = 10000.0
GRID_W = 64
VMEM_LIMIT = 56 * 1024 * 1024

MLA_HEADS, MLA_NOPE, MLA_ROPE, MLA_V, MLA_RANK = 16, 128, 64, 128, 512
SWA_HEADS, SWA_KV_HEADS, SWA_HD, SWA_WINDOW = 32, 4, 64, 128
NA_HEADS, NA_HD, NA_WIN_R, NA_WIN_C = 32, 64, 8, 16
DIFF_HEADS, DIFF_HD = 8, 128

PLAIN, ROPE64, ROPE128 = 0, 1, 2


def _cparams(sem):
    return pltpu.CompilerParams(dimension_semantics=sem, vmem_limit_bytes=VMEM_LIMIT)


def _dot_nt(a, b):
    return lax.dot_general(a, b, (((1,), (1,)), ((), ())), preferred_element_type=F32)


def _dot(a, b):
    return jnp.dot(a, b, preferred_element_type=F32)


def _rep(x, n):
    return x if n == 1 else jnp.concatenate([x] * n, axis=1)


def _mod_kernel(c_ref, w_ref, b_ref, o_ref):
    c = c_ref[...]
    s = c * jax.nn.sigmoid(c)
    o_ref[...] = _dot(s, w_ref[...]) + b_ref[...]


def _modulation(c_rows, mod_w, mod_b):
    r, d = c_rows.shape
    n = mod_w.shape[1]
    tn = math.gcd(n, 512)
    return pl.pallas_call(
        _mod_kernel,
        out_shape=jax.ShapeDtypeStruct((r, n), F32),
        grid=(n // tn,),
        in_specs=[pl.BlockSpec((r, d), lambda j: (0, 0)),
                  pl.BlockSpec((d, tn), lambda j: (0, j)),
                  pl.BlockSpec((1, tn), lambda j: (0, j))],
        out_specs=pl.BlockSpec((r, tn), lambda j: (0, j)),
        compiler_params=_cparams(("arbitrary",)),
        name="modulation",
    )(c_rows, mod_w, mod_b.reshape(1, n))


def _rope_group(a, cos, sin, code):
    if code == ROPE64:
        lane = lax.broadcasted_iota(jnp.int32, a.shape, 1)
        first = (lane % 64) < 32
        partner = jnp.where(first, pltpu.roll(a, 96, 1), pltpu.roll(a, 32, 1))
    else:
        partner = pltpu.roll(a, 64, 1)
    return a * cos + partner * sin


def _proj_kernel(*refs, has_mod, has_rope, tile_patterns, d_mod):
    it = iter(refs)
    x_ref, g_ref = next(it), next(it)
    mod_ref = next(it) if has_mod else None
    w_ref = next(it)
    cos_ref, sin_ref = (next(it), next(it)) if has_rope else (None, None)
    o_ref, h_scr = next(it), next(it)
    j = pl.program_id(1)

    @pl.when(j == 0)
    def _():
        x = x_ref[...].astype(F32)
        y = x * lax.rsqrt(jnp.mean(x * x, axis=-1, keepdims=True) + EPS) * g_ref[...]
        if has_mod:
            shift = mod_ref[0, :, 0:d_mod]
            scale = mod_ref[0, :, d_mod:2 * d_mod]
            y = y * (1.0 + scale) + shift
        h_scr[...] = y.astype(BF16)

    def epilogue(pattern):
        acc = _dot(h_scr[...], w_ref[...])
        for gi, (code, scale) in enumerate(pattern):
            a = acc[:, gi * LANES:(gi + 1) * LANES]
            if code != PLAIN:
                a = _rope_group(a, cos_ref[...], sin_ref[...], code)
            if scale != 1.0:
                a = a * scale
            o_ref[:, gi * LANES:(gi + 1) * LANES] = a.astype(o_ref.dtype)

    distinct = sorted(set(tile_patterns), key=tile_patterns.index)
    if len(distinct) == 1:
        epilogue(distinct[0])
    else:
        for pat in distinct:
            tiles = [jj for jj, p in enumerate(tile_patterns) if p == pat]
            cond = functools.reduce(jnp.logical_or, [j == jj for jj in tiles])
            pl.when(cond)(functools.partial(epilogue, pat))


def _projection(x_all, col_blk, kd, gain, mod, w, rope, groups, geo, tm, tn, name):
    m_all = x_all.shape[0]
    n_out = w.shape[1]
    assert n_out % tn == 0 and tn % LANES == 0 and len(groups) == n_out // LANES
    gpt = tn // LANES
    tile_patterns = tuple(tuple(groups[t * gpt:(t + 1) * gpt]) for t in range(n_out // tn))
    b, n, ctx = geo
    lat_tiles, tiles_per_batch = b * n // tm, n // tm
    has_mod, has_rope = mod is not None, rope is not None

    def row_group(i):
        return jnp.where(i < lat_tiles, i // tiles_per_batch, b)

    def pos_blk(i):
        return jnp.where(i < lat_tiles, i % tiles_per_batch, tiles_per_batch)

    in_specs = [pl.BlockSpec((tm, kd), lambda i, j: (i, col_blk)),
                pl.BlockSpec((1, kd), lambda i, j: (0, 0))]
    args = [x_all, gain.reshape(1, kd).astype(F32)]
    d_mod = 0
    if has_mod:
        d_mod = mod.shape[1] // 3
        in_specs.append(pl.BlockSpec((1, 1, mod.shape[1]), lambda i, j: (row_group(i), 0, 0)))
        args.append(mod.reshape(mod.shape[0], 1, mod.shape[1]))
    in_specs.append(pl.BlockSpec((kd, tn), lambda i, j: (0, j)))
    args.append(w)
    if has_rope:
        in_specs += [pl.BlockSpec((tm, LANES), lambda i, j: (pos_blk(i), 0))] * 2
        args += [rope[0], rope[1]]
    kern = functools.partial(_proj_kernel, has_mod=has_mod, has_rope=has_rope,
                             tile_patterns=tile_patterns, d_mod=d_mod)
    return pl.pallas_call(
        kern,
        out_shape=jax.ShapeDtypeStruct((m_all, n_out), BF16),
        grid=(m_all // tm, n_out // tn),
        in_specs=in_specs,
        out_specs=pl.BlockSpec((tm, tn), lambda i, j: (i, j)),
        scratch_shapes=[pltpu.VMEM((tm, kd), BF16)],
        compiler_params=_cparams(("parallel", "arbitrary")),
        name=name,
    )(*args)


def _out_kernel(o_ref, gate_ref, w_ref, x_ref, gm_ref, y_ref, u_scr):
    @pl.when(pl.program_id(1) == 0)
    def _():
        g = gate_ref[...].astype(F32)
        u_scr[...] = (o_ref[...].astype(F32) * (g * jax.nn.sigmoid(g))).astype(BF16)

    y_ref[...] = x_ref[...] + gm_ref[0] * _dot(u_scr[...], w_ref[...])


def _out_projection(o_all, proj, w_out, x_all, mod, geo, tm, tn, name):
    m_all, d = x_all.shape
    width = w_out.shape[0]
    b, n, ctx = geo
    lat_tiles, tiles_per_batch = b * n // tm, n // tm

    def row_group(i):
        return jnp.where(i < lat_tiles, i // tiles_per_batch, b)

    gate_blk = 2 * d // tn
    return pl.pallas_call(
        _out_kernel,
        out_shape=jax.ShapeDtypeStruct((m_all, d), F32),
        grid=(m_all // tm, d // tn),
        in_specs=[pl.BlockSpec((tm, width), lambda i, j: (i, 0)),
                  pl.BlockSpec((tm, width), lambda i, j: (i, 0)),
                  pl.BlockSpec((width, tn), lambda i, j: (0, j)),
                  pl.BlockSpec((tm, tn), lambda i, j: (i, j)),
                  pl.BlockSpec((1, 1, tn), lambda i, j: (row_group(i), 0, gate_blk + j))],
        out_specs=pl.BlockSpec((tm, tn), lambda i, j: (i, j)),
        scratch_shapes=[pltpu.VMEM((tm, width), BF16)],
        input_output_aliases={3: 0},
        compiler_params=_cparams(("parallel", "arbitrary")),
        name=name,
    )(o_all, proj, w_out, x_all, mod.reshape(mod.shape[0], 1, mod.shape[1]))


def _norm_kernel(x_ref, g_ref, o_ref):
    x = x_ref[...]
    o_ref[...] = x * lax.rsqrt(jnp.mean(x * x, axis=-1, keepdims=True) + EPS) * g_ref[...]


def _final_norm(x_all, gain, rows, tm):
    d = x_all.shape[1]
    return pl.pallas_call(
        _norm_kernel,
        out_shape=jax.ShapeDtypeStruct((rows, d), F32),
        grid=(rows // tm,),
        in_specs=[pl.BlockSpec((tm, d), lambda i: (i, 0)),
                  pl.BlockSpec((1, d), lambda i: (0, 0))],
        out_specs=pl.BlockSpec((tm, d), lambda i: (i, 0)),
        compiler_params=_cparams(("parallel",)),
        name="final_norm",
    )(x_all, gain.reshape(1, d))


def _q_row_block(geo, tq):
    b_, n, ctx = geo
    lat = n // tq

    def f(b, t):
        return jnp.where(t < lat, b * lat + t, b_ * lat + b * (ctx // tq) + (t - lat))
    return f


def _softmax_init(s, v, m_scr, l_scr, acc_scr):
    m = jnp.max(s, axis=-1, keepdims=True)
    p = jnp.exp2(s - m)
    m_scr[...] = jnp.broadcast_to(m, m_scr.shape)
    l_scr[...] = jnp.broadcast_to(jnp.sum(p, axis=-1, keepdims=True), l_scr.shape)
    acc_scr[...] = _dot(p.astype(BF16), v)


def _softmax_step(s, v, m_scr, l_scr, acc_scr):
    m_prev = m_scr[...]
    m_new = jnp.maximum(m_prev, jnp.max(s, axis=-1, keepdims=True))
    alpha = jnp.exp2(m_prev - m_new)
    p = jnp.exp2(s - _rep(m_new, s.shape[1] // LANES))
    l_scr[...] = alpha * l_scr[...] + jnp.sum(p, axis=-1, keepdims=True)
    acc_scr[...] = _rep(alpha, acc_scr.shape[1] // LANES) * acc_scr[...] + _dot(p.astype(BF16), v)
    m_scr[...] = m_new


def _mla_kernel(q_ref, kvl_ref, kvc_ref, krl_ref, krc_ref, o_ref,
                kl_scr, kc_scr, m_scr, l_scr, acc_scr, *, lat_tiles, tk):
    t = pl.program_id(2)
    n = kvl_ref.shape[0]

    @pl.when(t == 0)
    def _():
        kc_scr[:, 0:LANES] = kvc_ref[:, 0:LANES]
        kc_scr[:, LANES:2 * LANES] = krc_ref[...]

        def copy(c, carry):
            rows = pl.ds(pl.multiple_of(c * tk, tk), tk)
            kl_scr[rows, 0:LANES] = kvl_ref[rows, 0:LANES]
            kl_scr[rows, LANES:2 * LANES] = krl_ref[rows, :]
            return carry
        lax.fori_loop(0, n // tk, copy, 0)

    q = q_ref[...]
    _softmax_init(_dot_nt(q, kc_scr[...]), kvc_ref[:, LANES:2 * LANES], m_scr, l_scr, acc_scr)

    @pl.when(t < lat_tiles)
    def _():
        def body(c, carry):
            rows = pl.ds(pl.multiple_of(c * tk, tk), tk)
            _softmax_step(_dot_nt(q, kl_scr[rows, :]), kvl_ref[rows, LANES:2 * LANES],
                          m_scr, l_scr, acc_scr)
            return carry
        lax.fori_loop(0, n // tk, body, 0)

    o_ref[...] = (acc_scr[...] / l_scr[...]).astype(o_ref.dtype)


def _mla_attention(q_all, kv_all, a0, kr_blk, geo, tq, tk):
    b, n, ctx = geo
    m_all = q_all.shape[0]
    h = MLA_HEADS
    lat_tiles = n // tq
    qrow = _q_row_block(geo, tq)
    cblk = b * n // ctx
    return pl.pallas_call(
        functools.partial(_mla_kernel, lat_tiles=lat_tiles, tk=tk),
        out_shape=jax.ShapeDtypeStruct((m_all, h * MLA_V), BF16),
        grid=(b, h, lat_tiles + ctx // tq),
        in_specs=[pl.BlockSpec((tq, 256), lambda bi, hi, t: (qrow(bi, t), hi)),
                  pl.BlockSpec((n, 256), lambda bi, hi, t: (bi, hi)),
                  pl.BlockSpec((ctx, 256), lambda bi, hi, t: (cblk + bi, hi)),
                  pl.BlockSpec((n, LANES), lambda bi, hi, t: (bi, kr_blk)),
                  pl.BlockSpec((ctx, LANES), lambda bi, hi, t: (cblk + bi, kr_blk))],
        out_specs=pl.BlockSpec((tq, MLA_V), lambda bi, hi, t: (qrow(bi, t), hi)),
        scratch_shapes=[pltpu.VMEM((n, 256), BF16), pltpu.VMEM((ctx, 256), BF16),
                        pltpu.VMEM((tq, LANES), F32), pltpu.VMEM((tq, LANES), F32),
                        pltpu.VMEM((tq, MLA_V), F32)],
        compiler_params=_cparams(("parallel", "arbitrary", "arbitrary")),
        name="mla_attention",
    )(q_all, kv_all, kv_all, a0, a0)


def _diff_kernel(q_ref, kl_ref, vl_ref, kc_ref, vc_ref, lam_ref, sub_ref, o_ref,
                 m0, l0, a0, m1, l1, a1, *, lat_tiles, tk, lam_init):
    t = pl.program_id(2)
    n = kl_ref.shape[0]
    hd = DIFF_HD
    q = q_ref[...]
    stats = ((m0, l0, a0), (m1, l1, a1))
    for i in range(2):
        _softmax_init(_dot_nt(q[:, i * hd:(i + 1) * hd], kc_ref[:, i * hd:(i + 1) * hd]),
                      vc_ref[...], *stats[i])

    @pl.when(t < lat_tiles)
    def _():
        def body(c, carry):
            rows = pl.ds(pl.multiple_of(c * tk, tk), tk)
            v = vl_ref[rows, :]
            for i in range(2):
                _softmax_step(_dot_nt(q[:, i * hd:(i + 1) * hd], kl_ref[rows, i * hd:(i + 1) * hd]),
                              v, *stats[i])
            return carry
        lax.fori_loop(0, n // tk, body, 0)

    lv = lam_ref[...]
    lam = (jnp.exp(jnp.sum(lv[0:1] * lv[1:2], axis=-1, keepdims=True))
           - jnp.exp(jnp.sum(lv[2:3] * lv[3:4], axis=-1, keepdims=True)) + lam_init)
    o = a0[...] / _rep(l0[...], 2) - lam * (a1[...] / _rep(l1[...], 2))
    o = o * lax.rsqrt(jnp.mean(o * o, axis=-1, keepdims=True) + EPS) * sub_ref[...]
    o_ref[...] = (o * (1.0 - lam_init)).astype(o_ref.dtype)


def _diff_attention(proj, lam_rows, subln, geo, tq, tk, lam_init):
    b, n, ctx = geo
    m_all = proj.shape[0]
    h = DIFF_HEADS
    lat_tiles = n // tq
    qrow = _q_row_block(geo, tq)
    cblk = b * n // ctx
    qo, ko, vo = h, 2 * h, 3 * h
    return pl.pallas_call(
        functools.partial(_diff_kernel, lat_tiles=lat_tiles, tk=tk, lam_init=lam_init),
        out_shape=jax.ShapeDtypeStruct((m_all, h * 256), BF16),
        grid=(b, h, lat_tiles + ctx // tq),
        in_specs=[pl.BlockSpec((tq, 256), lambda bi, hi, t: (qrow(bi, t), qo + hi)),
                  pl.BlockSpec((n, 256), lambda bi, hi, t: (bi, ko + hi)),
                  pl.BlockSpec((n, 256), lambda bi, hi, t: (bi, vo + hi)),
                  pl.BlockSpec((ctx, 256), lambda bi, hi, t: (cblk + bi, ko + hi)),
                  pl.BlockSpec((ctx, 256), lambda bi, hi, t: (cblk + bi, vo + hi)),
                  pl.BlockSpec((8, LANES), lambda bi, hi, t: (0, 0)),
                  pl.BlockSpec((1, 256), lambda bi, hi, t: (0, 0))],
        out_specs=pl.BlockSpec((tq, 256), lambda bi, hi, t: (qrow(bi, t), hi)),
        scratch_shapes=[pltpu.VMEM((tq, LANES), F32), pltpu.VMEM((tq, LANES), F32),
                        pltpu.VMEM((tq, 256), F32)] * 2,
        compiler_params=_cparams(("parallel", "arbitrary", "arbitrary")),
        name="diff_attention",
    )(proj, proj, proj, proj, proj, lam_rows, subln.reshape(1, 256).astype(F32))


def _half_select(x, half):
    lane = lax.broadcasted_iota(jnp.int32, x.shape, 1)
    keep = (lane < 64) if half == 0 else (lane >= 64)
    return jnp.where(keep, x, jnp.zeros_like(x))


def _swa_kernel(sink_ref, q_ref, kl_ref, vl_ref, kc_ref, vc_ref, o_ref, *, lat_tiles, tq, band):
    t = pl.program_id(1)
    n = kl_ref.shape[0]
    per = SWA_HEADS // SWA_KV_HEADS

    def attend(is_lat):
        if is_lat:
            start = pl.multiple_of(jnp.clip(t * tq - SWA_WINDOW, 0, n - band), SWA_WINDOW)
            qpos = t * tq + lax.broadcasted_iota(jnp.int32, (per * tq, band), 0) % tq
            kpos = start + lax.broadcasted_iota(jnp.int32, (per * tq, band), 1)
            ok = jnp.abs(qpos - kpos) <= SWA_WINDOW
        for a in range(SWA_KV_HEADS // 2):
            cols = slice(a * LANES, (a + 1) * LANES)
            qs = jnp.concatenate([q_ref[:, (a * per + j) * LANES:(a * per + j + 1) * LANES]
                                  for j in range(per)], axis=0)
            kc, vc = kc_ref[:, cols], vc_ref[:, cols]
            if is_lat:
                kb, vb = kl_ref[pl.ds(start, band), cols], vl_ref[pl.ds(start, band), cols]
            out = None
            for half in range(2):
                sink = jnp.concatenate(
                    [jnp.full((tq, 1), sink_ref[(a * per + j) * 2 + half] * LOG2E, F32)
                     for j in range(per)], axis=0)
                s_c = _dot_nt(qs, _half_select(kc, half))
                m = jnp.maximum(jnp.max(s_c, axis=-1, keepdims=True), sink)
                if is_lat:
                    s_b = jnp.where(ok, _dot_nt(qs, _half_select(kb, half)), NEG_INF)
                    m = jnp.maximum(m, jnp.max(s_b, axis=-1, keepdims=True))
                p_c = jnp.exp2(s_c - m)
                l = jnp.sum(p_c, axis=-1, keepdims=True) + jnp.exp2(sink - m)
                o = _dot(p_c.astype(BF16), _half_select(vc, half))
                if is_lat:
                    p_b = jnp.exp2(s_b - m)
                    l = l + jnp.sum(p_b, axis=-1, keepdims=True)
                    o = o + _dot(p_b.astype(BF16), _half_select(vb, half))
                o = o / l
                out = o if out is None else out + o
            for j in range(per):
                o_ref[:, (a * per + j) * LANES:(a * per + j + 1) * LANES] = (
                    out[j * tq:(j + 1) * tq].astype(o_ref.dtype))

    pl.when(t < lat_tiles)(functools.partial(attend, True))
    pl.when(t >= lat_tiles)(functools.partial(attend, False))


def _swa_attention(proj, sink_perm, geo, tq):
    b, n, ctx = geo
    m_all = proj.shape[0]
    width = SWA_HEADS * SWA_HD
    lat_tiles = n // tq
    qrow = _q_row_block(geo, tq)
    cblk = b * n // ctx
    kvw = SWA_KV_HEADS * SWA_HD
    kblk, vblk = 2 * width // kvw, 2 * width // kvw + 1
    band = tq + 2 * SWA_WINDOW
    return pl.pallas_call(
        functools.partial(_swa_kernel, lat_tiles=lat_tiles, tq=tq, band=band),
        out_shape=jax.ShapeDtypeStruct((m_all, width), BF16),
        grid=(b, lat_tiles + ctx // tq),
        in_specs=[pl.BlockSpec(memory_space=pltpu.SMEM),
                  pl.BlockSpec((tq, width), lambda bi, t: (qrow(bi, t), 1)),
                  pl.BlockSpec((n, kvw), lambda bi, t: (bi, kblk)),
                  pl.BlockSpec((n, kvw), lambda bi, t: (bi, vblk)),
                  pl.BlockSpec((ctx, kvw), lambda bi, t: (cblk + bi, kblk)),
                  pl.BlockSpec((ctx, kvw), lambda bi, t: (cblk + bi, vblk))],
        out_specs=pl.BlockSpec((tq, width), lambda bi, t: (qrow(bi, t), 0)),
        compiler_params=_cparams(("parallel", "arbitrary")),
        name="swa_attention",
    )(sink_perm, proj, proj, proj, proj, proj)


def _na_kernel(q_ref, kl_ref, vl_ref, kc_ref, vc_ref, bt_ref, o_ref, s_scr,
               *, lat_tiles, rq, rk, rows):
    t = pl.program_id(2)
    w = GRID_W
    tq, band = rq * w, rk * w
    q = q_ref[...]

    def attend(is_lat):
        if is_lat:
            r0 = t * rq
            srow = jnp.clip(r0 - NA_WIN_R // 2, 0, rows - rk)
            start = pl.multiple_of(srow * w, w)
            kb, vb = kl_ref[pl.ds(start, band), :], vl_ref[pl.ds(start, band), :]
            qc = lax.broadcasted_iota(jnp.int32, (w, LANES), 0)
            lane = lax.broadcasted_iota(jnp.int32, (w, LANES), 1)
            kcol = lane % w
            cs = jnp.clip(qc - NA_WIN_C // 2, 0, w - NA_WIN_C)
            col_ok = (kcol >= cs) & (kcol < cs + NA_WIN_C)
            lo = lane < w
        out = None
        for half in range(2):
            s_c = _dot_nt(q, _half_select(kc_ref[...], half))
            m = jnp.max(s_c, axis=-1, keepdims=True)
            if is_lat:
                s_scr[...] = _dot_nt(q, _half_select(kb, half))
                for ri in range(rq):
                    r = r0 + ri
                    rs = jnp.clip(r - NA_WIN_R // 2, 0, rows - NA_WIN_R)
                    for kp in range(rk // 2):
                        kr0 = srow + 2 * kp
                        e = jnp.clip(kr0 - r + NA_WIN_R, 0, 2 * NA_WIN_R - 1)
                        ok0 = ((kr0 >= rs) & (kr0 < rs + NA_WIN_R)).astype(jnp.int32)
                        ok1 = ((kr0 + 1 >= rs) & (kr0 + 1 < rs + NA_WIN_R)).astype(jnp.int32)
                        valid = col_ok & (jnp.where(lo, ok0, ok1) > 0)
                        blk = (slice(ri * w, (ri + 1) * w), slice(kp * LANES, (kp + 1) * LANES))
                        s_scr[blk] = jnp.where(valid, s_scr[blk] + bt_ref[half, e] * LOG2E, NEG_INF)
                s_b = s_scr[...]
                m = jnp.maximum(m, jnp.max(s_b, axis=-1, keepdims=True))
            p_c = jnp.exp2(s_c - m)
            l = jnp.sum(p_c, axis=-1, keepdims=True)
            o = _dot(p_c.astype(BF16), _half_select(vc_ref[...], half))
            if is_lat:
                p_b = jnp.exp2(s_b - m)
                l = l + jnp.sum(p_b, axis=-1, keepdims=True)
                o = o + _dot(p_b.astype(BF16), _half_select(vb, half))
            o = o / l
            out = o if out is None else out + o
        o_ref[...] = out.astype(o_ref.dtype)

    pl.when(t < lat_tiles)(functools.partial(attend, True))
    pl.when(t >= lat_tiles)(functools.partial(attend, False))


def _na_attention(proj, bias_tab, geo, rq):
    b, n, ctx = geo
    m_all = proj.shape[0]
    width = NA_HEADS * NA_HD
    pairs = width // LANES
    tq = rq * GRID_W
    rk = rq + NA_WIN_R
    rows = n // GRID_W
    assert rows >= rk and ctx % tq == 0 and n % tq == 0
    lat_tiles = n // tq
    qrow = _q_row_block(geo, tq)
    cblk = b * n // ctx
    qo, ko, vo = pairs, 2 * pairs, 3 * pairs
    return pl.pallas_call(
        functools.partial(_na_kernel, lat_tiles=lat_tiles, rq=rq, rk=rk, rows=rows),
        out_shape=jax.ShapeDtypeStruct((m_all, width), BF16),
        grid=(b, pairs, lat_tiles + ctx // tq),
        in_specs=[pl.BlockSpec((tq, LANES), lambda bi, pi, t: (qrow(bi, t), qo + pi)),
                  pl.BlockSpec((n, LANES), lambda bi, pi, t: (bi, ko + pi)),
                  pl.BlockSpec((n, LANES), lambda bi, pi, t: (bi, vo + pi)),
                  pl.BlockSpec((ctx, LANES), lambda bi, pi, t: (cblk + bi, ko + pi)),
                  pl.BlockSpec((ctx, LANES), lambda bi, pi, t: (cblk + bi, vo + pi)),
                  pl.BlockSpec((2, 2 * NA_WIN_R, GRID_W, LANES), lambda bi, pi, t: (pi, 0, 0, 0))],
        out_specs=pl.BlockSpec((tq, LANES), lambda bi, pi, t: (qrow(bi, t), pi)),
        scratch_shapes=[pltpu.VMEM((tq, rk * GRID_W), F32)],
        compiler_params=_cparams(("parallel", "arbitrary", "arbitrary")),
        name="na_attention",
    )(proj, proj, proj, proj, proj, bias_tab)


def _rope_tables(n, d_rot, tm):
    t = jnp.arange(n)
    row = (t // GRID_W).astype(F32)
    col = (t % GRID_W).astype(F32)
    d_axis = d_rot // 2
    inv = ROPE_BASE ** (-jnp.arange(0, d_axis, 2, dtype=F32) / d_axis)
    ang = jnp.concatenate([row[:, None] * inv, col[:, None] * inv], axis=-1)
    cos, sin = jnp.cos(ang), jnp.sin(ang)
    reps = LANES // d_rot
    cos_t = jnp.tile(cos, (1, 2 * reps))
    sin_t = jnp.tile(jnp.concatenate([-sin, sin], axis=-1), (1, reps))
    cos_t = jnp.concatenate([cos_t, jnp.ones((tm, LANES), F32)], axis=0)
    sin_t = jnp.concatenate([sin_t, jnp.zeros((tm, LANES), F32)], axis=0)
    return cos_t, sin_t


def _na_bias_table(rpb):
    cq = jnp.arange(GRID_W)
    dc = jnp.clip(cq[None, :] - cq[:, None], -(NA_WIN_C - 1), NA_WIN_C - 1) + (NA_WIN_C - 1)
    bt = rpb[:, :, dc]
    bt = jnp.pad(bt, ((0, 0), (1, 1), (0, 0), (0, 0)))
    return jnp.concatenate([bt[:, :-1], bt[:, 1:]], axis=-1).astype(F32)


def kernel(x, c, ctx, c_ctx, l0_mod_w, l0_mod_b, l0_norm, l0_w_in, l0_q_norm, l0_w_qb, l0_kv_norm, l0_w_kvb, l0_w_out, l1_mod_w, l1_mod_b, l1_norm, l1_w_in, l1_sink, l1_w_out, l2_mod_w, l2_mod_b, l2_norm, l2_w_in, l2_rpb, l2_w_out, l3_mod_w, l3_mod_b, l3_norm, l3_w_in, l3_lam_q1, l3_lam_k1, l3_lam_q2, l3_lam_k2, l3_subln, l3_w_out, final_norm):
    b, n, d = x.shape
    n_ctx = ctx.shape[1]
    geo = (b, n, n_ctx)
    tm = min(512, n_ctx * b)
    tq = min(256, n_ctx)
    tk = min(512, n)
    tn_d = min(1024, d)
    assert n % tm == 0 and (b * n_ctx) % tm == 0 and n_ctx % tq == 0 and n % tk == 0 and b < 8

    x_all = jnp.concatenate([x.reshape(b * n, d), ctx.reshape(b * n_ctx, d)], axis=0)
    c_rows = jnp.concatenate([c, c_ctx[None, :], jnp.zeros((8 - b - 1, d), F32)], axis=0)
    rope64 = _rope_tables(n, 64, tm)
    rope128 = _rope_tables(n, 128, tm)

    def groups(*spans):
        out = []
        for cols, code, scale in spans:
            out += [(code, scale)] * (cols // LANES)
        return tuple(out)

    mod = _modulation(c_rows, l0_mod_w, l0_mod_b)
    r = MLA_RANK
    w_in = jnp.concatenate([l0_w_in[:, 2 * r + MLA_ROPE:], l0_w_in[:, :2 * r + MLA_ROPE],
                            jnp.zeros((d, LANES - MLA_ROPE), F32)], axis=1).astype(BF16)
    width = MLA_HEADS * MLA_V
    g0 = groups((width + 2 * r, PLAIN, 1.0), (LANES, ROPE64, 1.0))
    a0 = _projection(x_all, 0, d, l0_norm, mod, w_in, rope64, g0, geo, tm, 640, "mla_in")
    qk = MLA_NOPE + MLA_ROPE
    w_q = l0_w_qb.reshape(r, MLA_HEADS, qk)
    w_q = jnp.pad(w_q, ((0, 0), (0, 0), (0, 256 - qk))).reshape(r, MLA_HEADS * 256).astype(BF16)
    q_scale = qk ** -0.5 * LOG2E
    gq = groups((LANES, PLAIN, q_scale), (LANES, ROPE64, q_scale)) * MLA_HEADS
    q0 = _projection(a0, width // r, r, l0_q_norm, None, w_q, rope64, gq, geo, tm, 1024, "mla_q")
    gkv = groups((MLA_HEADS * 256, PLAIN, 1.0))
    kv0 = _projection(a0, width // r + 1, r, l0_kv_norm, None, l0_w_kvb.astype(BF16), None, gkv,
                      geo, tm, 1024, "mla_kv")
    o = _mla_attention(q0, kv0, a0, (width + 2 * r) // LANES, geo, tq, tk)
    x_all = _out_projection(o, a0, l0_w_out.astype(BF16), x_all, mod, geo, tm, tn_d, "mla_out")

    mod = _modulation(c_rows, l1_mod_w, l1_mod_b)
    per = SWA_HEADS // SWA_KV_HEADS
    order = [g * per + j for a in range(SWA_KV_HEADS // 2) for j in range(per)
             for g in (2 * a, 2 * a + 1)]
    col_perm = jnp.asarray([hh * SWA_HD + e for hh in order for e in range(SWA_HD)])
    width = SWA_HEADS * SWA_HD
    kvw = SWA_KV_HEADS * SWA_HD
    w_q, w_k, w_v, w_g = (l1_w_in[:, :width], l1_w_in[:, width:width + kvw],
                          l1_w_in[:, width + kvw:width + 2 * kvw], l1_w_in[:, width + 2 * kvw:])
    w_in = jnp.concatenate([w_g[:, col_perm], w_q[:, col_perm], w_k, w_v], axis=1).astype(BF16)
    g1 = groups((width, PLAIN, 1.0), (width, ROPE64, SWA_HD ** -0.5 * LOG2E),
                (kvw, ROPE64, 1.0), (kvw, PLAIN, 1.0))
    a1 = _projection(x_all, 0, d, l1_norm, mod, w_in, rope64, g1, geo, tm, 1152, "swa_in")
    o = _swa_attention(a1, l1_sink[jnp.asarray(order)], geo, min(128, n_ctx))
    x_all = _out_projection(o, a1, l1_w_out[col_perm, :].astype(BF16), x_all, mod, geo, tm, tn_d,
                            "swa_out")

    mod = _modulation(c_rows, l2_mod_w, l2_mod_b)
    width = NA_HEADS * NA_HD
    w_in = jnp.concatenate([l2_w_in[:, 3 * width:], l2_w_in[:, :3 * width]], axis=1).astype(BF16)
    g2 = groups((width, PLAIN, 1.0), (width, PLAIN, NA_HD ** -0.5 * LOG2E), (2 * width, PLAIN, 1.0))
    a2 = _projection(x_all, 0, d, l2_norm, mod, w_in, None, g2, geo, tm, 1024, "na_in")
    o = _na_attention(a2, _na_bias_table(l2_rpb), geo, 4)
    x_all = _out_projection(o, a2, l2_w_out.astype(BF16), x_all, mod, geo, tm, tn_d, "na_out")

    mod = _modulation(c_rows, l3_mod_w, l3_mod_b)
    width = DIFF_HEADS * 2 * DIFF_HD
    w_in = jnp.concatenate([l3_w_in[:, 3 * width:], l3_w_in[:, :3 * width]], axis=1).astype(BF16)
    g3 = groups((width, PLAIN, 1.0), (width, ROPE128, DIFF_HD ** -0.5 * LOG2E),
                (width, ROPE128, 1.0), (width, PLAIN, 1.0))
    a3 = _projection(x_all, 0, d, l3_norm, mod, w_in, rope128, g3, geo, tm, 1024, "diff_in")
    lam_init = 0.8 - 0.6 * math.exp(-0.3 * 3)
    lam_rows = jnp.concatenate([jnp.stack([l3_lam_q1, l3_lam_k1, l3_lam_q2, l3_lam_k2]),
                                jnp.zeros((4, DIFF_HD), F32)], axis=0)
    o = _diff_attention(a3, lam_rows, l3_subln, geo, tq, tk, lam_init)
    x_all = _out_projection(o, a3, l3_w_out.astype(BF16), x_all, mod, geo, tm, tn_d, "diff_out")

    return _final_norm(x_all, final_norm, b * n, tm).reshape(b, n, d)
```

```python
import functools
import math

import jax
import jax.numpy as jnp
from jax import lax
from jax.experimental import pallas as pl
from jax.experimental.pallas import tpu as pltpu

F32 = jnp.float32
BF16 = jnp.bfloat16

LANES = 128
LOG2E = 1.4426950408889634
NEG_INF = -1e30
EPS = 1e-6
ROPE_BASE = 10000.0
GRID_W = 64
VMEM_LIMIT = 56 * 1024 * 1024

MLA_HEADS, MLA_NOPE, MLA_ROPE, MLA_V, MLA_RANK = 16, 128, 64, 128, 512
SWA_HEADS, SWA_KV_HEADS, SWA_HD, SWA_WINDOW = 32, 4, 64, 128
NA_HEADS, NA_HD, NA_WIN_R, NA_WIN_C = 32, 64, 8, 16
DIFF_HEADS, DIFF_HD = 8, 128

PLAIN, ROPE64, ROPE128 = 0, 1, 2


def _cparams(sem):
    return pltpu.CompilerParams(dimension_semantics=sem, vmem_limit_bytes=VMEM_LIMIT)


def _dot_nt(a, b):
    return lax.dot_general(a, b, (((1,), (1,)), ((), ())), preferred_element_type=F32)


def _dot(a, b):
    return jnp.dot(a, b, preferred_element_type=F32)


def _rep(x, n):
    return x if n == 1 else jnp.concatenate([x] * n, axis=1)


def _largest_divisor(total, candidates):
    return next(c for c in candidates if total % c == 0)


def _mod_kernel(c_ref, w_ref, b_ref, o_ref):
    c = c_ref[...]
    s = c * jax.nn.sigmoid(c)
    o_ref[...] = _dot(s, w_ref[...]) + b_ref[...]


def _modulation(c_rows, mod_w, mod_b):
    r, d = c_rows.shape
    n = mod_w.shape[1]
    tn = math.gcd(n, 512)
    out = pl.pallas_call(
        _mod_kernel,
        out_shape=jax.ShapeDtypeStruct((r, n), F32),
        grid=(n // tn,),
        in_specs=[pl.BlockSpec((r, d), lambda j: (0, 0)),
                  pl.BlockSpec((d, tn), lambda j: (0, j)),
                  pl.BlockSpec((1, tn), lambda j: (0, j))],
        out_specs=pl.BlockSpec((r, tn), lambda j: (0, j)),
        compiler_params=_cparams(("arbitrary",)),
        name="modulation",
    )(c_rows, mod_w, mod_b.reshape(1, n))
    return out.reshape(r, 1, n)


def _mod_row(group):
    kind, val = group
    return (lambda i: i // val) if kind == "batch" else (lambda i: val)


def _rope_group(a, cos, sin, code):
    if code == ROPE64:
        lane = lax.broadcasted_iota(jnp.int32, a.shape, 1)
        first = (lane % 64) < 32
        partner = jnp.where(first, pltpu.roll(a, 96, 1), pltpu.roll(a, 32, 1))
    else:
        partner = pltpu.roll(a, 64, 1)
    return a * cos + partner * sin


def _proj_kernel(*refs, has_mod, has_rope, tile_patterns, d_mod):
    it = iter(refs)
    x_ref, g_ref = next(it), next(it)
    mod_ref = next(it) if has_mod else None
    w_ref = next(it)
    cos_ref, sin_ref = (next(it), next(it)) if has_rope else (None, None)
    o_ref, h_scr = next(it), next(it)
    j = pl.program_id(1)

    @pl.when(j == 0)
    def _():
        x = x_ref[...].astype(F32)
        y = x * lax.rsqrt(jnp.mean(x * x, axis=-1, keepdims=True) + EPS) * g_ref[...]
        if has_mod:
            shift = mod_ref[0, :, 0:d_mod]
            scale = mod_ref[0, :, d_mod:2 * d_mod]
            y = y * (1.0 + scale) + shift
        h_scr[...] = y.astype(BF16)

    def epilogue(pattern):
        acc = _dot(h_scr[...], w_ref[...])
        for gi, (code, scale) in enumerate(pattern):
            a = acc[:, gi * LANES:(gi + 1) * LANES]
            if code != PLAIN:
                a = _rope_group(a, cos_ref[...], sin_ref[...], code)
            if scale != 1.0:
                a = a * scale
            o_ref[:, gi * LANES:(gi + 1) * LANES] = a.astype(o_ref.dtype)

    distinct = sorted(set(tile_patterns), key=tile_patterns.index)
    if len(distinct) == 1:
        epilogue(distinct[0])
    else:
        for pat in distinct:
            tiles = [jj for jj, p in enumerate(tile_patterns) if p == pat]
            cond = functools.reduce(jnp.logical_or, [j == jj for jj in tiles])
            pl.when(cond)(functools.partial(epilogue, pat))


def _projection(x, col_blk, kd, gain, mod, group, w, rope, groups, tm, tn, name):
    m = x.shape[0]
    n_out = w.shape[1]
    tm = min(tm, m)
    if group is not None and group[0] == "const":
        name += "_ctx"
    assert m % tm == 0 and n_out % tn == 0 and tn % LANES == 0 and len(groups) == n_out // LANES
    if rope is None:
        groups = tuple((PLAIN, scale) for _, scale in groups)
    has_mod = mod is not None
    has_rope = any(code != PLAIN for code, _ in groups)
    gpt = tn // LANES
    tile_patterns = tuple(tuple(groups[t * gpt:(t + 1) * gpt]) for t in range(n_out // tn))

    in_specs = [pl.BlockSpec((tm, kd), lambda i, j: (i, col_blk)),
                pl.BlockSpec((1, kd), lambda i, j: (0, 0))]
    args = [x, gain.reshape(1, kd).astype(F32)]
    d_mod = 0
    if has_mod:
        d_mod = mod.shape[2] // 3
        row = _mod_row(group)
        in_specs.append(pl.BlockSpec((1, 1, 3 * d_mod), lambda i, j: (row(i), 0, 0)))
        args.append(mod)
    in_specs.append(pl.BlockSpec((kd, tn), lambda i, j: (0, j)))
    args.append(w)
    if has_rope:
        cos, sin, tiles_per_batch = rope
        in_specs += [pl.BlockSpec((tm, LANES), lambda i, j: (i % tiles_per_batch, 0))] * 2
        args += [cos, sin]
    kern = functools.partial(_proj_kernel, has_mod=has_mod, has_rope=has_rope,
                             tile_patterns=tile_patterns, d_mod=d_mod)
    return pl.pallas_call(
        kern,
        out_shape=jax.ShapeDtypeStruct((m, n_out), BF16),
        grid=(m // tm, n_out // tn),
        in_specs=in_specs,
        out_specs=pl.BlockSpec((tm, tn), lambda i, j: (i, j)),
        scratch_shapes=[pltpu.VMEM((tm, kd), BF16)],
        compiler_params=_cparams(("parallel", "arbitrary")),
        name=name,
    )(*args)


def _out_kernel(o_ref, gate_ref, w_ref, x_ref, gm_ref, y_ref, u_scr):
    @pl.when(pl.program_id(1) == 0)
    def _():
        g = gate_ref[...].astype(F32)
        u_scr[...] = (o_ref[...].astype(F32) * (g * jax.nn.sigmoid(g))).astype(BF16)

    y_ref[...] = x_ref[...] + gm_ref[0] * _dot(u_scr[...], w_ref[...])


def _out_projection(o, proj, w_out, x, mod, group, tm, tn, name):
    m, d = x.shape
    width = w_out.shape[0]
    tm = min(tm, m)
    if group[0] == "const":
        name += "_ctx"
    row = _mod_row(group)
    gate_blk = 2 * d // tn
    return pl.pallas_call(
        _out_kernel,
        out_shape=jax.ShapeDtypeStruct((m, d), F32),
        grid=(m // tm, d // tn),
        in_specs=[pl.BlockSpec((tm, width), lambda i, j: (i, 0)),
                  pl.BlockSpec((tm, width), lambda i, j: (i, 0)),
                  pl.BlockSpec((width, tn), lambda i, j: (0, j)),
                  pl.BlockSpec((tm, tn), lambda i, j: (i, j)),
                  pl.BlockSpec((1, 1, tn), lambda i, j: (row(i), 0, gate_blk + j))],
        out_specs=pl.BlockSpec((tm, tn), lambda i, j: (i, j)),
        scratch_shapes=[pltpu.VMEM((tm, width), BF16)],
        input_output_aliases={3: 0},
        compiler_params=_cparams(("parallel", "arbitrary")),
        name=name,
    )(o, proj, w_out, x, mod)


def _norm_kernel(x_ref, g_ref, o_ref):
    x = x_ref[...]
    o_ref[...] = x * lax.rsqrt(jnp.mean(x * x, axis=-1, keepdims=True) + EPS) * g_ref[...]


def _final_norm(x, gain, tm):
    m, d = x.shape
    return pl.pallas_call(
        _norm_kernel,
        out_shape=jax.ShapeDtypeStruct((m, d), F32),
        grid=(m // tm,),
        in_specs=[pl.BlockSpec((tm, d), lambda i: (i, 0)),
                  pl.BlockSpec((1, d), lambda i: (0, 0))],
        out_specs=pl.BlockSpec((tm, d), lambda i: (i, 0)),
        compiler_params=_cparams(("parallel",)),
        name="final_norm",
    )(x, gain.reshape(1, d))


def _flash(n_chunks, scores_t, values_t, stats, parts, unroll):
    for m_scr, l_scr, acc in stats:
        m_scr[...] = jnp.full(m_scr.shape, NEG_INF, F32)
        l_scr[...] = jnp.zeros(l_scr.shape, F32)
        acc[...] = jnp.zeros(acc.shape, F32)

    def body(c, carry):
        vt = values_t(c)
        for cols in parts:
            for st, (m_scr, l_scr, acc) in zip(scores_t(c, cols), stats):
                m_prev = m_scr[:, cols]
                m_new = jnp.maximum(m_prev, jnp.max(st, axis=0, keepdims=True))
                alpha = jnp.exp2(m_prev - m_new)
                pt = jnp.exp2(st - m_new)
                l_scr[:, cols] = alpha * l_scr[:, cols] + jnp.sum(pt, axis=0, keepdims=True)
                m_scr[:, cols] = m_new
                acc[:, cols] = alpha * acc[:, cols] + _dot(vt, pt.astype(BF16))
        return carry

    lax.fori_loop(0, n_chunks, body, 0, unroll=unroll)


def _transpose_chunks(dst, src, tk):
    def body(c, carry):
        rows = pl.ds(pl.multiple_of(c * tk, tk), tk)
        dst[c] = src[rows, :].astype(F32).T.astype(dst.dtype)
        return carry
    lax.fori_loop(0, dst.shape[0], body, 0)


def _softmax_once(s, v):
    m = jnp.max(s, axis=-1, keepdims=True)
    p = jnp.exp2(s - m)
    return _dot(p.astype(BF16), v) / jnp.sum(p, axis=-1, keepdims=True)


def _gather_rows(dst, src_ctx, src_lat, dst_cols, src_cols, ctx, step):
    dst[0:ctx, dst_cols] = src_ctx[:, src_cols]

    def copy(c, carry):
        r = pl.multiple_of(c * step, step)
        dst[pl.ds(ctx + r, step), dst_cols] = src_lat[pl.ds(r, step), src_cols]
        return carry
    lax.fori_loop(0, src_lat.shape[0] // step, copy, 0)


def _mla_kernel(q_ref, kvl_ref, kvc_ref, krl_ref, krc_ref, o_ref,
                k_scr, v_scr, vt_scr, m_scr, l_scr, acc_scr, *, tk, ctx, parts, unroll):
    lo, hi = slice(0, LANES), slice(LANES, 2 * LANES)

    @pl.when(pl.program_id(2) == 0)
    def _():
        _gather_rows(k_scr, kvc_ref, kvl_ref, lo, lo, ctx, ctx)
        _gather_rows(k_scr, krc_ref, krl_ref, hi, lo, ctx, ctx)
        _gather_rows(v_scr, kvc_ref, kvl_ref, lo, hi, ctx, ctx)
        _transpose_chunks(vt_scr, v_scr, tk)

    def chunk(c):
        return pl.ds(pl.multiple_of(c * tk, tk), tk)

    _flash(vt_scr.shape[0],
           lambda c, cols: (_dot_nt(k_scr[chunk(c), :], q_ref[cols, :]),),
           lambda c: vt_scr[c],
           [(m_scr, l_scr, acc_scr)], parts, unroll)
    o_ref[...] = (acc_scr[...] / l_scr[...]).T.astype(o_ref.dtype)


def _row_parts(tq, n_parts):
    step = tq // n_parts
    return tuple(slice(i * step, (i + 1) * step) for i in range(n_parts))


def _mla_attention(q_lat, kv_lat, kv_ctx, a_lat, a_ctx, kr_blk, geo, tq, tk, n_parts, unroll):
    b, n, ctx = geo
    h = MLA_HEADS
    nt = n // tq
    return pl.pallas_call(
        functools.partial(_mla_kernel, tk=tk, ctx=ctx, parts=_row_parts(tq, n_parts),
                          unroll=unroll),
        out_shape=jax.ShapeDtypeStruct((b * n, h * MLA_V), BF16),
        grid=(b, h, nt),
        in_specs=[pl.BlockSpec((tq, 256), lambda bi, hi, t: (bi * nt + t, hi)),
                  pl.BlockSpec((n, 256), lambda bi, hi, t: (bi, hi)),
                  pl.BlockSpec((ctx, 256), lambda bi, hi, t: (bi, hi)),
                  pl.BlockSpec((n, LANES), lambda bi, hi, t: (bi, kr_blk)),
                  pl.BlockSpec((ctx, LANES), lambda bi, hi, t: (bi, kr_blk))],
        out_specs=pl.BlockSpec((tq, MLA_V), lambda bi, hi, t: (bi * nt + t, hi)),
        scratch_shapes=[pltpu.VMEM((ctx + n, 256), BF16), pltpu.VMEM((ctx + n, MLA_V), BF16),
                        pltpu.VMEM(((ctx + n) // tk, MLA_V, tk), BF16),
                        pltpu.VMEM((1, tq), F32), pltpu.VMEM((1, tq), F32),
                        pltpu.VMEM((MLA_V, tq), F32)],
        compiler_params=_cparams(("parallel", "arbitrary", "arbitrary")),
        name="mla_attention",
    )(q_lat, kv_lat, kv_ctx, a_lat, a_ctx)


def _mla_ctx_kernel(q_ref, kv_ref, kr_ref, o_ref):
    q = q_ref[...]
    s = _dot_nt(q[:, 0:LANES], kv_ref[:, 0:LANES]) + _dot_nt(q[:, LANES:2 * LANES], kr_ref[...])
    o_ref[...] = _softmax_once(s, kv_ref[:, LANES:2 * LANES]).astype(o_ref.dtype)


def _mla_ctx_attention(q_ctx, kv_ctx, a_ctx, kr_blk, geo):
    b, n, ctx = geo
    h = MLA_HEADS
    return pl.pallas_call(
        _mla_ctx_kernel,
        out_shape=jax.ShapeDtypeStruct((b * ctx, h * MLA_V), BF16),
        grid=(b, h),
        in_specs=[pl.BlockSpec((ctx, 256), lambda bi, hi: (bi, hi)),
                  pl.BlockSpec((ctx, 256), lambda bi, hi: (bi, hi)),
                  pl.BlockSpec((ctx, LANES), lambda bi, hi: (bi, kr_blk))],
        out_specs=pl.BlockSpec((ctx, MLA_V), lambda bi, hi: (bi, hi)),
        compiler_params=_cparams(("parallel", "arbitrary")),
        name="mla_ctx_attention",
    )(q_ctx, kv_ctx, a_ctx)


def _diff_kernel(q_ref, kl_ref, vl_ref, kc_ref, vc_ref, lam_ref, sub_ref, o_ref,
                 k_scr, v_scr, vt_scr, m0, l0, a0, m1, l1, a1,
                 *, tk, ctx, lam_init, parts, unroll):
    hd = DIFF_HD
    full = slice(0, 2 * hd)

    @pl.when(pl.program_id(2) == 0)
    def _():
        _gather_rows(k_scr, kc_ref, kl_ref, full, full, ctx, ctx)
        _gather_rows(v_scr, vc_ref, vl_ref, full, full, ctx, ctx)
        _transpose_chunks(vt_scr, v_scr, tk)

    def chunk(c):
        return pl.ds(pl.multiple_of(c * tk, tk), tk)

    def scores_t(c, cols):
        return tuple(_dot_nt(k_scr[chunk(c), i * hd:(i + 1) * hd], q_ref[cols, i * hd:(i + 1) * hd])
                     for i in range(2))

    _flash(vt_scr.shape[0], scores_t, lambda c: vt_scr[c],
           [(m0, l0, a0), (m1, l1, a1)], parts, unroll)

    lv = lam_ref[...]
    lam = (jnp.exp(jnp.sum(lv[0:1] * lv[1:2], axis=-1, keepdims=True))
           - jnp.exp(jnp.sum(lv[2:3] * lv[3:4], axis=-1, keepdims=True)) + lam_init)
    o = (a0[...] / l0[...] - lam * (a1[...] / l1[...])).T
    o = o * lax.rsqrt(jnp.mean(o * o, axis=-1, keepdims=True) + EPS) * sub_ref[...]
    o_ref[...] = (o * (1.0 - lam_init)).astype(o_ref.dtype)


def _diff_attention(a_lat, kv_ctx, lam_rows, subln, geo, tq, tk, n_parts, unroll, lam_init):
    b, n, ctx = geo
    h = DIFF_HEADS
    nt = n // tq
    return pl.pallas_call(
        functools.partial(_diff_kernel, tk=tk, ctx=ctx, lam_init=lam_init,
                          parts=_row_parts(tq, n_parts), unroll=unroll),
        out_shape=jax.ShapeDtypeStruct((b * n, h * 256), BF16),
        grid=(b, h, nt),
        in_specs=[pl.BlockSpec((tq, 256), lambda bi, hi, t: (bi * nt + t, h + hi)),
                  pl.BlockSpec((n, 256), lambda bi, hi, t: (bi, 2 * h + hi)),
                  pl.BlockSpec((n, 256), lambda bi, hi, t: (bi, 3 * h + hi)),
                  pl.BlockSpec((ctx, 256), lambda bi, hi, t: (bi, hi)),
                  pl.BlockSpec((ctx, 256), lambda bi, hi, t: (bi, h + hi)),
                  pl.BlockSpec((8, LANES), lambda bi, hi, t: (0, 0)),
                  pl.BlockSpec((1, 256), lambda bi, hi, t: (0, 0))],
        out_specs=pl.BlockSpec((tq, 256), lambda bi, hi, t: (bi * nt + t, hi)),
        scratch_shapes=[pltpu.VMEM((ctx + n, 256), BF16), pltpu.VMEM((ctx + n, 256), BF16),
                        pltpu.VMEM(((ctx + n) // tk, 256, tk), BF16)]
        + [pltpu.VMEM((1, tq), F32), pltpu.VMEM((1, tq), F32), pltpu.VMEM((256, tq), F32)] * 2,
        compiler_params=_cparams(("parallel", "arbitrary", "arbitrary")),
        name="diff_attention",
    )(a_lat, a_lat, a_lat, kv_ctx, kv_ctx, lam_rows, subln.reshape(1, 256).astype(F32))


def _half_select(x, half):
    lane = lax.broadcasted_iota(jnp.int32, x.shape, 1)
    keep = (lane < 64) if half == 0 else (lane >= 64)
    return jnp.where(keep, x, jnp.zeros_like(x))


def _swa_kernel(*refs, is_lat, tq, band):
    if is_lat:
        sink_ref, q_ref, kl_ref, vl_ref, kc_ref, vc_ref, o_ref = refs
    else:
        sink_ref, q_ref, kc_ref, vc_ref, o_ref = refs
    t = pl.program_id(1)
    per = SWA_HEADS // SWA_KV_HEADS
    if is_lat:
        n = kl_ref.shape[0]
        start = pl.multiple_of(jnp.clip(t * tq - SWA_WINDOW, 0, n - band), SWA_WINDOW)
        qpos = t * tq + lax.broadcasted_iota(jnp.int32, (per * tq, band), 0) % tq
        kpos = start + lax.broadcasted_iota(jnp.int32, (per * tq, band), 1)
        ok = jnp.abs(qpos - kpos) <= SWA_WINDOW
    for a in range(SWA_KV_HEADS // 2):
        cols = slice(a * LANES, (a + 1) * LANES)
        qs = jnp.concatenate([q_ref[:, (a * per + j) * LANES:(a * per + j + 1) * LANES]
                              for j in range(per)], axis=0)
        kc, vc = kc_ref[:, cols], vc_ref[:, cols]
        if is_lat:
            kb, vb = kl_ref[pl.ds(start, band), cols], vl_ref[pl.ds(start, band), cols]
        out = None
        for half in range(2):
            sink = jnp.concatenate(
                [jnp.full((tq, 1), sink_ref[(a * per + j) * 2 + half] * LOG2E, F32)
                 for j in range(per)], axis=0)
            s_c = _dot_nt(qs, _half_select(kc, half))
            m = jnp.maximum(jnp.max(s_c, axis=-1, keepdims=True), sink)
            if is_lat:
                s_b = jnp.where(ok, _dot_nt(qs, _half_select(kb, half)), NEG_INF)
                m = jnp.maximum(m, jnp.max(s_b, axis=-1, keepdims=True))
            p_c = jnp.exp2(s_c - m)
            l = jnp.sum(p_c, axis=-1, keepdims=True) + jnp.exp2(sink - m)
            o = _dot(p_c.astype(BF16), _half_select(vc, half))
            if is_lat:
                p_b = jnp.exp2(s_b - m)
                l = l + jnp.sum(p_b, axis=-1, keepdims=True)
                o = o + _dot(p_b.astype(BF16), _half_select(vb, half))
            o = o / l
            out = o if out is None else out + o
        for j in range(per):
            o_ref[:, (a * per + j) * LANES:(a * per + j + 1) * LANES] = (
                out[j * tq:(j + 1) * tq].astype(o_ref.dtype))


def _swa_attention(a_q, a_lat, a_ctx, sink_perm, geo, tq, is_lat):
    b, n, ctx = geo
    width = SWA_HEADS * SWA_HD
    kvw = SWA_KV_HEADS * SWA_HD
    rows = n if is_lat else ctx
    nt = rows // tq
    kblk, vblk = 2 * width // kvw, 2 * width // kvw + 1
    in_specs = [pl.BlockSpec(memory_space=pltpu.SMEM),
                pl.BlockSpec((tq, width), lambda bi, t: (bi * nt + t, 1))]
    args = [sink_perm, a_q]
    if is_lat:
        in_specs += [pl.BlockSpec((n, kvw), lambda bi, t: (bi, kblk)),
                     pl.BlockSpec((n, kvw), lambda bi, t: (bi, vblk))]
        args += [a_lat, a_lat]
    in_specs += [pl.BlockSpec((ctx, kvw), lambda bi, t: (bi, kblk)),
                 pl.BlockSpec((ctx, kvw), lambda bi, t: (bi, vblk))]
    args += [a_ctx, a_ctx]
    return pl.pallas_call(
        functools.partial(_swa_kernel, is_lat=is_lat, tq=tq, band=tq + 2 * SWA_WINDOW),
        out_shape=jax.ShapeDtypeStruct((b * rows, width), BF16),
        grid=(b, nt),
        in_specs=in_specs,
        out_specs=pl.BlockSpec((tq, width), lambda bi, t: (bi * nt + t, 0)),
        compiler_params=_cparams(("parallel", "arbitrary")),
        name="swa_attention" if is_lat else "swa_ctx_attention",
    )(*args)


def _na_kernel(*refs, is_lat, rq, rk, rows):
    if is_lat:
        q_ref, kl_ref, vl_ref, kc_ref, vc_ref, bt_ref, o_ref, s_scr = refs
    else:
        q_ref, kc_ref, vc_ref, o_ref = refs
    t = pl.program_id(2)
    w = GRID_W
    band = rk * w
    q = q_ref[...]
    if is_lat:
        r0 = t * rq
        srow = jnp.clip(r0 - NA_WIN_R // 2, 0, rows - rk)
        start = pl.multiple_of(srow * w, w)
        kb, vb = kl_ref[pl.ds(start, band), :], vl_ref[pl.ds(start, band), :]
        qc = lax.broadcasted_iota(jnp.int32, (w, LANES), 0)
        lane = lax.broadcasted_iota(jnp.int32, (w, LANES), 1)
        kcol = lane % w
        cs = jnp.clip(qc - NA_WIN_C // 2, 0, w - NA_WIN_C)
        col_ok = (kcol >= cs) & (kcol < cs + NA_WIN_C)
        lo = lane < w
    out = None
    for half in range(2):
        s_c = _dot_nt(q, _half_select(kc_ref[...], half))
        m = jnp.max(s_c, axis=-1, keepdims=True)
        if is_lat:
            s_scr[...] = _dot_nt(q, _half_select(kb, half))
            for ri in range(rq):
                r = r0 + ri
                rs = jnp.clip(r - NA_WIN_R // 2, 0, rows - NA_WIN_R)
                for kp in range(rk // 2):
                    kr0 = srow + 2 * kp
                    e = jnp.clip(kr0 - r + NA_WIN_R, 0, 2 * NA_WIN_R - 1)
                    ok0 = ((kr0 >= rs) & (kr0 < rs + NA_WIN_R)).astype(jnp.int32)
                    ok1 = ((kr0 + 1 >= rs) & (kr0 + 1 < rs + NA_WIN_R)).astype(jnp.int32)
                    valid = col_ok & (jnp.where(lo, ok0, ok1) > 0)
                    blk = (slice(ri * w, (ri + 1) * w), slice(kp * LANES, (kp + 1) * LANES))
                    s_scr[blk] = jnp.where(valid, s_scr[blk] + bt_ref[half, e] * LOG2E, NEG_INF)
            s_b = s_scr[...]
            m = jnp.maximum(m, jnp.max(s_b, axis=-1, keepdims=True))
        p_c = jnp.exp2(s_c - m)
        l = jnp.sum(p_c, axis=-1, keepdims=True)
        o = _dot(p_c.astype(BF16), _half_select(vc_ref[...], half))
        if is_lat:
            p_b = jnp.exp2(s_b - m)
            l = l + jnp.sum(p_b, axis=-1, keepdims=True)
            o = o + _dot(p_b.astype(BF16), _half_select(vb, half))
        o = o / l
        out = o if out is None else out + o
    o_ref[...] = out.astype(o_ref.dtype)


def _na_attention(a_q, a_lat, a_ctx, bias_tab, geo, rq, is_lat):
    b, n, ctx = geo
    width = NA_HEADS * NA_HD
    pairs = width // LANES
    tq = rq * GRID_W
    rk = rq + NA_WIN_R
    grid_rows = n // GRID_W
    rows = n if is_lat else ctx
    assert grid_rows >= rk and rows % tq == 0
    nt = rows // tq
    qo, ko, vo = pairs, 2 * pairs, 3 * pairs
    in_specs = [pl.BlockSpec((tq, LANES), lambda bi, pi, t: (bi * nt + t, qo + pi))]
    args = [a_q]
    if is_lat:
        in_specs += [pl.BlockSpec((n, LANES), lambda bi, pi, t: (bi, ko + pi)),
                     pl.BlockSpec((n, LANES), lambda bi, pi, t: (bi, vo + pi))]
        args += [a_lat, a_lat]
    in_specs += [pl.BlockSpec((ctx, LANES), lambda bi, pi, t: (bi, ko + pi)),
                 pl.BlockSpec((ctx, LANES), lambda bi, pi, t: (bi, vo + pi))]
    args += [a_ctx, a_ctx]
    scratch = []
    if is_lat:
        in_specs.append(pl.BlockSpec((2, 2 * NA_WIN_R, GRID_W, LANES),
                                     lambda bi, pi, t: (pi, 0, 0, 0)))
        args.append(bias_tab)
        scratch = [pltpu.VMEM((tq, rk * GRID_W), F32)]
    return pl.pallas_call(
        functools.partial(_na_kernel, is_lat=is_lat, rq=rq, rk=rk, rows=grid_rows),
        out_shape=jax.ShapeDtypeStruct((b * rows, width), BF16),
        grid=(b, pairs, nt),
        in_specs=in_specs,
        out_specs=pl.BlockSpec((tq, LANES), lambda bi, pi, t: (bi * nt + t, pi)),
        scratch_shapes=scratch,
        compiler_params=_cparams(("parallel", "arbitrary", "arbitrary")),
        name="na_attention" if is_lat else "na_ctx_attention",
    )(*args)


def _rope_tables(n, d_rot):
    t = jnp.arange(n)
    row = (t // GRID_W).astype(F32)
    col = (t % GRID_W).astype(F32)
    d_axis = d_rot // 2
    inv = ROPE_BASE ** (-jnp.arange(0, d_axis, 2, dtype=F32) / d_axis)
    ang = jnp.concatenate([row[:, None] * inv, col[:, None] * inv], axis=-1)
    cos, sin = jnp.cos(ang), jnp.sin(ang)
    reps = LANES // d_rot
    return (jnp.tile(cos, (1, 2 * reps)),
            jnp.tile(jnp.concatenate([-sin, sin], axis=-1), (1, reps)))


def _na_bias_table(rpb):
    cq = jnp.arange(GRID_W)
    dc = jnp.clip(cq[None, :] - cq[:, None], -(NA_WIN_C - 1), NA_WIN_C - 1) + (NA_WIN_C - 1)
    bt = rpb[:, :, dc]
    bt = jnp.pad(bt, ((0, 0), (1, 1), (0, 0), (0, 0)))
    return jnp.concatenate([bt[:, :-1], bt[:, 1:]], axis=-1).astype(F32)


def _tile_plan(b, n, n_ctx, d):
    tm = _largest_divisor(math.gcd(n, b * n_ctx), (512, 256, 128))
    return dict(
        tm=tm,
        tn_d=_largest_divisor(d, (1024, 512, 256, 128)),
        tq_flash=_largest_divisor(n, (512, 256, 128)),
        parts_flash=1,
        tq_diff=_largest_divisor(n, (512, 256, 128)),
        parts_diff=1,
        tk=_largest_divisor(n + n_ctx, (768, 640, 512, 384, 256, 128)),
        unroll=1,
        tq_swa=128,
        rq_na=4,
    )


def kernel(x, c, ctx, c_ctx, l0_mod_w, l0_mod_b, l0_norm, l0_w_in, l0_q_norm, l0_w_qb, l0_kv_norm, l0_w_kvb, l0_w_out, l1_mod_w, l1_mod_b, l1_norm, l1_w_in, l1_sink, l1_w_out, l2_mod_w, l2_mod_b, l2_norm, l2_w_in, l2_rpb, l2_w_out, l3_mod_w, l3_mod_b, l3_norm, l3_w_in, l3_lam_q1, l3_lam_k1, l3_lam_q2, l3_lam_k2, l3_subln, l3_w_out, final_norm):
    b, n, d = x.shape
    n_ctx = ctx.shape[1]
    geo = (b, n, n_ctx)
    assert b < 8 and n % GRID_W == 0
    tp = _tile_plan(b, n, n_ctx, d)
    tm, tn_d = tp["tm"], tp["tn_d"]
    lat, cx = ("batch", n // tm), ("const", b)

    x_lat = x.reshape(b * n, d)
    x_ctx = ctx.reshape(b * n_ctx, d)
    c_rows = jnp.concatenate([c, c_ctx[None, :], jnp.zeros((8 - b - 1, d), F32)], axis=0)
    rope64 = _rope_tables(n, 64) + (n // tm,)
    rope128 = _rope_tables(n, 128) + (n // tm,)

    def groups(*spans):
        out = []
        for cols, code, scale in spans:
            out += [(code, scale)] * (cols // LANES)
        return tuple(out)

    def both(fn):
        return fn(x_lat, lat, True), fn(x_ctx, cx, False)

    mod = _modulation(c_rows, l0_mod_w, l0_mod_b)
    r = MLA_RANK
    w_in = jnp.concatenate([l0_w_in[:, 2 * r + MLA_ROPE:], l0_w_in[:, :2 * r + MLA_ROPE],
                            jnp.zeros((d, LANES - MLA_ROPE), F32)], axis=1).astype(BF16)
    width = MLA_HEADS * MLA_V
    g0 = groups((width + 2 * r, PLAIN, 1.0), (LANES, ROPE64, 1.0))
    a_lat, a_ctx = both(lambda xs, grp, is_lat: _projection(
        xs, 0, d, l0_norm, mod, grp, w_in, rope64 if is_lat else None, g0, tm, 640, "mla_in"))
    qk = MLA_NOPE + MLA_ROPE
    w_q = l0_w_qb.reshape(r, MLA_HEADS, qk)
    w_q = jnp.pad(w_q, ((0, 0), (0, 0), (0, 256 - qk))).reshape(r, MLA_HEADS * 256).astype(BF16)
    q_scale = qk ** -0.5 * LOG2E
    gq = groups((LANES, PLAIN, q_scale), (LANES, ROPE64, q_scale)) * MLA_HEADS
    gkv = groups((MLA_HEADS * 256, PLAIN, 1.0))
    w_kv = l0_w_kvb.astype(BF16)
    q_lat, kv_lat = (
        _projection(a_lat, width // r, r, l0_q_norm, None, None, w_q, rope64, gq, tm, 1024, "mla_q"),
        _projection(a_lat, width // r + 1, r, l0_kv_norm, None, None, w_kv, None, gkv, tm, 1024,
                    "mla_kv"))
    q_ctx, kv_ctx = (
        _projection(a_ctx, width // r, r, l0_q_norm, None, None, w_q, None, gq, tm, 1024,
                    "mla_q_ctx"),
        _projection(a_ctx, width // r + 1, r, l0_kv_norm, None, None, w_kv, None, gkv, tm, 1024,
                    "mla_kv_ctx"))
    kr_blk = (width + 2 * r) // LANES
    o_lat = _mla_attention(q_lat, kv_lat, kv_ctx, a_lat, a_ctx, kr_blk, geo, tp["tq_flash"],
                           tp["tk"], tp["parts_flash"], tp["unroll"])
    o_ctx = _mla_ctx_attention(q_ctx, kv_ctx, a_ctx, kr_blk, geo)
    w_out = l0_w_out.astype(BF16)
    x_lat = _out_projection(o_lat, a_lat, w_out, x_lat, mod, lat, tm, tn_d, "mla_out")
    x_ctx = _out_projection(o_ctx, a_ctx, w_out, x_ctx, mod, cx, tm, tn_d, "mla_out")

    mod = _modulation(c_rows, l1_mod_w, l1_mod_b)
    per = SWA_HEADS // SWA_KV_HEADS
    order = [g * per + j for a in range(SWA_KV_HEADS // 2) for j in range(per)
             for g in (2 * a, 2 * a + 1)]
    col_perm = jnp.asarray([hh * SWA_HD + e for hh in order for e in range(SWA_HD)])
    width = SWA_HEADS * SWA_HD
    kvw = SWA_KV_HEADS * SWA_HD
    w_q, w_k, w_v, w_g = (l1_w_in[:, :width], l1_w_in[:, width:width + kvw],
                          l1_w_in[:, width + kvw:width + 2 * kvw], l1_w_in[:, width + 2 * kvw:])
    w_in = jnp.concatenate([w_g[:, col_perm], w_q[:, col_perm], w_k, w_v], axis=1).astype(BF16)
    g1 = groups((width, PLAIN, 1.0), (width, ROPE64, SWA_HD ** -0.5 * LOG2E),
                (kvw, ROPE64, 1.0), (kvw, PLAIN, 1.0))
    a_lat, a_ctx = both(lambda xs, grp, is_lat: _projection(
        xs, 0, d, l1_norm, mod, grp, w_in, rope64 if is_lat else None, g1, tm, 1152, "swa_in"))
    sink_perm = l1_sink[jnp.asarray(order)]
    tq = tp["tq_swa"]
    o_lat = _swa_attention(a_lat, a_lat, a_ctx, sink_perm, geo, tq, True)
    o_ctx = _swa_attention(a_ctx, a_lat, a_ctx, sink_perm, geo, tq, False)
    w_out = l1_w_out[col_perm, :].astype(BF16)
    x_lat = _out_projection(o_lat, a_lat, w_out, x_lat, mod, lat, tm, tn_d, "swa_out")
    x_ctx = _out_projection(o_ctx, a_ctx, w_out, x_ctx, mod, cx, tm, tn_d, "swa_out")

    mod = _modulation(c_rows, l2_mod_w, l2_mod_b)
    width = NA_HEADS * NA_HD
    w_in = jnp.concatenate([l2_w_in[:, 3 * width:], l2_w_in[:, :3 * width]], axis=1).astype(BF16)
    g2 = groups((width, PLAIN, 1.0), (width, PLAIN, NA_HD ** -0.5 * LOG2E), (2 * width, PLAIN, 1.0))
    a_lat, a_ctx = both(lambda xs, grp, is_lat: _projection(
        xs, 0, d, l2_norm, mod, grp, w_in, None, g2, tm, 1024, "na_in"))
    bias_tab = _na_bias_table(l2_rpb)
    o_lat = _na_attention(a_lat, a_lat, a_ctx, bias_tab, geo, tp["rq_na"], True)
    o_ctx = _na_attention(a_ctx, a_lat, a_ctx, bias_tab, geo, tp["rq_na"], False)
    w_out = l2_w_out.astype(BF16)
    x_lat = _out_projection(o_lat, a_lat, w_out, x_lat, mod, lat, tm, tn_d, "na_out")
    x_ctx = _out_projection(o_ctx, a_ctx, w_out, x_ctx, mod, cx, tm, tn_d, "na_out")

    mod = _modulation(c_rows, l3_mod_w, l3_mod_b)
    width = DIFF_HEADS * 2 * DIFF_HD
    w_in = jnp.concatenate([l3_w_in[:, 3 * width:], l3_w_in[:, :3 * width]], axis=1).astype(BF16)
    g3 = groups((width, PLAIN, 1.0), (width, ROPE128, DIFF_HD ** -0.5 * LOG2E),
                (width, ROPE128, 1.0), (width, PLAIN, 1.0))
    a_lat = _projection(x_lat, 0, d, l3_norm, mod, lat, w_in, rope128, g3, tm, 1024, "diff_in")
    kv_ctx = _projection(x_ctx, 0, d, l3_norm, mod, cx, w_in[:, 2 * width:], None, g3[2 * width // LANES:],
                         tm, 1024, "diff_in")
    lam_init = 0.8 - 0.6 * math.exp(-0.3 * 3)
    lam_rows = jnp.concatenate([jnp.stack([l3_lam_q1, l3_lam_k1, l3_lam_q2, l3_lam_k2]),
                                jnp.zeros((4, DIFF_HD), F32)], axis=0)
    o_lat = _diff_attention(a_lat, kv_ctx, lam_rows, l3_subln, geo, tp["tq_diff"], tp["tk"],
                            tp["parts_diff"], tp["unroll"], lam_init)
    x_lat = _out_projection(o_lat, a_lat, l3_w_out.astype(BF16), x_lat, mod, lat, tm, tn_d, "diff_out")

    return _final_norm(x_lat, final_norm, tm).reshape(b, n, d)
```

```python
import functools
import math

import jax
import jax.numpy as jnp
from jax import lax
from jax.experimental import pallas as pl
from jax.experimental.pallas import tpu as pltpu

F32 = jnp.float32
BF16 = jnp.bfloat16

LANES = 128
LOG2E = 1.4426950408889634
NEG_INF = -1e30
EPS = 1e-6
ROPE_BASE = 10000.0
GRID_W = 64
VMEM_LIMIT = 56 * 1024 * 1024

MLA_HEADS, MLA_NOPE, MLA_ROPE, MLA_V, MLA_RANK = 16, 128, 64, 128, 512
SWA_HEADS, SWA_KV_HEADS, SWA_HD, SWA_WINDOW = 32, 4, 64, 128
NA_HEADS, NA_HD, NA_WIN_R, NA_WIN_C = 32, 64, 8, 16
DIFF_HEADS, DIFF_HD = 8, 128

PLAIN, ROPE64, ROPE128 = 0, 1, 2


def _cparams(sem):
    return pltpu.CompilerParams(dimension_semantics=sem, vmem_limit_bytes=VMEM_LIMIT)


def _dot_nt(a, b):
    return lax.dot_general(a, b, (((1,), (1,)), ((), ())), preferred_element_type=F32)


def _dot(a, b):
    return jnp.dot(a, b, preferred_element_type=F32)


def _rep(x, n):
    return x if n == 1 else jnp.concatenate([x] * n, axis=1)


def _largest_divisor(total, candidates):
    return next(c for c in candidates if total % c == 0)


def _mod_kernel(c_ref, w_ref, b_ref, o_ref):
    c = c_ref[...]
    s = c * jax.nn.sigmoid(c)
    o_ref[...] = _dot(s, w_ref[...]) + b_ref[...]


def _modulation(c_rows, mod_w, mod_b):
    r, d = c_rows.shape
    n = mod_w.shape[1]
    tn = math.gcd(n, 512)
    out = pl.pallas_call(
        _mod_kernel,
        out_shape=jax.ShapeDtypeStruct((r, n), F32),
        grid=(n // tn,),
        in_specs=[pl.BlockSpec((r, d), lambda j: (0, 0)),
                  pl.BlockSpec((d, tn), lambda j: (0, j)),
                  pl.BlockSpec((1, tn), lambda j: (0, j))],
        out_specs=pl.BlockSpec((r, tn), lambda j: (0, j)),
        compiler_params=_cparams(("arbitrary",)),
        name="modulation",
    )(c_rows, mod_w, mod_b.reshape(1, n))
    return out.reshape(r, 1, n)


def _mod_row(group):
    kind, val = group
    return (lambda i: i // val) if kind == "batch" else (lambda i: val)


def _rope_group(a, cos, sin, code):
    if code == ROPE64:
        lane = lax.broadcasted_iota(jnp.int32, a.shape, 1)
        first = (lane % 64) < 32
        partner = jnp.where(first, pltpu.roll(a, 96, 1), pltpu.roll(a, 32, 1))
    else:
        partner = pltpu.roll(a, 64, 1)
    return a * cos + partner * sin


def _proj_kernel(*refs, has_mod, has_rope, tile_patterns, d_mod):
    it = iter(refs)
    x_ref, g_ref = next(it), next(it)
    mod_ref = next(it) if has_mod else None
    w_ref = next(it)
    cos_ref, sin_ref = (next(it), next(it)) if has_rope else (None, None)
    o_ref, h_scr = next(it), next(it)
    j = pl.program_id(1)

    @pl.when(j == 0)
    def _():
        x = x_ref[...].astype(F32)
        y = x * lax.rsqrt(jnp.mean(x * x, axis=-1, keepdims=True) + EPS) * g_ref[...]
        if has_mod:
            shift = mod_ref[0, :, 0:d_mod]
            scale = mod_ref[0, :, d_mod:2 * d_mod]
            y = y * (1.0 + scale) + shift
        h_scr[...] = y.astype(BF16)

    def epilogue(pattern):
        acc = _dot(h_scr[...], w_ref[...])
        for gi, (code, scale) in enumerate(pattern):
            a = acc[:, gi * LANES:(gi + 1) * LANES]
            if code != PLAIN:
                a = _rope_group(a, cos_ref[...], sin_ref[...], code)
            if scale != 1.0:
                a = a * scale
            o_ref[:, gi * LANES:(gi + 1) * LANES] = a.astype(o_ref.dtype)

    distinct = sorted(set(tile_patterns), key=tile_patterns.index)
    if len(distinct) == 1:
        epilogue(distinct[0])
    else:
        for pat in distinct:
            tiles = [jj for jj, p in enumerate(tile_patterns) if p == pat]
            cond = functools.reduce(jnp.logical_or, [j == jj for jj in tiles])
            pl.when(cond)(functools.partial(epilogue, pat))


def _projection(x, col_blk, kd, gain, mod, group, w, rope, groups, tm, tn, name):
    m = x.shape[0]
    n_out = w.shape[1]
    tm = min(tm, m)
    if group is not None and group[0] == "const":
        name += "_ctx"
    assert m % tm == 0 and n_out % tn == 0 and tn % LANES == 0 and len(groups) == n_out // LANES
    if rope is None:
        groups = tuple((PLAIN, scale) for _, scale in groups)
    has_mod = mod is not None
    has_rope = any(code != PLAIN for code, _ in groups)
    gpt = tn // LANES
    tile_patterns = tuple(tuple(groups[t * gpt:(t + 1) * gpt]) for t in range(n_out // tn))

    in_specs = [pl.BlockSpec((tm, kd), lambda i, j: (i, col_blk)),
                pl.BlockSpec((1, kd), lambda i, j: (0, 0))]
    args = [x, gain.reshape(1, kd).astype(F32)]
    d_mod = 0
    if has_mod:
        d_mod = mod.shape[2] // 3
        row = _mod_row(group)
        in_specs.append(pl.BlockSpec((1, 1, 3 * d_mod), lambda i, j: (row(i), 0, 0)))
        args.append(mod)
    in_specs.append(pl.BlockSpec((kd, tn), lambda i, j: (0, j)))
    args.append(w)
    if has_rope:
        cos, sin, tiles_per_batch = rope
        in_specs += [pl.BlockSpec((tm, LANES), lambda i, j: (i % tiles_per_batch, 0))] * 2
        args += [cos, sin]
    kern = functools.partial(_proj_kernel, has_mod=has_mod, has_rope=has_rope,
                             tile_patterns=tile_patterns, d_mod=d_mod)
    return pl.pallas_call(
        kern,
        out_shape=jax.ShapeDtypeStruct((m, n_out), BF16),
        grid=(m // tm, n_out // tn),
        in_specs=in_specs,
        out_specs=pl.BlockSpec((tm, tn), lambda i, j: (i, j)),
        scratch_shapes=[pltpu.VMEM((tm, kd), BF16)],
        compiler_params=_cparams(("parallel", "arbitrary")),
        name=name,
    )(*args)


def _out_kernel(o_ref, gate_ref, w_ref, x_ref, gm_ref, y_ref, u_scr):
    @pl.when(pl.program_id(1) == 0)
    def _():
        g = gate_ref[...].astype(F32)
        u_scr[...] = (o_ref[...].astype(F32) * (g * jax.nn.sigmoid(g))).astype(BF16)

    y_ref[...] = x_ref[...] + gm_ref[0] * _dot(u_scr[...], w_ref[...])


def _out_projection(o, proj, w_out, x, mod, group, tm, tn, name):
    m, d = x.shape
    width = w_out.shape[0]
    tm = min(tm, m)
    if group[0] == "const":
        name += "_ctx"
    row = _mod_row(group)
    gate_blk = 2 * d // tn
    return pl.pallas_call(
        _out_kernel,
        out_shape=jax.ShapeDtypeStruct((m, d), F32),
        grid=(m // tm, d // tn),
        in_specs=[pl.BlockSpec((tm, width), lambda i, j: (i, 0)),
                  pl.BlockSpec((tm, width), lambda i, j: (i, 0)),
                  pl.BlockSpec((width, tn), lambda i, j: (0, j)),
                  pl.BlockSpec((tm, tn), lambda i, j: (i, j)),
                  pl.BlockSpec((1, 1, tn), lambda i, j: (row(i), 0, gate_blk + j))],
        out_specs=pl.BlockSpec((tm, tn), lambda i, j: (i, j)),
        scratch_shapes=[pltpu.VMEM((tm, width), BF16)],
        input_output_aliases={3: 0},
        compiler_params=_cparams(("parallel", "arbitrary")),
        name=name,
    )(o, proj, w_out, x, mod)


def _norm_kernel(x_ref, g_ref, o_ref):
    x = x_ref[...]
    o_ref[...] = x * lax.rsqrt(jnp.mean(x * x, axis=-1, keepdims=True) + EPS) * g_ref[...]


def _final_norm(x, gain, tm):
    m, d = x.shape
    return pl.pallas_call(
        _norm_kernel,
        out_shape=jax.ShapeDtypeStruct((m, d), F32),
        grid=(m // tm,),
        in_specs=[pl.BlockSpec((tm, d), lambda i: (i, 0)),
                  pl.BlockSpec((1, d), lambda i: (0, 0))],
        out_specs=pl.BlockSpec((tm, d), lambda i: (i, 0)),
        compiler_params=_cparams(("parallel",)),
        name="final_norm",
    )(x, gain.reshape(1, d))


def _flash(n_chunks, scores_t, values_t, stats, bufs):
    for m_scr, l_scr, acc in stats:
        m_scr[...] = jnp.full(m_scr.shape, NEG_INF, F32)
        l_scr[...] = jnp.zeros(l_scr.shape, F32)
        acc[...] = jnp.zeros(acc.shape, F32)

    def load_scores(c, slot):
        for s, (st, _, _) in zip(scores_t(c), bufs):
            st[slot] = s

    def softmax(slot):
        for (m_scr, l_scr, _), (st, pt, alpha_buf) in zip(stats, bufs):
            s = st[slot]
            m_prev = m_scr[...]
            m_new = jnp.maximum(m_prev, jnp.max(s, axis=0, keepdims=True))
            alpha = jnp.exp2(m_prev - m_new)
            p = jnp.exp2(s - m_new)
            l_scr[...] = alpha * l_scr[...] + jnp.sum(p, axis=0, keepdims=True)
            m_scr[...] = m_new
            alpha_buf[slot] = alpha
            pt[slot] = p.astype(pt.dtype)

    def weighted_values(c, slot):
        vt = values_t(c)
        for (_, _, acc), (_, pt, alpha_buf) in zip(stats, bufs):
            acc[...] = alpha_buf[slot] * acc[...] + _dot(vt, pt[slot])

    def step(c, cur, first, last):
        if not last:
            load_scores(c + 1, 1 - cur)
        if not first:
            weighted_values(c - 1, 1 - cur)
        softmax(cur)

    load_scores(0, 0)
    step(0, 0, True, n_chunks == 1)
    pairs = max(n_chunks - 2, 0) // 2

    def body(i, carry):
        c = 1 + 2 * i
        step(c, 1, False, False)
        step(c + 1, 0, False, False)
        return carry
    lax.fori_loop(0, pairs, body, 0)
    for c in range(1 + 2 * pairs, n_chunks):
        step(c, c % 2, False, c == n_chunks - 1)
    weighted_values(n_chunks - 1, (n_chunks - 1) % 2)


def _transpose_chunks(dst, src, tk):
    def body(c, carry):
        rows = pl.ds(pl.multiple_of(c * tk, tk), tk)
        dst[c] = src[rows, :].astype(F32).T.astype(dst.dtype)
        return carry
    lax.fori_loop(0, dst.shape[0], body, 0)


def _softmax_once(s, v):
    m = jnp.max(s, axis=-1, keepdims=True)
    p = jnp.exp2(s - m)
    return _dot(p.astype(BF16), v) / jnp.sum(p, axis=-1, keepdims=True)


def _gather_rows(dst, src_ctx, src_lat, dst_cols, src_cols, ctx, step):
    dst[0:ctx, dst_cols] = src_ctx[:, src_cols]

    def copy(c, carry):
        r = pl.multiple_of(c * step, step)
        dst[pl.ds(ctx + r, step), dst_cols] = src_lat[pl.ds(r, step), src_cols]
        return carry
    lax.fori_loop(0, src_lat.shape[0] // step, copy, 0)


def _mla_kernel(q_ref, kvl_ref, kvc_ref, krl_ref, krc_ref, o_ref,
                k_scr, v_scr, vt_scr, qt_scr, m_scr, l_scr, acc_scr, st, pt, alpha, *, tk, ctx):
    lo, hi = slice(0, LANES), slice(LANES, 2 * LANES)

    @pl.when(pl.program_id(2) == 0)
    def _():
        _gather_rows(k_scr, kvc_ref, kvl_ref, lo, lo, ctx, ctx)
        _gather_rows(k_scr, krc_ref, krl_ref, hi, lo, ctx, ctx)
        _gather_rows(v_scr, kvc_ref, kvl_ref, lo, hi, ctx, ctx)
        _transpose_chunks(vt_scr, v_scr, tk)

    def chunk(c):
        return pl.ds(pl.multiple_of(c * tk, tk), tk)

    qt_scr[...] = q_ref[...].astype(F32).T.astype(BF16)
    _flash(vt_scr.shape[0],
           lambda c: (_dot(k_scr[chunk(c), :], qt_scr[...]),),
           lambda c: vt_scr[c],
           [(m_scr, l_scr, acc_scr)], [(st, pt, alpha)])
    o_ref[...] = (acc_scr[...] / l_scr[...]).T.astype(o_ref.dtype)


def _flash_buffers(tq, tk):
    return [pltpu.VMEM((2, tk, tq), F32), pltpu.VMEM((2, tk, tq), BF16),
            pltpu.VMEM((2, 1, tq), F32)]


def _mla_attention(q_lat, kv_lat, kv_ctx, a_lat, a_ctx, kr_blk, geo, tq, tk):
    b, n, ctx = geo
    h = MLA_HEADS
    nt = n // tq
    return pl.pallas_call(
        functools.partial(_mla_kernel, tk=tk, ctx=ctx),
        out_shape=jax.ShapeDtypeStruct((b * n, h * MLA_V), BF16),
        grid=(b, h, nt),
        in_specs=[pl.BlockSpec((tq, 256), lambda bi, hi, t: (bi * nt + t, hi)),
                  pl.BlockSpec((n, 256), lambda bi, hi, t: (bi, hi)),
                  pl.BlockSpec((ctx, 256), lambda bi, hi, t: (bi, hi)),
                  pl.BlockSpec((n, LANES), lambda bi, hi, t: (bi, kr_blk)),
                  pl.BlockSpec((ctx, LANES), lambda bi, hi, t: (bi, kr_blk))],
        out_specs=pl.BlockSpec((tq, MLA_V), lambda bi, hi, t: (bi * nt + t, hi)),
        scratch_shapes=[pltpu.VMEM((ctx + n, 256), BF16), pltpu.VMEM((ctx + n, MLA_V), BF16),
                        pltpu.VMEM(((ctx + n) // tk, MLA_V, tk), BF16),
                        pltpu.VMEM((256, tq), BF16),
                        pltpu.VMEM((1, tq), F32), pltpu.VMEM((1, tq), F32),
                        pltpu.VMEM((MLA_V, tq), F32)] + _flash_buffers(tq, tk),
        compiler_params=_cparams(("parallel", "arbitrary", "arbitrary")),
        name="mla_attention",
    )(q_lat, kv_lat, kv_ctx, a_lat, a_ctx)


def _mla_ctx_kernel(q_ref, kv_ref, kr_ref, o_ref):
    q = q_ref[...]
    s = _dot_nt(q[:, 0:LANES], kv_ref[:, 0:LANES]) + _dot_nt(q[:, LANES:2 * LANES], kr_ref[...])
    o_ref[...] = _softmax_once(s, kv_ref[:, LANES:2 * LANES]).astype(o_ref.dtype)


def _mla_ctx_attention(q_ctx, kv_ctx, a_ctx, kr_blk, geo):
    b, n, ctx = geo
    h = MLA_HEADS
    return pl.pallas_call(
        _mla_ctx_kernel,
        out_shape=jax.ShapeDtypeStruct((b * ctx, h * MLA_V), BF16),
        grid=(b, h),
        in_specs=[pl.BlockSpec((ctx, 256), lambda bi, hi: (bi, hi)),
                  pl.BlockSpec((ctx, 256), lambda bi, hi: (bi, hi)),
                  pl.BlockSpec((ctx, LANES), lambda bi, hi: (bi, kr_blk))],
        out_specs=pl.BlockSpec((ctx, MLA_V), lambda bi, hi: (bi, hi)),
        compiler_params=_cparams(("parallel", "arbitrary")),
        name="mla_ctx_attention",
    )(q_ctx, kv_ctx, a_ctx)


def _diff_kernel(q_ref, kl_ref, vl_ref, kc_ref, vc_ref, lam_ref, sub_ref, o_ref,
                 k_scr, v_scr, vt_scr, qt_scr, m0, l0, a0, m1, l1, a1,
                 st0, pt0, alpha0, st1, pt1, alpha1, *, tk, ctx, lam_init):
    hd = DIFF_HD
    full = slice(0, 2 * hd)

    @pl.when(pl.program_id(2) == 0)
    def _():
        _gather_rows(k_scr, kc_ref, kl_ref, full, full, ctx, ctx)
        _gather_rows(v_scr, vc_ref, vl_ref, full, full, ctx, ctx)
        _transpose_chunks(vt_scr, v_scr, tk)

    def chunk(c):
        return pl.ds(pl.multiple_of(c * tk, tk), tk)

    qt_scr[...] = q_ref[...].astype(F32).T.astype(BF16)

    def scores_t(c):
        return tuple(_dot(k_scr[chunk(c), i * hd:(i + 1) * hd], qt_scr[i * hd:(i + 1) * hd, :])
                     for i in range(2))

    _flash(vt_scr.shape[0], scores_t, lambda c: vt_scr[c],
           [(m0, l0, a0), (m1, l1, a1)], [(st0, pt0, alpha0), (st1, pt1, alpha1)])

    lv = lam_ref[...]
    lam = (jnp.exp(jnp.sum(lv[0:1] * lv[1:2], axis=-1, keepdims=True))
           - jnp.exp(jnp.sum(lv[2:3] * lv[3:4], axis=-1, keepdims=True)) + lam_init)
    o = (a0[...] / l0[...] - lam * (a1[...] / l1[...])).T
    o = o * lax.rsqrt(jnp.mean(o * o, axis=-1, keepdims=True) + EPS) * sub_ref[...]
    o_ref[...] = (o * (1.0 - lam_init)).astype(o_ref.dtype)


def _diff_attention(a_lat, kv_ctx, lam_rows, subln, geo, tq, tk, lam_init):
    b, n, ctx = geo
    h = DIFF_HEADS
    nt = n // tq
    return pl.pallas_call(
        functools.partial(_diff_kernel, tk=tk, ctx=ctx, lam_init=lam_init),
        out_shape=jax.ShapeDtypeStruct((b * n, h * 256), BF16),
        grid=(b, h, nt),
        in_specs=[pl.BlockSpec((tq, 256), lambda bi, hi, t: (bi * nt + t, h + hi)),
                  pl.BlockSpec((n, 256), lambda bi, hi, t: (bi, 2 * h + hi)),
                  pl.BlockSpec((n, 256), lambda bi, hi, t: (bi, 3 * h + hi)),
                  pl.BlockSpec((ctx, 256), lambda bi, hi, t: (bi, hi)),
                  pl.BlockSpec((ctx, 256), lambda bi, hi, t: (bi, h + hi)),
                  pl.BlockSpec((8, LANES), lambda bi, hi, t: (0, 0)),
                  pl.BlockSpec((1, 256), lambda bi, hi, t: (0, 0))],
        out_specs=pl.BlockSpec((tq, 256), lambda bi, hi, t: (bi * nt + t, hi)),
        scratch_shapes=[pltpu.VMEM((ctx + n, 256), BF16), pltpu.VMEM((ctx + n, 256), BF16),
                        pltpu.VMEM(((ctx + n) // tk, 256, tk), BF16),
                        pltpu.VMEM((256, tq), BF16)]
        + [pltpu.VMEM((1, tq), F32), pltpu.VMEM((1, tq), F32), pltpu.VMEM((256, tq), F32)] * 2
        + _flash_buffers(tq, tk) * 2,
        compiler_params=_cparams(("parallel", "arbitrary", "arbitrary")),
        name="diff_attention",
    )(a_lat, a_lat, a_lat, kv_ctx, kv_ctx, lam_rows, subln.reshape(1, 256).astype(F32))


def _half_select(x, half):
    lane = lax.broadcasted_iota(jnp.int32, x.shape, 1)
    keep = (lane < 64) if half == 0 else (lane >= 64)
    return jnp.where(keep, x, jnp.zeros_like(x))


def _swa_kernel(*refs, is_lat, tq, band):
    if is_lat:
        sink_ref, q_ref, kl_ref, vl_ref, kc_ref, vc_ref, o_ref = refs
    else:
        sink_ref, q_ref, kc_ref, vc_ref, o_ref = refs
    t = pl.program_id(1)
    per = SWA_HEADS // SWA_KV_HEADS
    if is_lat:
        n = kl_ref.shape[0]
        start = pl.multiple_of(jnp.clip(t * tq - SWA_WINDOW, 0, n - band), SWA_WINDOW)
        qpos = t * tq + lax.broadcasted_iota(jnp.int32, (per * tq, band), 0) % tq
        kpos = start + lax.broadcasted_iota(jnp.int32, (per * tq, band), 1)
        ok = jnp.abs(qpos - kpos) <= SWA_WINDOW
    for a in range(SWA_KV_HEADS // 2):
        cols = slice(a * LANES, (a + 1) * LANES)
        qs = jnp.concatenate([q_ref[:, (a * per + j) * LANES:(a * per + j + 1) * LANES]
                              for j in range(per)], axis=0)
        kc, vc = kc_ref[:, cols], vc_ref[:, cols]
        if is_lat:
            kb, vb = kl_ref[pl.ds(start, band), cols], vl_ref[pl.ds(start, band), cols]
        out = None
        for half in range(2):
            sink = jnp.concatenate(
                [jnp.full((tq, 1), sink_ref[(a * per + j) * 2 + half] * LOG2E, F32)
                 for j in range(per)], axis=0)
            s_c = _dot_nt(qs, _half_select(kc, half))
            m = jnp.maximum(jnp.max(s_c, axis=-1, keepdims=True), sink)
            if is_lat:
                s_b = jnp.where(ok, _dot_nt(qs, _half_select(kb, half)), NEG_INF)
                m = jnp.maximum(m, jnp.max(s_b, axis=-1, keepdims=True))
            p_c = jnp.exp2(s_c - m)
            l = jnp.sum(p_c, axis=-1, keepdims=True) + jnp.exp2(sink - m)
            o = _dot(p_c.astype(BF16), _half_select(vc, half))
            if is_lat:
                p_b = jnp.exp2(s_b - m)
                l = l + jnp.sum(p_b, axis=-1, keepdims=True)
                o = o + _dot(p_b.astype(BF16), _half_select(vb, half))
            o = o / l
            out = o if out is None else out + o
        for j in range(per):
            o_ref[:, (a * per + j) * LANES:(a * per + j + 1) * LANES] = (
                out[j * tq:(j + 1) * tq].astype(o_ref.dtype))


def _swa_attention(a_q, a_lat, a_ctx, sink_perm, geo, tq, is_lat):
    b, n, ctx = geo
    width = SWA_HEADS * SWA_HD
    kvw = SWA_KV_HEADS * SWA_HD
    rows = n if is_lat else ctx
    nt = rows // tq
    kblk, vblk = 2 * width // kvw, 2 * width // kvw + 1
    in_specs = [pl.BlockSpec(memory_space=pltpu.SMEM),
                pl.BlockSpec((tq, width), lambda bi, t: (bi * nt + t, 1))]
    args = [sink_perm, a_q]
    if is_lat:
        in_specs += [pl.BlockSpec((n, kvw), lambda bi, t: (bi, kblk)),
                     pl.BlockSpec((n, kvw), lambda bi, t: (bi, vblk))]
        args += [a_lat, a_lat]
    in_specs += [pl.BlockSpec((ctx, kvw), lambda bi, t: (bi, kblk)),
                 pl.BlockSpec((ctx, kvw), lambda bi, t: (bi, vblk))]
    args += [a_ctx, a_ctx]
    return pl.pallas_call(
        functools.partial(_swa_kernel, is_lat=is_lat, tq=tq, band=tq + 2 * SWA_WINDOW),
        out_shape=jax.ShapeDtypeStruct((b * rows, width), BF16),
        grid=(b, nt),
        in_specs=in_specs,
        out_specs=pl.BlockSpec((tq, width), lambda bi, t: (bi * nt + t, 0)),
        compiler_params=_cparams(("parallel", "arbitrary")),
        name="swa_attention" if is_lat else "swa_ctx_attention",
    )(*args)


def _na_kernel(*refs, is_lat, rq, rk, rows):
    if is_lat:
        q_ref, kl_ref, vl_ref, kc_ref, vc_ref, bt_ref, o_ref, s_scr = refs
    else:
        q_ref, kc_ref, vc_ref, o_ref = refs
    t = pl.program_id(2)
    w = GRID_W
    band = rk * w
    q = q_ref[...]
    if is_lat:
        r0 = t * rq
        srow = jnp.clip(r0 - NA_WIN_R // 2, 0, rows - rk)
        start = pl.multiple_of(srow * w, w)
        kb, vb = kl_ref[pl.ds(start, band), :], vl_ref[pl.ds(start, band), :]
        qc = lax.broadcasted_iota(jnp.int32, (w, LANES), 0)
        lane = lax.broadcasted_iota(jnp.int32, (w, LANES), 1)
        kcol = lane % w
        cs = jnp.clip(qc - NA_WIN_C // 2, 0, w - NA_WIN_C)
        col_ok = (kcol >= cs) & (kcol < cs + NA_WIN_C)
        lo = lane < w
    out = None
    for half in range(2):
        s_c = _dot_nt(q, _half_select(kc_ref[...], half))
        m = jnp.max(s_c, axis=-1, keepdims=True)
        if is_lat:
            s_scr[...] = _dot_nt(q, _half_select(kb, half))
            for ri in range(rq):
                r = r0 + ri
                rs = jnp.clip(r - NA_WIN_R // 2, 0, rows - NA_WIN_R)
                for kp in range(rk // 2):
                    kr0 = srow + 2 * kp
                    e = jnp.clip(kr0 - r + NA_WIN_R, 0, 2 * NA_WIN_R - 1)
                    ok0 = ((kr0 >= rs) & (kr0 < rs + NA_WIN_R)).astype(jnp.int32)
                    ok1 = ((kr0 + 1 >= rs) & (kr0 + 1 < rs + NA_WIN_R)).astype(jnp.int32)
                    valid = col_ok & (jnp.where(lo, ok0, ok1) > 0)
                    blk = (slice(ri * w, (ri + 1) * w), slice(kp * LANES, (kp + 1) * LANES))
                    s_scr[blk] = jnp.where(valid, s_scr[blk] + bt_ref[half, e] * LOG2E, NEG_INF)
            s_b = s_scr[...]
            m = jnp.maximum(m, jnp.max(s_b, axis=-1, keepdims=True))
        p_c = jnp.exp2(s_c - m)
        l = jnp.sum(p_c, axis=-1, keepdims=True)
        o = _dot(p_c.astype(BF16), _half_select(vc_ref[...], half))
        if is_lat:
            p_b = jnp.exp2(s_b - m)
            l = l + jnp.sum(p_b, axis=-1, keepdims=True)
            o = o + _dot(p_b.astype(BF16), _half_select(vb, half))
        o = o / l
        out = o if out is None else out + o
    o_ref[...] = out.astype(o_ref.dtype)


def _na_attention(a_q, a_lat, a_ctx, bias_tab, geo, rq, is_lat):
    b, n, ctx = geo
    width = NA_HEADS * NA_HD
    pairs = width // LANES
    tq = rq * GRID_W
    rk = rq + NA_WIN_R
    grid_rows = n // GRID_W
    rows = n if is_lat else ctx
    assert grid_rows >= rk and rows % tq == 0
    nt = rows // tq
    qo, ko, vo = pairs, 2 * pairs, 3 * pairs
    in_specs = [pl.BlockSpec((tq, LANES), lambda bi, pi, t: (bi * nt + t, qo + pi))]
    args = [a_q]
    if is_lat:
        in_specs += [pl.BlockSpec((n, LANES), lambda bi, pi, t: (bi, ko + pi)),
                     pl.BlockSpec((n, LANES), lambda bi, pi, t: (bi, vo + pi))]
        args += [a_lat, a_lat]
    in_specs += [pl.BlockSpec((ctx, LANES), lambda bi, pi, t: (bi, ko + pi)),
                 pl.BlockSpec((ctx, LANES), lambda bi, pi, t: (bi, vo + pi))]
    args += [a_ctx, a_ctx]
    scratch = []
    if is_lat:
        in_specs.append(pl.BlockSpec((2, 2 * NA_WIN_R, GRID_W, LANES),
                                     lambda bi, pi, t: (pi, 0, 0, 0)))
        args.append(bias_tab)
        scratch = [pltpu.VMEM((tq, rk * GRID_W), F32)]
    return pl.pallas_call(
        functools.partial(_na_kernel, is_lat=is_lat, rq=rq, rk=rk, rows=grid_rows),
        out_shape=jax.ShapeDtypeStruct((b * rows, width), BF16),
        grid=(b, pairs, nt),
        in_specs=in_specs,
        out_specs=pl.BlockSpec((tq, LANES), lambda bi, pi, t: (bi * nt + t, pi)),
        scratch_shapes=scratch,
        compiler_params=_cparams(("parallel", "arbitrary", "arbitrary")),
        name="na_attention" if is_lat else "na_ctx_attention",
    )(*args)


def _rope_tables(n, d_rot):
    t = jnp.arange(n)
    row = (t // GRID_W).astype(F32)
    col = (t % GRID_W).astype(F32)
    d_axis = d_rot // 2
    inv = ROPE_BASE ** (-jnp.arange(0, d_axis, 2, dtype=F32) / d_axis)
    ang = jnp.concatenate([row[:, None] * inv, col[:, None] * inv], axis=-1)
    cos, sin = jnp.cos(ang), jnp.sin(ang)
    reps = LANES // d_rot
    return (jnp.tile(cos, (1, 2 * reps)),
            jnp.tile(jnp.concatenate([-sin, sin], axis=-1), (1, reps)))


def _na_bias_table(rpb):
    cq = jnp.arange(GRID_W)
    dc = jnp.clip(cq[None, :] - cq[:, None], -(NA_WIN_C - 1), NA_WIN_C - 1) + (NA_WIN_C - 1)
    bt = rpb[:, :, dc]
    bt = jnp.pad(bt, ((0, 0), (1, 1), (0, 0), (0, 0)))
    return jnp.concatenate([bt[:, :-1], bt[:, 1:]], axis=-1).astype(F32)


def _tile_plan(b, n, n_ctx, d):
    tm = _largest_divisor(math.gcd(n, b * n_ctx), (512, 256, 128))
    return dict(
        tm=tm,
        tn_d=_largest_divisor(d, (1024, 512, 256, 128)),
        tq_flash=_largest_divisor(n, (512, 256, 128)),
        tq_diff=_largest_divisor(n, (512, 256, 128)),
        tk=_largest_divisor(n + n_ctx, (768, 640, 512, 384, 256, 128)),
        tq_swa=128,
        rq_na=4,
    )


def kernel(x, c, ctx, c_ctx, l0_mod_w, l0_mod_b, l0_norm, l0_w_in, l0_q_norm, l0_w_qb, l0_kv_norm, l0_w_kvb, l0_w_out, l1_mod_w, l1_mod_b, l1_norm, l1_w_in, l1_sink, l1_w_out, l2_mod_w, l2_mod_b, l2_norm, l2_w_in, l2_rpb, l2_w_out, l3_mod_w, l3_mod_b, l3_norm, l3_w_in, l3_lam_q1, l3_lam_k1, l3_lam_q2, l3_lam_k2, l3_subln, l3_w_out, final_norm):
    b, n, d = x.shape
    n_ctx = ctx.shape[1]
    geo = (b, n, n_ctx)
    assert b < 8 and n % GRID_W == 0
    tp = _tile_plan(b, n, n_ctx, d)
    tm, tn_d = tp["tm"], tp["tn_d"]
    lat, cx = ("batch", n // tm), ("const", b)

    x_lat = x.reshape(b * n, d)
    x_ctx = ctx.reshape(b * n_ctx, d)
    c_rows = jnp.concatenate([c, c_ctx[None, :], jnp.zeros((8 - b - 1, d), F32)], axis=0)
    rope64 = _rope_tables(n, 64) + (n // tm,)
    rope128 = _rope_tables(n, 128) + (n // tm,)

    def groups(*spans):
        out = []
        for cols, code, scale in spans:
            out += [(code, scale)] * (cols // LANES)
        return tuple(out)

    def both(fn):
        return fn(x_lat, lat, True), fn(x_ctx, cx, False)

    mod = _modulation(c_rows, l0_mod_w, l0_mod_b)
    r = MLA_RANK
    w_in = jnp.concatenate([l0_w_in[:, 2 * r + MLA_ROPE:], l0_w_in[:, :2 * r + MLA_ROPE],
                            jnp.zeros((d, LANES - MLA_ROPE), F32)], axis=1).astype(BF16)
    width = MLA_HEADS * MLA_V
    g0 = groups((width + 2 * r, PLAIN, 1.0), (LANES, ROPE64, 1.0))
    a_lat, a_ctx = both(lambda xs, grp, is_lat: _projection(
        xs, 0, d, l0_norm, mod, grp, w_in, rope64 if is_lat else None, g0, tm, 640, "mla_in"))
    qk = MLA_NOPE + MLA_ROPE
    w_q = l0_w_qb.reshape(r, MLA_HEADS, qk)
    w_q = jnp.pad(w_q, ((0, 0), (0, 0), (0, 256 - qk))).reshape(r, MLA_HEADS * 256).astype(BF16)
    q_scale = qk ** -0.5 * LOG2E
    gq = groups((LANES, PLAIN, q_scale), (LANES, ROPE64, q_scale)) * MLA_HEADS
    gkv = groups((MLA_HEADS * 256, PLAIN, 1.0))
    w_kv = l0_w_kvb.astype(BF16)
    q_lat, kv_lat = (
        _projection(a_lat, width // r, r, l0_q_norm, None, None, w_q, rope64, gq, tm, 1024, "mla_q"),
        _projection(a_lat, width // r + 1, r, l0_kv_norm, None, None, w_kv, None, gkv, tm, 1024,
                    "mla_kv"))
    q_ctx, kv_ctx = (
        _projection(a_ctx, width // r, r, l0_q_norm, None, None, w_q, None, gq, tm, 1024,
                    "mla_q_ctx"),
        _projection(a_ctx, width // r + 1, r, l0_kv_norm, None, None, w_kv, None, gkv, tm, 1024,
                    "mla_kv_ctx"))
    kr_blk = (width + 2 * r) // LANES
    o_lat = _mla_attention(q_lat, kv_lat, kv_ctx, a_lat, a_ctx, kr_blk, geo, tp["tq_flash"],
                           tp["tk"])
    o_ctx = _mla_ctx_attention(q_ctx, kv_ctx, a_ctx, kr_blk, geo)
    w_out = l0_w_out.astype(BF16)
    x_lat = _out_projection(o_lat, a_lat, w_out, x_lat, mod, lat, tm, tn_d, "mla_out")
    x_ctx = _out_projection(o_ctx, a_ctx, w_out, x_ctx, mod, cx, tm, tn_d, "mla_out")

    mod = _modulation(c_rows, l1_mod_w, l1_mod_b)
    per = SWA_HEADS // SWA_KV_HEADS
    order = [g * per + j for a in range(SWA_KV_HEADS // 2) for j in range(per)
             for g in (2 * a, 2 * a + 1)]
    col_perm = jnp.asarray([hh * SWA_HD + e for hh in order for e in range(SWA_HD)])
    width = SWA_HEADS * SWA_HD
    kvw = SWA_KV_HEADS * SWA_HD
    w_q, w_k, w_v, w_g = (l1_w_in[:, :width], l1_w_in[:, width:width + kvw],
                          l1_w_in[:, width + kvw:width + 2 * kvw], l1_w_in[:, width + 2 * kvw:])
    w_in = jnp.concatenate([w_g[:, col_perm], w_q[:, col_perm], w_k, w_v], axis=1).astype(BF16)
    g1 = groups((width, PLAIN, 1.0), (width, ROPE64, SWA_HD ** -0.5 * LOG2E),
                (kvw, ROPE64, 1.0), (kvw, PLAIN, 1.0))
    a_lat, a_ctx = both(lambda xs, grp, is_lat: _projection(
        xs, 0, d, l1_norm, mod, grp, w_in, rope64 if is_lat else None, g1, tm, 1152, "swa_in"))
    sink_perm = l1_sink[jnp.asarray(order)]
    tq = tp["tq_swa"]
    o_lat = _swa_attention(a_lat, a_lat, a_ctx, sink_perm, geo, tq, True)
    o_ctx = _swa_attention(a_ctx, a_lat, a_ctx, sink_perm, geo, tq, False)
    w_out = l1_w_out[col_perm, :].astype(BF16)
    x_lat = _out_projection(o_lat, a_lat, w_out, x_lat, mod, lat, tm, tn_d, "swa_out")
    x_ctx = _out_projection(o_ctx, a_ctx, w_out, x_ctx, mod, cx, tm, tn_d, "swa_out")

    mod = _modulation(c_rows, l2_mod_w, l2_mod_b)
    width = NA_HEADS * NA_HD
    w_in = jnp.concatenate([l2_w_in[:, 3 * width:], l2_w_in[:, :3 * width]], axis=1).astype(BF16)
    g2 = groups((width, PLAIN, 1.0), (width, PLAIN, NA_HD ** -0.5 * LOG2E), (2 * width, PLAIN, 1.0))
    a_lat, a_ctx = both(lambda xs, grp, is_lat: _projection(
        xs, 0, d, l2_norm, mod, grp, w_in, None, g2, tm, 1024, "na_in"))
    bias_tab = _na_bias_table(l2_rpb)
    o_lat = _na_attention(a_lat, a_lat, a_ctx, bias_tab, geo, tp["rq_na"], True)
    o_ctx = _na_attention(a_ctx, a_lat, a_ctx, bias_tab, geo, tp["rq_na"], False)
    w_out = l2_w_out.astype(BF16)
    x_lat = _out_projection(o_lat, a_lat, w_out, x_lat, mod, lat, tm, tn_d, "na_out")
    x_ctx = _out_projection(o_ctx, a_ctx, w_out, x_ctx, mod, cx, tm, tn_d, "na_out")

    mod = _modulation(c_rows, l3_mod_w, l3_mod_b)
    width = DIFF_HEADS * 2 * DIFF_HD
    w_in = jnp.concatenate([l3_w_in[:, 3 * width:], l3_w_in[:, :3 * width]], axis=1).astype(BF16)
    g3 = groups((width, PLAIN, 1.0), (width, ROPE128, DIFF_HD ** -0.5 * LOG2E),
                (width, ROPE128, 1.0), (width, PLAIN, 1.0))
    a_lat = _projection(x_lat, 0, d, l3_norm, mod, lat, w_in, rope128, g3, tm, 1024, "diff_in")
    kv_ctx = _projection(x_ctx, 0, d, l3_norm, mod, cx, w_in[:, 2 * width:], None, g3[2 * width // LANES:],
                         tm, 1024, "diff_in")
    lam_init = 0.8 - 0.6 * math.exp(-0.3 * 3)
    lam_rows = jnp.concatenate([jnp.stack([l3_lam_q1, l3_lam_k1, l3_lam_q2, l3_lam_k2]),
                                jnp.zeros((4, DIFF_HD), F32)], axis=0)
    o_lat = _diff_attention(a_lat, kv_ctx, lam_rows, l3_subln, geo, tp["tq_diff"], tp["tk"],
                            lam_init)
    x_lat = _out_projection(o_lat, a_lat, l3_w_out.astype(BF16), x_lat, mod, lat, tm, tn_d, "diff_out")

    return _final_norm(x_lat, final_norm, tm).reshape(b, n, d)
```

```python
import functools
import math

import jax
import jax.numpy as jnp
from jax import lax
from jax.experimental import pallas as pl
from jax.experimental.pallas import tpu as pltpu

F32 = jnp.float32
BF16 = jnp.bfloat16

LANES = 128
LOG2E = 1.4426950408889634
NEG_INF = -1e30
EPS = 1e-6
ROPE_BASE = 10000.0
GRID_W = 64
VMEM_LIMIT = 56 * 1024 * 1024

MLA_HEADS, MLA_NOPE, MLA_ROPE, MLA_V, MLA_RANK = 16, 128, 64, 128, 512
SWA_HEADS, SWA_KV_HEADS, SWA_HD, SWA_WINDOW = 32, 4, 64, 128
NA_HEADS, NA_HD, NA_WIN_R, NA_WIN_C = 32, 64, 8, 16
DIFF_HEADS, DIFF_HD = 8, 128

PLAIN, ROPE64, ROPE128 = 0, 1, 2
MLA_PAIRS_PER_ITER, DIFF_PAIRS_PER_ITER = 1, 2


def _cparams(sem):
    return pltpu.CompilerParams(dimension_semantics=sem, vmem_limit_bytes=VMEM_LIMIT)


def _dot_nt(a, b):
    return lax.dot_general(a, b, (((1,), (1,)), ((), ())), preferred_element_type=F32)


def _dot(a, b):
    return jnp.dot(a, b, preferred_element_type=F32)


def _rep(x, n):
    return x if n == 1 else jnp.concatenate([x] * n, axis=1)


def _largest_divisor(total, candidates):
    return next(c for c in candidates if total % c == 0)


def _mod_kernel(c_ref, w_ref, b_ref, o_ref):
    c = c_ref[...]
    s = c * jax.nn.sigmoid(c)
    o_ref[...] = _dot(s, w_ref[...]) + b_ref[...]


def _modulation(c_rows, mod_w, mod_b):
    r, d = c_rows.shape
    n = mod_w.shape[1]
    tn = math.gcd(n, 512)
    out = pl.pallas_call(
        _mod_kernel,
        out_shape=jax.ShapeDtypeStruct((r, n), F32),
        grid=(n // tn,),
        in_specs=[pl.BlockSpec((r, d), lambda j: (0, 0)),
                  pl.BlockSpec((d, tn), lambda j: (0, j)),
                  pl.BlockSpec((1, tn), lambda j: (0, j))],
        out_specs=pl.BlockSpec((r, tn), lambda j: (0, j)),
        compiler_params=_cparams(("arbitrary",)),
        name="modulation",
    )(c_rows, mod_w, mod_b.reshape(1, n))
    return out.reshape(r, 1, n)


def _mod_row(group):
    kind, val = group
    return (lambda i: i // val) if kind == "batch" else (lambda i: val)


def _rope_group(a, cos, sin, code):
    if code == ROPE64:
        lane = lax.broadcasted_iota(jnp.int32, a.shape, 1)
        first = (lane % 64) < 32
        partner = jnp.where(first, pltpu.roll(a, 96, 1), pltpu.roll(a, 32, 1))
    else:
        partner = pltpu.roll(a, 64, 1)
    return a * cos + partner * sin


def _proj_kernel(*refs, has_mod, has_rope, tile_patterns, d_mod):
    it = iter(refs)
    x_ref, g_ref = next(it), next(it)
    mod_ref = next(it) if has_mod else None
    w_ref = next(it)
    cos_ref, sin_ref = (next(it), next(it)) if has_rope else (None, None)
    o_ref, h_scr = next(it), next(it)
    j = pl.program_id(1)

    @pl.when(j == 0)
    def _():
        x = x_ref[...].astype(F32)
        y = x * lax.rsqrt(jnp.mean(x * x, axis=-1, keepdims=True) + EPS) * g_ref[...]
        if has_mod:
            shift = mod_ref[0, :, 0:d_mod]
            scale = mod_ref[0, :, d_mod:2 * d_mod]
            y = y * (1.0 + scale) + shift
        h_scr[...] = y.astype(BF16)

    def epilogue(pattern):
        acc = _dot(h_scr[...], w_ref[...])
        for gi, (code, scale) in enumerate(pattern):
            a = acc[:, gi * LANES:(gi + 1) * LANES]
            if code != PLAIN:
                a = _rope_group(a, cos_ref[...], sin_ref[...], code)
            if scale != 1.0:
                a = a * scale
            o_ref[:, gi * LANES:(gi + 1) * LANES] = a.astype(o_ref.dtype)

    distinct = sorted(set(tile_patterns), key=tile_patterns.index)
    if len(distinct) == 1:
        epilogue(distinct[0])
    else:
        for pat in distinct:
            tiles = [jj for jj, p in enumerate(tile_patterns) if p == pat]
            cond = functools.reduce(jnp.logical_or, [j == jj for jj in tiles])
            pl.when(cond)(functools.partial(epilogue, pat))


def _projection(x, col_blk, kd, gain, mod, group, w, rope, groups, tm, tn, name):
    m = x.shape[0]
    n_out = w.shape[1]
    tm = min(tm, m)
    if group is not None and group[0] == "const":
        name += "_ctx"
    assert m % tm == 0 and n_out % tn == 0 and tn % LANES == 0 and len(groups) == n_out // LANES
    if rope is None:
        groups = tuple((PLAIN, scale) for _, scale in groups)
    has_mod = mod is not None
    has_rope = any(code != PLAIN for code, _ in groups)
    gpt = tn // LANES
    tile_patterns = tuple(tuple(groups[t * gpt:(t + 1) * gpt]) for t in range(n_out // tn))

    in_specs = [pl.BlockSpec((tm, kd), lambda i, j: (i, col_blk)),
                pl.BlockSpec((1, kd), lambda i, j: (0, 0))]
    args = [x, gain.reshape(1, kd).astype(F32)]
    d_mod = 0
    if has_mod:
        d_mod = mod.shape[2] // 3
        row = _mod_row(group)
        in_specs.append(pl.BlockSpec((1, 1, 3 * d_mod), lambda i, j: (row(i), 0, 0)))
        args.append(mod)
    in_specs.append(pl.BlockSpec((kd, tn), lambda i, j: (0, j)))
    args.append(w)
    if has_rope:
        cos, sin, tiles_per_batch = rope
        in_specs += [pl.BlockSpec((tm, LANES), lambda i, j: (i % tiles_per_batch, 0))] * 2
        args += [cos, sin]
    kern = functools.partial(_proj_kernel, has_mod=has_mod, has_rope=has_rope,
                             tile_patterns=tile_patterns, d_mod=d_mod)
    return pl.pallas_call(
        kern,
        out_shape=jax.ShapeDtypeStruct((m, n_out), BF16),
        grid=(m // tm, n_out // tn),
        in_specs=in_specs,
        out_specs=pl.BlockSpec((tm, tn), lambda i, j: (i, j)),
        scratch_shapes=[pltpu.VMEM((tm, kd), BF16)],
        compiler_params=_cparams(("parallel", "arbitrary")),
        name=name,
    )(*args)


def _out_kernel(o_ref, gate_ref, w_ref, x_ref, gm_ref, y_ref, u_scr):
    @pl.when(pl.program_id(1) == 0)
    def _():
        g = gate_ref[...].astype(F32)
        u_scr[...] = (o_ref[...].astype(F32) * (g * jax.nn.sigmoid(g))).astype(BF16)

    y_ref[...] = x_ref[...] + gm_ref[0] * _dot(u_scr[...], w_ref[...])


def _out_projection(o, proj, w_out, x, mod, group, tm, tn, name):
    m, d = x.shape
    width = w_out.shape[0]
    tm = min(tm, m)
    if group[0] == "const":
        name += "_ctx"
    row = _mod_row(group)
    gate_blk = 2 * d // tn
    return pl.pallas_call(
        _out_kernel,
        out_shape=jax.ShapeDtypeStruct((m, d), F32),
        grid=(m // tm, d // tn),
        in_specs=[pl.BlockSpec((tm, width), lambda i, j: (i, 0)),
                  pl.BlockSpec((tm, width), lambda i, j: (i, 0)),
                  pl.BlockSpec((width, tn), lambda i, j: (0, j)),
                  pl.BlockSpec((tm, tn), lambda i, j: (i, j)),
                  pl.BlockSpec((1, 1, tn), lambda i, j: (row(i), 0, gate_blk + j))],
        out_specs=pl.BlockSpec((tm, tn), lambda i, j: (i, j)),
        scratch_shapes=[pltpu.VMEM((tm, width), BF16)],
        input_output_aliases={3: 0},
        compiler_params=_cparams(("parallel", "arbitrary")),
        name=name,
    )(o, proj, w_out, x, mod)


def _norm_kernel(x_ref, g_ref, o_ref):
    x = x_ref[...]
    o_ref[...] = x * lax.rsqrt(jnp.mean(x * x, axis=-1, keepdims=True) + EPS) * g_ref[...]


def _final_norm(x, gain, tm):
    m, d = x.shape
    return pl.pallas_call(
        _norm_kernel,
        out_shape=jax.ShapeDtypeStruct((m, d), F32),
        grid=(m // tm,),
        in_specs=[pl.BlockSpec((tm, d), lambda i: (i, 0)),
                  pl.BlockSpec((1, d), lambda i: (0, 0))],
        out_specs=pl.BlockSpec((tm, d), lambda i: (i, 0)),
        compiler_params=_cparams(("parallel",)),
        name="final_norm",
    )(x, gain.reshape(1, d))


def _flash(n_chunks, scores_t, values_t, stats, bufs, pairs_per_iter):
    for m_scr, l_scr, acc in stats:
        m_scr[...] = jnp.full(m_scr.shape, NEG_INF, F32)
        l_scr[...] = jnp.zeros(l_scr.shape, F32)
        acc[...] = jnp.zeros(acc.shape, F32)

    def load_scores(c, slot):
        for s, (st, _, _) in zip(scores_t(c), bufs):
            st[slot] = s

    def softmax(slot):
        for (m_scr, l_scr, _), (st, pt, alpha_buf) in zip(stats, bufs):
            s = st[slot]
            m_prev = m_scr[...]
            m_new = jnp.maximum(m_prev, jnp.max(s, axis=0, keepdims=True))
            alpha = jnp.exp2(m_prev - m_new)
            p = jnp.exp2(s - m_new)
            l_scr[...] = alpha * l_scr[...] + jnp.sum(p, axis=0, keepdims=True)
            m_scr[...] = m_new
            alpha_buf[slot] = alpha
            pt[slot] = p.astype(pt.dtype)

    def weighted_values(c, slot):
        vt = values_t(c)
        for (_, _, acc), (_, pt, alpha_buf) in zip(stats, bufs):
            acc[...] = alpha_buf[slot] * acc[...] + _dot(vt, pt[slot])

    def step(c, cur, first, last):
        if not last:
            load_scores(c + 1, 1 - cur)
        if not first:
            weighted_values(c - 1, 1 - cur)
        softmax(cur)

    load_scores(0, 0)
    step(0, 0, True, n_chunks == 1)
    span = 2 * pairs_per_iter
    iters = max(n_chunks - 2, 0) // span

    def body(i, carry):
        for k in range(span):
            step(1 + span * i + k, (1 + k) % 2, False, False)
        return carry
    lax.fori_loop(0, iters, body, 0)
    for c in range(1 + span * iters, n_chunks):
        step(c, c % 2, False, c == n_chunks - 1)
    weighted_values(n_chunks - 1, (n_chunks - 1) % 2)


def _transpose_chunks(dst, src, tk):
    def body(c, carry):
        rows = pl.ds(pl.multiple_of(c * tk, tk), tk)
        dst[c] = src[rows, :].astype(F32).T.astype(dst.dtype)
        return carry
    lax.fori_loop(0, dst.shape[0], body, 0)


def _softmax_once(s, v):
    m = jnp.max(s, axis=-1, keepdims=True)
    p = jnp.exp2(s - m)
    return _dot(p.astype(BF16), v) / jnp.sum(p, axis=-1, keepdims=True)


def _gather_rows(dst, src_ctx, src_lat, dst_cols, src_cols, ctx, step):
    dst[0:ctx, dst_cols] = src_ctx[:, src_cols]

    def copy(c, carry):
        r = pl.multiple_of(c * step, step)
        dst[pl.ds(ctx + r, step), dst_cols] = src_lat[pl.ds(r, step), src_cols]
        return carry
    lax.fori_loop(0, src_lat.shape[0] // step, copy, 0)


def _mla_kernel(q_ref, kvl_ref, kvc_ref, krl_ref, krc_ref, o_ref,
                k_scr, v_scr, vt_scr, qt_scr, m_scr, l_scr, acc_scr, st, pt, alpha, *, tk, ctx):
    lo, hi = slice(0, LANES), slice(LANES, 2 * LANES)

    @pl.when(pl.program_id(2) == 0)
    def _():
        _gather_rows(k_scr, kvc_ref, kvl_ref, lo, lo, ctx, ctx)
        _gather_rows(k_scr, krc_ref, krl_ref, hi, lo, ctx, ctx)
        _gather_rows(v_scr, kvc_ref, kvl_ref, lo, hi, ctx, ctx)
        _transpose_chunks(vt_scr, v_scr, tk)

    def chunk(c):
        return pl.ds(pl.multiple_of(c * tk, tk), tk)

    qt_scr[...] = q_ref[...].astype(F32).T.astype(BF16)
    _flash(vt_scr.shape[0],
           lambda c: (_dot(k_scr[chunk(c), :], qt_scr[...]),),
           lambda c: vt_scr[c],
           [(m_scr, l_scr, acc_scr)], [(st, pt, alpha)], MLA_PAIRS_PER_ITER)
    o_ref[...] = (acc_scr[...] / l_scr[...]).T.astype(o_ref.dtype)


def _flash_buffers(tq, tk):
    return [pltpu.VMEM((2, tk, tq), F32), pltpu.VMEM((2, tk, tq), BF16),
            pltpu.VMEM((2, 1, tq), F32)]


def _mla_attention(q_lat, kv_lat, kv_ctx, a_lat, a_ctx, kr_blk, geo, tq, tk):
    b, n, ctx = geo
    h = MLA_HEADS
    nt = n // tq
    return pl.pallas_call(
        functools.partial(_mla_kernel, tk=tk, ctx=ctx),
        out_shape=jax.ShapeDtypeStruct((b * n, h * MLA_V), BF16),
        grid=(b, h, nt),
        in_specs=[pl.BlockSpec((tq, 256), lambda bi, hi, t: (bi * nt + t, hi)),
                  pl.BlockSpec((n, 256), lambda bi, hi, t: (bi, hi)),
                  pl.BlockSpec((ctx, 256), lambda bi, hi, t: (bi, hi)),
                  pl.BlockSpec((n, LANES), lambda bi, hi, t: (bi, kr_blk)),
                  pl.BlockSpec((ctx, LANES), lambda bi, hi, t: (bi, kr_blk))],
        out_specs=pl.BlockSpec((tq, MLA_V), lambda bi, hi, t: (bi * nt + t, hi)),
        scratch_shapes=[pltpu.VMEM((ctx + n, 256), BF16), pltpu.VMEM((ctx + n, MLA_V), BF16),
                        pltpu.VMEM(((ctx + n) // tk, MLA_V, tk), BF16),
                        pltpu.VMEM((256, tq), BF16),
                        pltpu.VMEM((1, tq), F32), pltpu.VMEM((1, tq), F32),
                        pltpu.VMEM((MLA_V, tq), F32)] + _flash_buffers(tq, tk),
        compiler_params=_cparams(("parallel", "arbitrary", "arbitrary")),
        name="mla_attention",
    )(q_lat, kv_lat, kv_ctx, a_lat, a_ctx)


def _mla_ctx_kernel(q_ref, kv_ref, kr_ref, o_ref):
    q = q_ref[...]
    s = _dot_nt(q[:, 0:LANES], kv_ref[:, 0:LANES]) + _dot_nt(q[:, LANES:2 * LANES], kr_ref[...])
    o_ref[...] = _softmax_once(s, kv_ref[:, LANES:2 * LANES]).astype(o_ref.dtype)


def _mla_ctx_attention(q_ctx, kv_ctx, a_ctx, kr_blk, geo):
    b, n, ctx = geo
    h = MLA_HEADS
    return pl.pallas_call(
        _mla_ctx_kernel,
        out_shape=jax.ShapeDtypeStruct((b * ctx, h * MLA_V), BF16),
        grid=(b, h),
        in_specs=[pl.BlockSpec((ctx, 256), lambda bi, hi: (bi, hi)),
                  pl.BlockSpec((ctx, 256), lambda bi, hi: (bi, hi)),
                  pl.BlockSpec((ctx, LANES), lambda bi, hi: (bi, kr_blk))],
        out_specs=pl.BlockSpec((ctx, MLA_V), lambda bi, hi: (bi, hi)),
        compiler_params=_cparams(("parallel", "arbitrary")),
        name="mla_ctx_attention",
    )(q_ctx, kv_ctx, a_ctx)


def _diff_kernel(q_ref, kl_ref, vl_ref, kc_ref, vc_ref, lam_ref, sub_ref, o_ref,
                 k_scr, v_scr, vt_scr, qt_scr, m0, l0, a0, m1, l1, a1,
                 st0, pt0, alpha0, st1, pt1, alpha1, *, tk, ctx, lam_init):
    hd = DIFF_HD
    full = slice(0, 2 * hd)

    @pl.when(pl.program_id(2) == 0)
    def _():
        _gather_rows(k_scr, kc_ref, kl_ref, full, full, ctx, ctx)
        _gather_rows(v_scr, vc_ref, vl_ref, full, full, ctx, ctx)
        _transpose_chunks(vt_scr, v_scr, tk)

    def chunk(c):
        return pl.ds(pl.multiple_of(c * tk, tk), tk)

    qt_scr[...] = q_ref[...].astype(F32).T.astype(BF16)

    def scores_t(c):
        return tuple(_dot(k_scr[chunk(c), i * hd:(i + 1) * hd], qt_scr[i * hd:(i + 1) * hd, :])
                     for i in range(2))

    _flash(vt_scr.shape[0], scores_t, lambda c: vt_scr[c],
           [(m0, l0, a0), (m1, l1, a1)], [(st0, pt0, alpha0), (st1, pt1, alpha1)],
           DIFF_PAIRS_PER_ITER)

    lv = lam_ref[...]
    lam = (jnp.exp(jnp.sum(lv[0:1] * lv[1:2], axis=-1, keepdims=True))
           - jnp.exp(jnp.sum(lv[2:3] * lv[3:4], axis=-1, keepdims=True)) + lam_init)
    o = (a0[...] / l0[...] - lam * (a1[...] / l1[...])).T
    o = o * lax.rsqrt(jnp.mean(o * o, axis=-1, keepdims=True) + EPS) * sub_ref[...]
    o_ref[...] = (o * (1.0 - lam_init)).astype(o_ref.dtype)


def _diff_attention(a_lat, kv_ctx, lam_rows, subln, geo, tq, tk, lam_init):
    b, n, ctx = geo
    h = DIFF_HEADS
    nt = n // tq
    return pl.pallas_call(
        functools.partial(_diff_kernel, tk=tk, ctx=ctx, lam_init=lam_init),
        out_shape=jax.ShapeDtypeStruct((b * n, h * 256), BF16),
        grid=(b, h, nt),
        in_specs=[pl.BlockSpec((tq, 256), lambda bi, hi, t: (bi * nt + t, h + hi)),
                  pl.BlockSpec((n, 256), lambda bi, hi, t: (bi, 2 * h + hi)),
                  pl.BlockSpec((n, 256), lambda bi, hi, t: (bi, 3 * h + hi)),
                  pl.BlockSpec((ctx, 256), lambda bi, hi, t: (bi, hi)),
                  pl.BlockSpec((ctx, 256), lambda bi, hi, t: (bi, h + hi)),
                  pl.BlockSpec((8, LANES), lambda bi, hi, t: (0, 0)),
                  pl.BlockSpec((1, 256), lambda bi, hi, t: (0, 0))],
        out_specs=pl.BlockSpec((tq, 256), lambda bi, hi, t: (bi * nt + t, hi)),
        scratch_shapes=[pltpu.VMEM((ctx + n, 256), BF16), pltpu.VMEM((ctx + n, 256), BF16),
                        pltpu.VMEM(((ctx + n) // tk, 256, tk), BF16),
                        pltpu.VMEM((256, tq), BF16)]
        + [pltpu.VMEM((1, tq), F32), pltpu.VMEM((1, tq), F32), pltpu.VMEM((256, tq), F32)] * 2
        + _flash_buffers(tq, tk) * 2,
        compiler_params=_cparams(("parallel", "arbitrary", "arbitrary")),
        name="diff_attention",
    )(a_lat, a_lat, a_lat, kv_ctx, kv_ctx, lam_rows, subln.reshape(1, 256).astype(F32))


def _half_select(x, half, axis=1):
    idx = lax.broadcasted_iota(jnp.int32, x.shape, axis)
    keep = (idx < 64) if half == 0 else (idx >= 64)
    return jnp.where(keep, x, jnp.zeros_like(x))


def _swa_kernel(*refs, is_lat, tq, band):
    if is_lat:
        sink_ref, q_ref, kl_ref, vl_ref, kc_ref, vc_ref, o_ref = refs
    else:
        sink_ref, q_ref, kc_ref, vc_ref, o_ref = refs
    t = pl.program_id(1)
    per = SWA_HEADS // SWA_KV_HEADS
    if is_lat:
        n = kl_ref.shape[0]
        start = pl.multiple_of(jnp.clip(t * tq - SWA_WINDOW, 0, n - band), SWA_WINDOW)
        kpos = start + lax.broadcasted_iota(jnp.int32, (band, per * tq), 0)
        qpos = t * tq + lax.broadcasted_iota(jnp.int32, (band, per * tq), 1) % tq
        ok = jnp.abs(qpos - kpos) <= SWA_WINDOW
    for a in range(SWA_KV_HEADS // 2):
        cols = slice(a * LANES, (a + 1) * LANES)
        qs = jnp.concatenate([q_ref[:, (a * per + j) * LANES:(a * per + j + 1) * LANES]
                              for j in range(per)], axis=0)
        kc = kc_ref[:, cols]
        vct = vc_ref[:, cols].astype(F32).T.astype(BF16)
        if is_lat:
            kb = kl_ref[pl.ds(start, band), cols]
            vbt = vl_ref[pl.ds(start, band), cols].astype(F32).T.astype(BF16)
        out_t = None
        for half in range(2):
            sink = jnp.concatenate(
                [jnp.full((1, tq), sink_ref[(a * per + j) * 2 + half] * LOG2E, F32)
                 for j in range(per)], axis=1)
            st_c = _dot_nt(_half_select(kc, half, 1), qs)
            m = jnp.maximum(jnp.max(st_c, axis=0, keepdims=True), sink)
            if is_lat:
                st_b = jnp.where(ok, _dot_nt(_half_select(kb, half, 1), qs), NEG_INF)
                m = jnp.maximum(m, jnp.max(st_b, axis=0, keepdims=True))
            pt_c = jnp.exp2(st_c - m)
            l = jnp.sum(pt_c, axis=0, keepdims=True) + jnp.exp2(sink - m)
            o_t = _dot(_half_select(vct, half, 0), pt_c.astype(BF16))
            if is_lat:
                pt_b = jnp.exp2(st_b - m)
                l = l + jnp.sum(pt_b, axis=0, keepdims=True)
                o_t = o_t + _dot(_half_select(vbt, half, 0), pt_b.astype(BF16))
            o_t = o_t / l
            out_t = o_t if out_t is None else out_t + o_t
        out = out_t.T
        for j in range(per):
            o_ref[:, (a * per + j) * LANES:(a * per + j + 1) * LANES] = (
                out[j * tq:(j + 1) * tq].astype(o_ref.dtype))


def _swa_attention(a_q, a_lat, a_ctx, sink_perm, geo, tq, is_lat):
    b, n, ctx = geo
    width = SWA_HEADS * SWA_HD
    kvw = SWA_KV_HEADS * SWA_HD
    rows = n if is_lat else ctx
    nt = rows // tq
    kblk, vblk = 2 * width // kvw, 2 * width // kvw + 1
    in_specs = [pl.BlockSpec(memory_space=pltpu.SMEM),
                pl.BlockSpec((tq, width), lambda bi, t: (bi * nt + t, 1))]
    args = [sink_perm, a_q]
    if is_lat:
        in_specs += [pl.BlockSpec((n, kvw), lambda bi, t: (bi, kblk)),
                     pl.BlockSpec((n, kvw), lambda bi, t: (bi, vblk))]
        args += [a_lat, a_lat]
    in_specs += [pl.BlockSpec((ctx, kvw), lambda bi, t: (bi, kblk)),
                 pl.BlockSpec((ctx, kvw), lambda bi, t: (bi, vblk))]
    args += [a_ctx, a_ctx]
    return pl.pallas_call(
        functools.partial(_swa_kernel, is_lat=is_lat, tq=tq, band=tq + 2 * SWA_WINDOW),
        out_shape=jax.ShapeDtypeStruct((b * rows, width), BF16),
        grid=(b, nt),
        in_specs=in_specs,
        out_specs=pl.BlockSpec((tq, width), lambda bi, t: (bi * nt + t, 0)),
        compiler_params=_cparams(("parallel", "arbitrary")),
        name="swa_attention" if is_lat else "swa_ctx_attention",
    )(*args)


def _na_kernel(*refs, is_lat, rq, rk, rows):
    if is_lat:
        q_ref, kl_ref, vl_ref, kc_ref, vc_ref, bt_ref, o_ref, s_scr = refs
    else:
        q_ref, kc_ref, vc_ref, o_ref = refs
    t = pl.program_id(2)
    w = GRID_W
    band = rk * w
    q = q_ref[...]
    kc = kc_ref[...]
    vct = vc_ref[...].astype(F32).T.astype(BF16)
    if is_lat:
        r0 = t * rq
        srow = jnp.clip(r0 - NA_WIN_R // 2, 0, rows - rk)
        start = pl.multiple_of(srow * w, w)
        kb = kl_ref[pl.ds(start, band), :]
        vbt = vl_ref[pl.ds(start, band), :].astype(F32).T.astype(BF16)
        lo = lax.broadcasted_iota(jnp.int32, (w, LANES), 1) < w
    out_t = None
    for half in range(2):
        st_c = _dot_nt(_half_select(kc, half, 1), q)
        m = jnp.max(st_c, axis=0, keepdims=True)
        if is_lat:
            s_half = s_scr.at[half]
            s_half[...] = _dot_nt(_half_select(kb, half, 1), q)
            for kj in range(rk):
                kr = srow + kj
                for qp in range(rq // 2):
                    ra = r0 + 2 * qp
                    rsa = jnp.clip(ra - NA_WIN_R // 2, 0, rows - NA_WIN_R)
                    rsb = jnp.clip(ra + 1 - NA_WIN_R // 2, 0, rows - NA_WIN_R)
                    e = jnp.clip(kr - ra + NA_WIN_R - 1, 0, 2 * NA_WIN_R - 1)
                    pen0 = jnp.where((kr >= rsa) & (kr < rsa + NA_WIN_R), 0.0, NEG_INF)
                    pen1 = jnp.where((kr >= rsb) & (kr < rsb + NA_WIN_R), 0.0, NEG_INF)
                    blk = (slice(kj * w, (kj + 1) * w), slice(qp * LANES, (qp + 1) * LANES))
                    s_half[blk] = s_half[blk] + (bt_ref[half, e] * LOG2E + jnp.where(lo, pen0, pen1))
            st_b = s_half[...]
            m = jnp.maximum(m, jnp.max(st_b, axis=0, keepdims=True))
        pt_c = jnp.exp2(st_c - m)
        l = jnp.sum(pt_c, axis=0, keepdims=True)
        o_t = _dot(_half_select(vct, half, 0), pt_c.astype(BF16))
        if is_lat:
            pt_b = jnp.exp2(st_b - m)
            l = l + jnp.sum(pt_b, axis=0, keepdims=True)
            o_t = o_t + _dot(_half_select(vbt, half, 0), pt_b.astype(BF16))
        o_t = o_t / l
        out_t = o_t if out_t is None else out_t + o_t
    o_ref[...] = out_t.T.astype(o_ref.dtype)


def _na_attention(a_q, a_lat, a_ctx, bias_tab, geo, rq, is_lat):
    b, n, ctx = geo
    width = NA_HEADS * NA_HD
    pairs = width // LANES
    tq = rq * GRID_W
    rk = rq + NA_WIN_R
    grid_rows = n // GRID_W
    rows = n if is_lat else ctx
    assert grid_rows >= rk and rows % tq == 0
    nt = rows // tq
    qo, ko, vo = pairs, 2 * pairs, 3 * pairs
    in_specs = [pl.BlockSpec((tq, LANES), lambda bi, pi, t: (bi * nt + t, qo + pi))]
    args = [a_q]
    if is_lat:
        in_specs += [pl.BlockSpec((n, LANES), lambda bi, pi, t: (bi, ko + pi)),
                     pl.BlockSpec((n, LANES), lambda bi, pi, t: (bi, vo + pi))]
        args += [a_lat, a_lat]
    in_specs += [pl.BlockSpec((ctx, LANES), lambda bi, pi, t: (bi, ko + pi)),
                 pl.BlockSpec((ctx, LANES), lambda bi, pi, t: (bi, vo + pi))]
    args += [a_ctx, a_ctx]
    scratch = []
    if is_lat:
        in_specs.append(pl.BlockSpec((2, 2 * NA_WIN_R, GRID_W, LANES),
                                     lambda bi, pi, t: (pi, 0, 0, 0)))
        args.append(bias_tab)
        scratch = [pltpu.VMEM((2, rk * GRID_W, tq), F32)]
    return pl.pallas_call(
        functools.partial(_na_kernel, is_lat=is_lat, rq=rq, rk=rk, rows=grid_rows),
        out_shape=jax.ShapeDtypeStruct((b * rows, width), BF16),
        grid=(b, pairs, nt),
        in_specs=in_specs,
        out_specs=pl.BlockSpec((tq, LANES), lambda bi, pi, t: (bi * nt + t, pi)),
        scratch_shapes=scratch,
        compiler_params=_cparams(("parallel", "arbitrary", "arbitrary")),
        name="na_attention" if is_lat else "na_ctx_attention",
    )(*args)


def _rope_tables(n, d_rot):
    t = jnp.arange(n)
    row = (t // GRID_W).astype(F32)
    col = (t % GRID_W).astype(F32)
    d_axis = d_rot // 2
    inv = ROPE_BASE ** (-jnp.arange(0, d_axis, 2, dtype=F32) / d_axis)
    ang = jnp.concatenate([row[:, None] * inv, col[:, None] * inv], axis=-1)
    cos, sin = jnp.cos(ang), jnp.sin(ang)
    reps = LANES // d_rot
    return (jnp.tile(cos, (1, 2 * reps)),
            jnp.tile(jnp.concatenate([-sin, sin], axis=-1), (1, reps)))


def _na_bias_table(rpb):
    col = jnp.arange(GRID_W)
    dc = jnp.clip(col[:, None] - col[None, :], -(NA_WIN_C - 1), NA_WIN_C - 1) + (NA_WIN_C - 1)
    cs = jnp.clip(col - NA_WIN_C // 2, 0, GRID_W - NA_WIN_C)[None, :]
    col_ok = (col[:, None] >= cs) & (col[:, None] < cs + NA_WIN_C)
    bt = jnp.where(col_ok, rpb[:, :, dc], NEG_INF)
    bt = jnp.pad(bt, ((0, 0), (1, 1), (0, 0), (0, 0)))
    return jnp.concatenate([bt[:, 1:], bt[:, :-1]], axis=-1).astype(F32)


def _tile_plan(b, n, n_ctx, d):
    tm = _largest_divisor(math.gcd(n, b * n_ctx), (512, 256, 128))
    return dict(
        tm=tm,
        tn_d=_largest_divisor(d, (2048, 1024, 512, 256, 128)),
        tq_flash=_largest_divisor(n, (512, 256, 128)),
        tq_diff=_largest_divisor(n, (512, 256, 128)),
        tk=_largest_divisor(n + n_ctx, (768, 640, 512, 384, 256, 128)),
        tq_swa=128,
        rq_na=4,
    )


def kernel(x, c, ctx, c_ctx, l0_mod_w, l0_mod_b, l0_norm, l0_w_in, l0_q_norm, l0_w_qb, l0_kv_norm, l0_w_kvb, l0_w_out, l1_mod_w, l1_mod_b, l1_norm, l1_w_in, l1_sink, l1_w_out, l2_mod_w, l2_mod_b, l2_norm, l2_w_in, l2_rpb, l2_w_out, l3_mod_w, l3_mod_b, l3_norm, l3_w_in, l3_lam_q1, l3_lam_k1, l3_lam_q2, l3_lam_k2, l3_subln, l3_w_out, final_norm):
    b, n, d = x.shape
    n_ctx = ctx.shape[1]
    geo = (b, n, n_ctx)
    assert b < 8 and n % GRID_W == 0
    tp = _tile_plan(b, n, n_ctx, d)
    tm, tn_d = tp["tm"], tp["tn_d"]
    lat, cx = ("batch", n // tm), ("const", b)

    x_lat = x.reshape(b * n, d)
    x_ctx = ctx.reshape(b * n_ctx, d)
    c_rows = jnp.concatenate([c, c_ctx[None, :], jnp.zeros((8 - b - 1, d), F32)], axis=0)
    rope64 = _rope_tables(n, 64) + (n // tm,)
    rope128 = _rope_tables(n, 128) + (n // tm,)

    def groups(*spans):
        out = []
        for cols, code, scale in spans:
            out += [(code, scale)] * (cols // LANES)
        return tuple(out)

    def both(fn):
        return fn(x_lat, lat, True), fn(x_ctx, cx, False)

    mod = _modulation(c_rows, l0_mod_w, l0_mod_b)
    r = MLA_RANK
    w_in = jnp.concatenate([l0_w_in[:, 2 * r + MLA_ROPE:], l0_w_in[:, :2 * r + MLA_ROPE],
                            jnp.zeros((d, LANES - MLA_ROPE), F32)], axis=1).astype(BF16)
    width = MLA_HEADS * MLA_V
    g0 = groups((width + 2 * r, PLAIN, 1.0), (LANES, ROPE64, 1.0))
    a_lat, a_ctx = both(lambda xs, grp, is_lat: _projection(
        xs, 0, d, l0_norm, mod, grp, w_in, rope64 if is_lat else None, g0, tm, 640, "mla_in"))
    qk = MLA_NOPE + MLA_ROPE
    w_q = l0_w_qb.reshape(r, MLA_HEADS, qk)
    w_q = jnp.pad(w_q, ((0, 0), (0, 0), (0, 256 - qk))).reshape(r, MLA_HEADS * 256).astype(BF16)
    q_scale = qk ** -0.5 * LOG2E
    gq = groups((LANES, PLAIN, q_scale), (LANES, ROPE64, q_scale)) * MLA_HEADS
    gkv = groups((MLA_HEADS * 256, PLAIN, 1.0))
    w_kv = l0_w_kvb.astype(BF16)
    q_lat, kv_lat = (
        _projection(a_lat, width // r, r, l0_q_norm, None, None, w_q, rope64, gq, tm, 1024, "mla_q"),
        _projection(a_lat, width // r + 1, r, l0_kv_norm, None, None, w_kv, None, gkv, tm, 1024,
                    "mla_kv"))
    q_ctx, kv_ctx = (
        _projection(a_ctx, width // r, r, l0_q_norm, None, None, w_q, None, gq, tm, 1024,
                    "mla_q_ctx"),
        _projection(a_ctx, width // r + 1, r, l0_kv_norm, None, None, w_kv, None, gkv, tm, 1024,
                    "mla_kv_ctx"))
    kr_blk = (width + 2 * r) // LANES
    o_lat = _mla_attention(q_lat, kv_lat, kv_ctx, a_lat, a_ctx, kr_blk, geo, tp["tq_flash"],
                           tp["tk"])
    o_ctx = _mla_ctx_attention(q_ctx, kv_ctx, a_ctx, kr_blk, geo)
    w_out = l0_w_out.astype(BF16)
    x_lat = _out_projection(o_lat, a_lat, w_out, x_lat, mod, lat, tm, tn_d, "mla_out")
    x_ctx = _out_projection(o_ctx, a_ctx, w_out, x_ctx, mod, cx, tm, tn_d, "mla_out")

    mod = _modulation(c_rows, l1_mod_w, l1_mod_b)
    per = SWA_HEADS // SWA_KV_HEADS
    order = [g * per + j for a in range(SWA_KV_HEADS // 2) for j in range(per)
             for g in (2 * a, 2 * a + 1)]
    col_perm = jnp.asarray([hh * SWA_HD + e for hh in order for e in range(SWA_HD)])
    width = SWA_HEADS * SWA_HD
    kvw = SWA_KV_HEADS * SWA_HD
    w_q, w_k, w_v, w_g = (l1_w_in[:, :width], l1_w_in[:, width:width + kvw],
                          l1_w_in[:, width + kvw:width + 2 * kvw], l1_w_in[:, width + 2 * kvw:])
    w_in = jnp.concatenate([w_g[:, col_perm], w_q[:, col_perm], w_k, w_v], axis=1).astype(BF16)
    g1 = groups((width, PLAIN, 1.0), (width, ROPE64, SWA_HD ** -0.5 * LOG2E),
                (kvw, ROPE64, 1.0), (kvw, PLAIN, 1.0))
    a_lat, a_ctx = both(lambda xs, grp, is_lat: _projection(
        xs, 0, d, l1_norm, mod, grp, w_in, rope64 if is_lat else None, g1, tm, 1152, "swa_in"))
    sink_perm = l1_sink[jnp.asarray(order)]
    tq = tp["tq_swa"]
    o_lat = _swa_attention(a_lat, a_lat, a_ctx, sink_perm, geo, tq, True)
    o_ctx = _swa_attention(a_ctx, a_lat, a_ctx, sink_perm, geo, tq, False)
    w_out = l1_w_out[col_perm, :].astype(BF16)
    x_lat = _out_projection(o_lat, a_lat, w_out, x_lat, mod, lat, tm, tn_d, "swa_out")
    x_ctx = _out_projection(o_ctx, a_ctx, w_out, x_ctx, mod, cx, tm, tn_d, "swa_out")

    mod = _modulation(c_rows, l2_mod_w, l2_mod_b)
    width = NA_HEADS * NA_HD
    w_in = jnp.concatenate([l2_w_in[:, 3 * width:], l2_w_in[:, :3 * width]], axis=1).astype(BF16)
    g2 = groups((width, PLAIN, 1.0), (width, PLAIN, NA_HD ** -0.5 * LOG2E), (2 * width, PLAIN, 1.0))
    a_lat, a_ctx = both(lambda xs, grp, is_lat: _projection(
        xs, 0, d, l2_norm, mod, grp, w_in, None, g2, tm, 1024, "na_in"))
    bias_tab = _na_bias_table(l2_rpb)
    o_lat = _na_attention(a_lat, a_lat, a_ctx, bias_tab, geo, 2 * tp["rq_na"], True)
    o_ctx = _na_attention(a_ctx, a_lat, a_ctx, bias_tab, geo, tp["rq_na"], False)
    w_out = l2_w_out.astype(BF16)
    x_lat = _out_projection(o_lat, a_lat, w_out, x_lat, mod, lat, tm, tn_d, "na_out")
    x_ctx = _out_projection(o_ctx, a_ctx, w_out, x_ctx, mod, cx, tm, tn_d, "na_out")

    mod = _modulation(c_rows, l3_mod_w, l3_mod_b)
    width = DIFF_HEADS * 2 * DIFF_HD
    w_in = jnp.concatenate([l3_w_in[:, 3 * width:], l3_w_in[:, :3 * width]], axis=1).astype(BF16)
    g3 = groups((width, PLAIN, 1.0), (width, ROPE128, DIFF_HD ** -0.5 * LOG2E),
                (width, ROPE128, 1.0), (width, PLAIN, 1.0))
    a_lat = _projection(x_lat, 0, d, l3_norm, mod, lat, w_in, rope128, g3, tm, 1024, "diff_in")
    kv_ctx = _projection(x_ctx, 0, d, l3_norm, mod, cx, w_in[:, 2 * width:], None, g3[2 * width // LANES:],
                         tm, 1024, "diff_in")
    lam_init = 0.8 - 0.6 * math.exp(-0.3 * 3)
    lam_rows = jnp.concatenate([jnp.stack([l3_lam_q1, l3_lam_k1, l3_lam_q2, l3_lam_k2]),
                                jnp.zeros((4, DIFF_HD), F32)], axis=0)
    o_lat = _diff_attention(a_lat, kv_ctx, lam_rows, l3_subln, geo, tp["tq_diff"], tp["tk"],
                            lam_init)
    x_lat = _out_projection(o_lat, a_lat, l3_w_out.astype(BF16), x_lat, mod, lat, tm, tn_d, "diff_out")

    return _final_norm(x_lat, final_norm, tm).reshape(b, n, d)
```

```python
import functools
import math

import jax
import jax.numpy as jnp
from jax import lax
from jax.experimental import pallas as pl
from jax.experimental.pallas import tpu as pltpu

F32 = jnp.float32
BF16 = jnp.bfloat16

LANES = 128
LOG2E = 1.4426950408889634
NEG_INF = -1e30
EPS = 1e-6
ROPE_BASE = 10000.0
GRID_W = 64
VMEM_LIMIT = 56 * 1024 * 1024

MLA_HEADS, MLA_NOPE, MLA_ROPE, MLA_V, MLA_RANK = 16, 128, 64, 128, 512
SWA_HEADS, SWA_KV_HEADS, SWA_HD, SWA_WINDOW = 32, 4, 64, 128
NA_HEADS, NA_HD, NA_WIN_R, NA_WIN_C = 32, 64, 8, 16
DIFF_HEADS, DIFF_HD = 8, 128

PLAIN, ROPE64, ROPE128 = 0, 1, 2
MLA_PAIRS_PER_ITER, DIFF_PAIRS_PER_ITER = 1, 2


def _cparams(sem):
    return pltpu.CompilerParams(dimension_semantics=sem, vmem_limit_bytes=VMEM_LIMIT)


def _dot_nt(a, b):
    return lax.dot_general(a, b, (((1,), (1,)), ((), ())), preferred_element_type=F32)


def _dot(a, b):
    return jnp.dot(a, b, preferred_element_type=F32)


def _rep(x, n):
    return x if n == 1 else jnp.concatenate([x] * n, axis=1)


def _largest_divisor(total, candidates):
    return next(c for c in candidates if total % c == 0)


def _mod_kernel(c_ref, w_ref, b_ref, o_ref):
    c = c_ref[...]
    s = c * jax.nn.sigmoid(c)
    o_ref[...] = _dot(s, w_ref[...]) + b_ref[...]


def _modulation(c_rows, mod_w, mod_b):
    r, d = c_rows.shape
    n = mod_w.shape[1]
    tn = math.gcd(n, 512)
    out = pl.pallas_call(
        _mod_kernel,
        out_shape=jax.ShapeDtypeStruct((r, n), F32),
        grid=(n // tn,),
        in_specs=[pl.BlockSpec((r, d), lambda j: (0, 0)),
                  pl.BlockSpec((d, tn), lambda j: (0, j)),
                  pl.BlockSpec((1, tn), lambda j: (0, j))],
        out_specs=pl.BlockSpec((r, tn), lambda j: (0, j)),
        compiler_params=_cparams(("arbitrary",)),
        name="modulation",
    )(c_rows, mod_w, mod_b.reshape(1, n))
    return out.reshape(r, 1, n)


def _mod_row(group):
    kind, val = group
    return (lambda i: i // val) if kind == "batch" else (lambda i: val)


def _rope_group(a, cos, sin, code):
    if code == ROPE64:
        lane = lax.broadcasted_iota(jnp.int32, a.shape, 1)
        first = (lane % 64) < 32
        partner = jnp.where(first, pltpu.roll(a, 96, 1), pltpu.roll(a, 32, 1))
    else:
        partner = pltpu.roll(a, 64, 1)
    return a * cos + partner * sin


def _proj_kernel(*refs, has_mod, has_rope, tile_patterns, d_mod):
    it = iter(refs)
    x_ref, g_ref = next(it), next(it)
    mod_ref = next(it) if has_mod else None
    w_ref = next(it)
    cos_ref, sin_ref = (next(it), next(it)) if has_rope else (None, None)
    o_ref, h_scr = next(it), next(it)
    j = pl.program_id(1)

    @pl.when(j == 0)
    def _():
        x = x_ref[...].astype(F32)
        y = x * lax.rsqrt(jnp.mean(x * x, axis=-1, keepdims=True) + EPS) * g_ref[...]
        if has_mod:
            shift = mod_ref[0, :, 0:d_mod]
            scale = mod_ref[0, :, d_mod:2 * d_mod]
            y = y * (1.0 + scale) + shift
        h_scr[...] = y.astype(BF16)

    def epilogue(pattern):
        acc = _dot(h_scr[...], w_ref[...])
        for gi, (code, scale) in enumerate(pattern):
            a = acc[:, gi * LANES:(gi + 1) * LANES]
            if code != PLAIN:
                a = _rope_group(a, cos_ref[...], sin_ref[...], code)
            if scale != 1.0:
                a = a * scale
            o_ref[:, gi * LANES:(gi + 1) * LANES] = a.astype(o_ref.dtype)

    distinct = sorted(set(tile_patterns), key=tile_patterns.index)
    if len(distinct) == 1:
        epilogue(distinct[0])
    else:
        for pat in distinct:
            tiles = [jj for jj, p in enumerate(tile_patterns) if p == pat]
            cond = functools.reduce(jnp.logical_or, [j == jj for jj in tiles])
            pl.when(cond)(functools.partial(epilogue, pat))


def _projection(x, col_blk, kd, gain, mod, group, w, rope, groups, tm, tn, name):
    m = x.shape[0]
    n_out = w.shape[1]
    tm = min(tm, m)
    if group is not None and group[0] == "const":
        name += "_ctx"
    assert m % tm == 0 and n_out % tn == 0 and tn % LANES == 0 and len(groups) == n_out // LANES
    if rope is None:
        groups = tuple((PLAIN, scale) for _, scale in groups)
    has_mod = mod is not None
    has_rope = any(code != PLAIN for code, _ in groups)
    gpt = tn // LANES
    tile_patterns = tuple(tuple(groups[t * gpt:(t + 1) * gpt]) for t in range(n_out // tn))

    in_specs = [pl.BlockSpec((tm, kd), lambda i, j: (i, col_blk)),
                pl.BlockSpec((1, kd), lambda i, j: (0, 0))]
    args = [x, gain.reshape(1, kd).astype(F32)]
    d_mod = 0
    if has_mod:
        d_mod = mod.shape[2] // 3
        row = _mod_row(group)
        in_specs.append(pl.BlockSpec((1, 1, 3 * d_mod), lambda i, j: (row(i), 0, 0)))
        args.append(mod)
    in_specs.append(pl.BlockSpec((kd, tn), lambda i, j: (0, j)))
    args.append(w)
    if has_rope:
        cos, sin, tiles_per_batch = rope
        in_specs += [pl.BlockSpec((tm, LANES), lambda i, j: (i % tiles_per_batch, 0))] * 2
        args += [cos, sin]
    kern = functools.partial(_proj_kernel, has_mod=has_mod, has_rope=has_rope,
                             tile_patterns=tile_patterns, d_mod=d_mod)
    return pl.pallas_call(
        kern,
        out_shape=jax.ShapeDtypeStruct((m, n_out), BF16),
        grid=(m // tm, n_out // tn),
        in_specs=in_specs,
        out_specs=pl.BlockSpec((tm, tn), lambda i, j: (i, j)),
        scratch_shapes=[pltpu.VMEM((tm, kd), BF16)],
        compiler_params=_cparams(("parallel", "arbitrary")),
        name=name,
    )(*args)


def _out_kernel(o_ref, gate_ref, w_ref, x_ref, gm_ref, y_ref, u_scr):
    @pl.when(pl.program_id(1) == 0)
    def _():
        g = gate_ref[...].astype(F32)
        u_scr[...] = (o_ref[...].astype(F32) * (g * jax.nn.sigmoid(g))).astype(BF16)

    y_ref[...] = x_ref[...] + gm_ref[0] * _dot(u_scr[...], w_ref[...])


def _out_projection(o, proj, w_out, x, mod, group, tm, tn, name):
    m, d = x.shape
    width = w_out.shape[0]
    tm = min(tm, m)
    if group[0] == "const":
        name += "_ctx"
    row = _mod_row(group)
    gate_blk = 2 * d // tn
    return pl.pallas_call(
        _out_kernel,
        out_shape=jax.ShapeDtypeStruct((m, d), F32),
        grid=(m // tm, d // tn),
        in_specs=[pl.BlockSpec((tm, width), lambda i, j: (i, 0)),
                  pl.BlockSpec((tm, width), lambda i, j: (i, 0)),
                  pl.BlockSpec((width, tn), lambda i, j: (0, j)),
                  pl.BlockSpec((tm, tn), lambda i, j: (i, j)),
                  pl.BlockSpec((1, 1, tn), lambda i, j: (row(i), 0, gate_blk + j))],
        out_specs=pl.BlockSpec((tm, tn), lambda i, j: (i, j)),
        scratch_shapes=[pltpu.VMEM((tm, width), BF16)],
        input_output_aliases={3: 0},
        compiler_params=_cparams(("parallel", "arbitrary")),
        name=name,
    )(o, proj, w_out, x, mod)


def _norm_kernel(x_ref, g_ref, o_ref):
    x = x_ref[...]
    o_ref[...] = x * lax.rsqrt(jnp.mean(x * x, axis=-1, keepdims=True) + EPS) * g_ref[...]


def _final_norm(x, gain, tm):
    m, d = x.shape
    return pl.pallas_call(
        _norm_kernel,
        out_shape=jax.ShapeDtypeStruct((m, d), F32),
        grid=(m // tm,),
        in_specs=[pl.BlockSpec((tm, d), lambda i: (i, 0)),
                  pl.BlockSpec((1, d), lambda i: (0, 0))],
        out_specs=pl.BlockSpec((tm, d), lambda i: (i, 0)),
        compiler_params=_cparams(("parallel",)),
        name="final_norm",
    )(x, gain.reshape(1, d))


def _flash(n_chunks, scores_t, values_t, stats, bufs, pairs_per_iter):
    for m_scr, l_scr, acc in stats:
        m_scr[...] = jnp.full(m_scr.shape, NEG_INF, F32)
        l_scr[...] = jnp.zeros(l_scr.shape, F32)
        acc[...] = jnp.zeros(acc.shape, F32)

    def load_scores(c, slot):
        for s, (st, _, _, _) in zip(scores_t(c), bufs):
            st[slot] = s

    def softmax(slot):
        for (m_scr, l_scr, _), (st, pt, alpha_buf, _) in zip(stats, bufs):
            s = st[slot]
            m_prev = m_scr[...]
            m_new = jnp.maximum(m_prev, jnp.max(s, axis=0, keepdims=True))
            alpha = jnp.exp2(m_prev - m_new)
            p = jnp.exp2(s - m_new)
            l_scr[...] = alpha * l_scr[...] + jnp.sum(p, axis=0, keepdims=True)
            m_scr[...] = m_new
            alpha_buf[slot] = alpha
            pt[slot] = p.astype(pt.dtype)

    def weighted_values(c, slot):
        vts = values_t(c)
        if not isinstance(vts, tuple):
            vts = (vts,) * len(stats)
        for vt, (_, _, acc), (_, pt, alpha_buf, _) in zip(vts, stats, bufs):
            acc[...] = alpha_buf[slot] * acc[...] + _dot(vt, pt[slot])

    def step(c, cur, first, last):
        if not last:
            load_scores(c + 1, 1 - cur)
        if not first:
            weighted_values(c - 1, 1 - cur)
        softmax(cur)

    load_scores(0, 0)
    step(0, 0, True, n_chunks == 1)
    span = 2 * pairs_per_iter
    iters = max(n_chunks - 2, 0) // span

    def body(i, carry):
        for k in range(span):
            step(1 + span * i + k, (1 + k) % 2, False, False)
        return carry
    lax.fori_loop(0, iters, body, 0)
    for c in range(1 + span * iters, n_chunks):
        step(c, c % 2, False, c == n_chunks - 1)
    weighted_values(n_chunks - 1, (n_chunks - 1) % 2)


def _transpose_chunks(dst, src, tk):
    def body(c, carry):
        rows = pl.ds(pl.multiple_of(c * tk, tk), tk)
        dst[c] = src[rows, :].astype(F32).T.astype(dst.dtype)
        return carry
    lax.fori_loop(0, dst.shape[0], body, 0)


def _softmax_once(s, v):
    m = jnp.max(s, axis=-1, keepdims=True)
    p = jnp.exp2(s - m)
    return _dot(p.astype(BF16), v) / jnp.sum(p, axis=-1, keepdims=True)


def _gather_rows(dst, src_ctx, src_lat, dst_cols, src_cols, ctx, step):
    dst[0:ctx, dst_cols] = src_ctx[:, src_cols]

    def copy(c, carry):
        r = pl.multiple_of(c * step, step)
        dst[pl.ds(ctx + r, step), dst_cols] = src_lat[pl.ds(r, step), src_cols]
        return carry
    lax.fori_loop(0, src_lat.shape[0] // step, copy, 0)


def _mla_kernel(q_ref, kvl_ref, kvc_ref, krl_ref, krc_ref, o_ref,
                k_scr, v_scr, vt_scr, qt_scr, *stat_and_bufs, tk, ctx, heads):
    lo, hi = slice(0, LANES), slice(LANES, 2 * LANES)
    stats = [stat_and_bufs[3 * g:3 * g + 3] for g in range(heads)]
    bufs = [stat_and_bufs[3 * heads + 4 * g:3 * heads + 4 * g + 4] for g in range(heads)]

    @pl.when(pl.program_id(2) == 0)
    def _():
        for g in range(heads):
            kn = slice(2 * g * LANES, (2 * g + 1) * LANES)
            vv = slice((2 * g + 1) * LANES, (2 * g + 2) * LANES)
            _gather_rows(k_scr.at[g], kvc_ref, kvl_ref, lo, kn, ctx, ctx)
            _gather_rows(k_scr.at[g], krc_ref, krl_ref, hi, lo, ctx, ctx)
            _gather_rows(v_scr, kvc_ref, kvl_ref, lo, vv, ctx, ctx)
            _transpose_chunks(vt_scr.at[g], v_scr, tk)

    def chunk(c):
        return pl.ds(pl.multiple_of(c * tk, tk), tk)

    for g in range(heads):
        qt_scr[g] = q_ref[:, 2 * g * LANES:(2 * g + 2) * LANES].astype(F32).T.astype(BF16)
    _flash(vt_scr.shape[1],
           lambda c: tuple(_dot(k_scr[g, chunk(c), :], qt_scr[g]) for g in range(heads)),
           lambda c: tuple(vt_scr[g, c] for g in range(heads)),
           stats, bufs, MLA_PAIRS_PER_ITER)
    for g, (_, l_scr, acc) in enumerate(stats):
        o_ref[:, g * MLA_V:(g + 1) * MLA_V] = (acc[...] / l_scr[...]).T.astype(o_ref.dtype)


def _flash_buffers(tq, tk):
    return [pltpu.VMEM((2, tk, tq), F32), pltpu.VMEM((2, tk, tq), BF16),
            pltpu.VMEM((2, 1, tq), F32), pltpu.VMEM((2, 1, tq), F32)]


def _mla_attention(q_lat, kv_lat, kv_ctx, a_lat, a_ctx, kr_blk, geo, tq, tk, heads):
    b, n, ctx = geo
    h = MLA_HEADS
    nt = n // tq
    return pl.pallas_call(
        functools.partial(_mla_kernel, tk=tk, ctx=ctx, heads=heads),
        out_shape=jax.ShapeDtypeStruct((b * n, h * MLA_V), BF16),
        grid=(b, h // heads, nt),
        in_specs=[pl.BlockSpec((tq, heads * 256), lambda bi, hi, t: (bi * nt + t, hi)),
                  pl.BlockSpec((n, heads * 256), lambda bi, hi, t: (bi, hi)),
                  pl.BlockSpec((ctx, heads * 256), lambda bi, hi, t: (bi, hi)),
                  pl.BlockSpec((n, LANES), lambda bi, hi, t: (bi, kr_blk)),
                  pl.BlockSpec((ctx, LANES), lambda bi, hi, t: (bi, kr_blk))],
        out_specs=pl.BlockSpec((tq, heads * MLA_V), lambda bi, hi, t: (bi * nt + t, hi)),
        scratch_shapes=[pltpu.VMEM((heads, ctx + n, 256), BF16),
                        pltpu.VMEM((ctx + n, MLA_V), BF16),
                        pltpu.VMEM((heads, (ctx + n) // tk, MLA_V, tk), BF16),
                        pltpu.VMEM((heads, 256, tq), BF16)]
        + [pltpu.VMEM((1, tq), F32), pltpu.VMEM((1, tq), F32),
           pltpu.VMEM((MLA_V, tq), F32)] * heads
        + _flash_buffers(tq, tk) * heads,
        compiler_params=_cparams(("parallel", "arbitrary", "arbitrary")),
        name="mla_attention",
    )(q_lat, kv_lat, kv_ctx, a_lat, a_ctx)


def _mla_ctx_kernel(q_ref, kv_ref, kr_ref, o_ref):
    q = q_ref[...]
    s = _dot_nt(q[:, 0:LANES], kv_ref[:, 0:LANES]) + _dot_nt(q[:, LANES:2 * LANES], kr_ref[...])
    o_ref[...] = _softmax_once(s, kv_ref[:, LANES:2 * LANES]).astype(o_ref.dtype)


def _mla_ctx_attention(q_ctx, kv_ctx, a_ctx, kr_blk, geo):
    b, n, ctx = geo
    h = MLA_HEADS
    return pl.pallas_call(
        _mla_ctx_kernel,
        out_shape=jax.ShapeDtypeStruct((b * ctx, h * MLA_V), BF16),
        grid=(b, h),
        in_specs=[pl.BlockSpec((ctx, 256), lambda bi, hi: (bi, hi)),
                  pl.BlockSpec((ctx, 256), lambda bi, hi: (bi, hi)),
                  pl.BlockSpec((ctx, LANES), lambda bi, hi: (bi, kr_blk))],
        out_specs=pl.BlockSpec((ctx, MLA_V), lambda bi, hi: (bi, hi)),
        compiler_params=_cparams(("parallel", "arbitrary")),
        name="mla_ctx_attention",
    )(q_ctx, kv_ctx, a_ctx)


def _diff_kernel(q_ref, kl_ref, vl_ref, kc_ref, vc_ref, lam_ref, sub_ref, o_ref,
                 k_scr, v_scr, vt_scr, qt_scr, m0, l0, a0, m1, l1, a1,
                 st0, pt0, alpha0, cmax0, st1, pt1, alpha1, cmax1, *, tk, ctx, lam_init):
    hd = DIFF_HD
    full = slice(0, 2 * hd)

    @pl.when(pl.program_id(2) == 0)
    def _():
        _gather_rows(k_scr, kc_ref, kl_ref, full, full, ctx, ctx)
        _gather_rows(v_scr, vc_ref, vl_ref, full, full, ctx, ctx)
        _transpose_chunks(vt_scr, v_scr, tk)

    def chunk(c):
        return pl.ds(pl.multiple_of(c * tk, tk), tk)

    qt_scr[...] = q_ref[...].astype(F32).T.astype(BF16)

    def scores_t(c):
        return tuple(_dot(k_scr[chunk(c), i * hd:(i + 1) * hd], qt_scr[i * hd:(i + 1) * hd, :])
                     for i in range(2))

    _flash(vt_scr.shape[0], scores_t, lambda c: vt_scr[c],
           [(m0, l0, a0), (m1, l1, a1)], [(st0, pt0, alpha0, cmax0), (st1, pt1, alpha1, cmax1)],
           DIFF_PAIRS_PER_ITER)

    lv = lam_ref[...]
    lam = (jnp.exp(jnp.sum(lv[0:1] * lv[1:2], axis=-1, keepdims=True))
           - jnp.exp(jnp.sum(lv[2:3] * lv[3:4], axis=-1, keepdims=True)) + lam_init)
    o = (a0[...] / l0[...] - lam * (a1[...] / l1[...])).T
    o = o * lax.rsqrt(jnp.mean(o * o, axis=-1, keepdims=True) + EPS) * sub_ref[...]
    o_ref[...] = (o * (1.0 - lam_init)).astype(o_ref.dtype)


def _diff_attention(a_lat, kv_ctx, lam_rows, subln, geo, tq, tk, lam_init):
    b, n, ctx = geo
    h = DIFF_HEADS
    nt = n // tq
    return pl.pallas_call(
        functools.partial(_diff_kernel, tk=tk, ctx=ctx, lam_init=lam_init),
        out_shape=jax.ShapeDtypeStruct((b * n, h * 256), BF16),
        grid=(b, h, nt),
        in_specs=[pl.BlockSpec((tq, 256), lambda bi, hi, t: (bi * nt + t, h + hi)),
                  pl.BlockSpec((n, 256), lambda bi, hi, t: (bi, 2 * h + hi)),
                  pl.BlockSpec((n, 256), lambda bi, hi, t: (bi, 3 * h + hi)),
                  pl.BlockSpec((ctx, 256), lambda bi, hi, t: (bi, hi)),
                  pl.BlockSpec((ctx, 256), lambda bi, hi, t: (bi, h + hi)),
                  pl.BlockSpec((8, LANES), lambda bi, hi, t: (0, 0)),
                  pl.BlockSpec((1, 256), lambda bi, hi, t: (0, 0))],
        out_specs=pl.BlockSpec((tq, 256), lambda bi, hi, t: (bi * nt + t, hi)),
        scratch_shapes=[pltpu.VMEM((ctx + n, 256), BF16), pltpu.VMEM((ctx + n, 256), BF16),
                        pltpu.VMEM(((ctx + n) // tk, 256, tk), BF16),
                        pltpu.VMEM((256, tq), BF16)]
        + [pltpu.VMEM((1, tq), F32), pltpu.VMEM((1, tq), F32), pltpu.VMEM((256, tq), F32)] * 2
        + _flash_buffers(tq, tk) * 2,
        compiler_params=_cparams(("parallel", "arbitrary", "arbitrary")),
        name="diff_attention",
    )(a_lat, a_lat, a_lat, kv_ctx, kv_ctx, lam_rows, subln.reshape(1, 256).astype(F32))


def _half_select(x, half, axis=1):
    idx = lax.broadcasted_iota(jnp.int32, x.shape, axis)
    keep = (idx < 64) if half == 0 else (idx >= 64)
    return jnp.where(keep, x, jnp.zeros_like(x))


def _staged(ref, value):
    ref[...] = value
    return ref[...]


def _swa_kernel(*refs, is_lat, tq, band):
    if is_lat:
        sink_ref, q_ref, kl_ref, vl_ref, kc_ref, vc_ref, o_ref, sc_scr, sb_scr = refs
    else:
        sink_ref, q_ref, kc_ref, vc_ref, o_ref, sc_scr = refs
    t = pl.program_id(1)
    per = SWA_HEADS // SWA_KV_HEADS
    if is_lat:
        n = kl_ref.shape[0]
        start = pl.multiple_of(jnp.clip(t * tq - SWA_WINDOW, 0, n - band), SWA_WINDOW)
        kpos = start + lax.broadcasted_iota(jnp.int32, (band, per * tq), 0)
        qpos = t * tq + lax.broadcasted_iota(jnp.int32, (band, per * tq), 1) % tq
        ok = jnp.abs(qpos - kpos) <= SWA_WINDOW
    for a in range(SWA_KV_HEADS // 2):
        cols = slice(a * LANES, (a + 1) * LANES)
        qs = jnp.concatenate([q_ref[:, (a * per + j) * LANES:(a * per + j + 1) * LANES]
                              for j in range(per)], axis=0)
        kc = kc_ref[:, cols]
        vct = vc_ref[:, cols].astype(F32).T.astype(BF16)
        if is_lat:
            kb = kl_ref[pl.ds(start, band), cols]
            vbt = vl_ref[pl.ds(start, band), cols].astype(F32).T.astype(BF16)
        out_t = None
        for half in range(2):
            sink = jnp.concatenate(
                [jnp.full((1, tq), sink_ref[(a * per + j) * 2 + half] * LOG2E, F32)
                 for j in range(per)], axis=1)
            st_c = _staged(sc_scr, _dot_nt(_half_select(kc, half, 1), qs))
            m = jnp.maximum(jnp.max(st_c, axis=0, keepdims=True), sink)
            if is_lat:
                st_b = _staged(sb_scr, _dot_nt(_half_select(kb, half, 1), qs))
                st_b = jnp.where(ok, st_b, NEG_INF)
                m = jnp.maximum(m, jnp.max(st_b, axis=0, keepdims=True))
            pt_c = jnp.exp2(st_c - m)
            l = jnp.sum(pt_c, axis=0, keepdims=True) + jnp.exp2(sink - m)
            o_t = _dot(_half_select(vct, half, 0), pt_c.astype(BF16))
            if is_lat:
                pt_b = jnp.exp2(st_b - m)
                l = l + jnp.sum(pt_b, axis=0, keepdims=True)
                o_t = o_t + _dot(_half_select(vbt, half, 0), pt_b.astype(BF16))
            o_t = o_t / l
            out_t = o_t if out_t is None else out_t + o_t
        out = out_t.T
        for j in range(per):
            o_ref[:, (a * per + j) * LANES:(a * per + j + 1) * LANES] = (
                out[j * tq:(j + 1) * tq].astype(o_ref.dtype))


def _swa_attention(a_q, a_lat, a_ctx, sink_perm, geo, tq, is_lat):
    b, n, ctx = geo
    width = SWA_HEADS * SWA_HD
    kvw = SWA_KV_HEADS * SWA_HD
    rows = n if is_lat else ctx
    nt = rows // tq
    kblk, vblk = 2 * width // kvw, 2 * width // kvw + 1
    in_specs = [pl.BlockSpec(memory_space=pltpu.SMEM),
                pl.BlockSpec((tq, width), lambda bi, t: (bi * nt + t, 1))]
    args = [sink_perm, a_q]
    if is_lat:
        in_specs += [pl.BlockSpec((n, kvw), lambda bi, t: (bi, kblk)),
                     pl.BlockSpec((n, kvw), lambda bi, t: (bi, vblk))]
        args += [a_lat, a_lat]
    in_specs += [pl.BlockSpec((ctx, kvw), lambda bi, t: (bi, kblk)),
                 pl.BlockSpec((ctx, kvw), lambda bi, t: (bi, vblk))]
    args += [a_ctx, a_ctx]
    band = tq + 2 * SWA_WINDOW
    stacked = SWA_HEADS // SWA_KV_HEADS * tq
    scratch = [pltpu.VMEM((ctx, stacked), F32)] + ([pltpu.VMEM((band, stacked), F32)] if is_lat else [])
    return pl.pallas_call(
        functools.partial(_swa_kernel, is_lat=is_lat, tq=tq, band=band),
        out_shape=jax.ShapeDtypeStruct((b * rows, width), BF16),
        grid=(b, nt),
        in_specs=in_specs,
        out_specs=pl.BlockSpec((tq, width), lambda bi, t: (bi * nt + t, 0)),
        scratch_shapes=scratch,
        compiler_params=_cparams(("parallel", "arbitrary")),
        name="swa_attention" if is_lat else "swa_ctx_attention",
    )(*args)


def _na_kernel(*refs, is_lat, rq, rk, rows):
    if is_lat:
        q_ref, kl_ref, vl_ref, kc_ref, vc_ref, bt_ref, o_ref, sc_scr, s_scr = refs
    else:
        q_ref, kc_ref, vc_ref, o_ref, sc_scr = refs
    t = pl.program_id(2)
    w = GRID_W
    band = rk * w
    q = q_ref[...]
    kc = kc_ref[...]
    vct = vc_ref[...].astype(F32).T.astype(BF16)
    if is_lat:
        r0 = t * rq
        srow = jnp.clip(r0 - NA_WIN_R // 2, 0, rows - rk)
        start = pl.multiple_of(srow * w, w)
        kb = kl_ref[pl.ds(start, band), :]
        vbt = vl_ref[pl.ds(start, band), :].astype(F32).T.astype(BF16)
        lo = lax.broadcasted_iota(jnp.int32, (w, LANES), 1) < w
    out_t = None
    for half in range(2):
        st_c = _staged(sc_scr.at[half], _dot_nt(_half_select(kc, half, 1), q))
        m = jnp.max(st_c, axis=0, keepdims=True)
        if is_lat:
            s_half = s_scr.at[half]
            s_half[...] = _dot_nt(_half_select(kb, half, 1), q)
            for kj in range(rk):
                kr = srow + kj
                for qp in range(rq // 2):
                    ra = r0 + 2 * qp
                    rsa = jnp.clip(ra - NA_WIN_R // 2, 0, rows - NA_WIN_R)
                    rsb = jnp.clip(ra + 1 - NA_WIN_R // 2, 0, rows - NA_WIN_R)
                    e = jnp.clip(kr - ra + NA_WIN_R - 1, 0, 2 * NA_WIN_R - 1)
                    pen0 = jnp.where((kr >= rsa) & (kr < rsa + NA_WIN_R), 0.0, NEG_INF)
                    pen1 = jnp.where((kr >= rsb) & (kr < rsb + NA_WIN_R), 0.0, NEG_INF)
                    blk = (slice(kj * w, (kj + 1) * w), slice(qp * LANES, (qp + 1) * LANES))
                    s_half[blk] = s_half[blk] + (bt_ref[half, e] * LOG2E + jnp.where(lo, pen0, pen1))
            st_b = s_half[...]
            m = jnp.maximum(m, jnp.max(st_b, axis=0, keepdims=True))
        pt_c = jnp.exp2(st_c - m)
        l = jnp.sum(pt_c, axis=0, keepdims=True)
        o_t = _dot(_half_select(vct, half, 0), pt_c.astype(BF16))
        if is_lat:
            pt_b = jnp.exp2(st_b - m)
            l = l + jnp.sum(pt_b, axis=0, keepdims=True)
            o_t = o_t + _dot(_half_select(vbt, half, 0), pt_b.astype(BF16))
        o_t = o_t / l
        out_t = o_t if out_t is None else out_t + o_t
    o_ref[...] = out_t.T.astype(o_ref.dtype)


def _na_attention(a_q, a_lat, a_ctx, bias_tab, geo, rq, is_lat):
    b, n, ctx = geo
    width = NA_HEADS * NA_HD
    pairs = width // LANES
    tq = rq * GRID_W
    rk = rq + NA_WIN_R
    grid_rows = n // GRID_W
    rows = n if is_lat else ctx
    assert grid_rows >= rk and rows % tq == 0
    nt = rows // tq
    qo, ko, vo = pairs, 2 * pairs, 3 * pairs
    in_specs = [pl.BlockSpec((tq, LANES), lambda bi, pi, t: (bi * nt + t, qo + pi))]
    args = [a_q]
    if is_lat:
        in_specs += [pl.BlockSpec((n, LANES), lambda bi, pi, t: (bi, ko + pi)),
                     pl.BlockSpec((n, LANES), lambda bi, pi, t: (bi, vo + pi))]
        args += [a_lat, a_lat]
    in_specs += [pl.BlockSpec((ctx, LANES), lambda bi, pi, t: (bi, ko + pi)),
                 pl.BlockSpec((ctx, LANES), lambda bi, pi, t: (bi, vo + pi))]
    args += [a_ctx, a_ctx]
    scratch = [pltpu.VMEM((2, ctx, tq), F32)]
    if is_lat:
        in_specs.append(pl.BlockSpec((2, 2 * NA_WIN_R, GRID_W, LANES),
                                     lambda bi, pi, t: (pi, 0, 0, 0)))
        args.append(bias_tab)
        scratch.append(pltpu.VMEM((2, rk * GRID_W, tq), F32))
    return pl.pallas_call(
        functools.partial(_na_kernel, is_lat=is_lat, rq=rq, rk=rk, rows=grid_rows),
        out_shape=jax.ShapeDtypeStruct((b * rows, width), BF16),
        grid=(b, pairs, nt),
        in_specs=in_specs,
        out_specs=pl.BlockSpec((tq, LANES), lambda bi, pi, t: (bi * nt + t, pi)),
        scratch_shapes=scratch,
        compiler_params=_cparams(("parallel", "arbitrary", "arbitrary")),
        name="na_attention" if is_lat else "na_ctx_attention",
    )(*args)


def _rope_tables(n, d_rot):
    t = jnp.arange(n)
    row = (t // GRID_W).astype(F32)
    col = (t % GRID_W).astype(F32)
    d_axis = d_rot // 2
    inv = ROPE_BASE ** (-jnp.arange(0, d_axis, 2, dtype=F32) / d_axis)
    ang = jnp.concatenate([row[:, None] * inv, col[:, None] * inv], axis=-1)
    cos, sin = jnp.cos(ang), jnp.sin(ang)
    reps = LANES // d_rot
    return (jnp.tile(cos, (1, 2 * reps)),
            jnp.tile(jnp.concatenate([-sin, sin], axis=-1), (1, reps)))


def _na_bias_table(rpb):
    col = jnp.arange(GRID_W)
    dc = jnp.clip(col[:, None] - col[None, :], -(NA_WIN_C - 1), NA_WIN_C - 1) + (NA_WIN_C - 1)
    cs = jnp.clip(col - NA_WIN_C // 2, 0, GRID_W - NA_WIN_C)[None, :]
    col_ok = (col[:, None] >= cs) & (col[:, None] < cs + NA_WIN_C)
    bt = jnp.where(col_ok, rpb[:, :, dc], NEG_INF)
    bt = jnp.pad(bt, ((0, 0), (1, 1), (0, 0), (0, 0)))
    return jnp.concatenate([bt[:, 1:], bt[:, :-1]], axis=-1).astype(F32)


def _tile_plan(b, n, n_ctx, d):
    tm = _largest_divisor(math.gcd(n, b * n_ctx), (512, 256, 128))
    return dict(
        tm=tm,
        tn_d=_largest_divisor(d, (2048, 1024, 512, 256, 128)),
        tq_flash=_largest_divisor(n, (512, 256, 128)),
        tq_diff=_largest_divisor(n, (512, 256, 128)),
        tk=_largest_divisor(n + n_ctx, (768, 640, 512, 384, 256, 128)),
        tk_mla=_largest_divisor(n + n_ctx, (768, 640, 512, 384, 256, 128)),
        heads_mla=2,
        tq_swa=128,
        rq_na=4,
    )


def kernel(x, c, ctx, c_ctx, l0_mod_w, l0_mod_b, l0_norm, l0_w_in, l0_q_norm, l0_w_qb, l0_kv_norm, l0_w_kvb, l0_w_out, l1_mod_w, l1_mod_b, l1_norm, l1_w_in, l1_sink, l1_w_out, l2_mod_w, l2_mod_b, l2_norm, l2_w_in, l2_rpb, l2_w_out, l3_mod_w, l3_mod_b, l3_norm, l3_w_in, l3_lam_q1, l3_lam_k1, l3_lam_q2, l3_lam_k2, l3_subln, l3_w_out, final_norm):
    b, n, d = x.shape
    n_ctx = ctx.shape[1]
    geo = (b, n, n_ctx)
    assert b < 8 and n % GRID_W == 0
    tp = _tile_plan(b, n, n_ctx, d)
    tm, tn_d = tp["tm"], tp["tn_d"]
    lat, cx = ("batch", n // tm), ("const", b)

    x_lat = x.reshape(b * n, d)
    x_ctx = ctx.reshape(b * n_ctx, d)
    c_rows = jnp.concatenate([c, c_ctx[None, :], jnp.zeros((8 - b - 1, d), F32)], axis=0)
    rope64 = _rope_tables(n, 64) + (n // tm,)
    rope128 = _rope_tables(n, 128) + (n // tm,)

    def groups(*spans):
        out = []
        for cols, code, scale in spans:
            out += [(code, scale)] * (cols // LANES)
        return tuple(out)

    def both(fn):
        return fn(x_lat, lat, True), fn(x_ctx, cx, False)

    mod = _modulation(c_rows, l0_mod_w, l0_mod_b)
    r = MLA_RANK
    w_in = jnp.concatenate([l0_w_in[:, 2 * r + MLA_ROPE:], l0_w_in[:, :2 * r + MLA_ROPE],
                            jnp.zeros((d, LANES - MLA_ROPE), F32)], axis=1).astype(BF16)
    width = MLA_HEADS * MLA_V
    g0 = groups((width + 2 * r, PLAIN, 1.0), (LANES, ROPE64, 1.0))
    a_lat, a_ctx = both(lambda xs, grp, is_lat: _projection(
        xs, 0, d, l0_norm, mod, grp, w_in, rope64 if is_lat else None, g0, tm, 640, "mla_in"))
    qk = MLA_NOPE + MLA_ROPE
    w_q = l0_w_qb.reshape(r, MLA_HEADS, qk)
    w_q = jnp.pad(w_q, ((0, 0), (0, 0), (0, 256 - qk))).reshape(r, MLA_HEADS * 256).astype(BF16)
    q_scale = qk ** -0.5 * LOG2E
    gq = groups((LANES, PLAIN, q_scale), (LANES, ROPE64, q_scale)) * MLA_HEADS
    gkv = groups((MLA_HEADS * 256, PLAIN, 1.0))
    w_kv = l0_w_kvb.astype(BF16)
    q_lat, kv_lat = (
        _projection(a_lat, width // r, r, l0_q_norm, None, None, w_q, rope64, gq, tm, 1024, "mla_q"),
        _projection(a_lat, width // r + 1, r, l0_kv_norm, None, None, w_kv, None, gkv, tm, 1024,
                    "mla_kv"))
    q_ctx, kv_ctx = (
        _projection(a_ctx, width // r, r, l0_q_norm, None, None, w_q, None, gq, tm, 1024,
                    "mla_q_ctx"),
        _projection(a_ctx, width // r + 1, r, l0_kv_norm, None, None, w_kv, None, gkv, tm, 1024,
                    "mla_kv_ctx"))
    kr_blk = (width + 2 * r) // LANES
    o_lat = _mla_attention(q_lat, kv_lat, kv_ctx, a_lat, a_ctx, kr_blk, geo, tp["tq_flash"],
                           tp["tk_mla"], tp["heads_mla"])
    o_ctx = _mla_ctx_attention(q_ctx, kv_ctx, a_ctx, kr_blk, geo)
    w_out = l0_w_out.astype(BF16)
    x_lat = _out_projection(o_lat, a_lat, w_out, x_lat, mod, lat, tm, tn_d, "mla_out")
    x_ctx = _out_projection(o_ctx, a_ctx, w_out, x_ctx, mod, cx, tm, tn_d, "mla_out")

    mod = _modulation(c_rows, l1_mod_w, l1_mod_b)
    per = SWA_HEADS // SWA_KV_HEADS
    order = [g * per + j for a in range(SWA_KV_HEADS // 2) for j in range(per)
             for g in (2 * a, 2 * a + 1)]
    col_perm = jnp.asarray([hh * SWA_HD + e for hh in order for e in range(SWA_HD)])
    width = SWA_HEADS * SWA_HD
    kvw = SWA_KV_HEADS * SWA_HD
    w_q, w_k, w_v, w_g = (l1_w_in[:, :width], l1_w_in[:, width:width + kvw],
                          l1_w_in[:, width + kvw:width + 2 * kvw], l1_w_in[:, width + 2 * kvw:])
    w_in = jnp.concatenate([w_g[:, col_perm], w_q[:, col_perm], w_k, w_v], axis=1).astype(BF16)
    g1 = groups((width, PLAIN, 1.0), (width, ROPE64, SWA_HD ** -0.5 * LOG2E),
                (kvw, ROPE64, 1.0), (kvw, PLAIN, 1.0))
    a_lat, a_ctx = both(lambda xs, grp, is_lat: _projection(
        xs, 0, d, l1_norm, mod, grp, w_in, rope64 if is_lat else None, g1, tm, 1152, "swa_in"))
    sink_perm = l1_sink[jnp.asarray(order)]
    tq = tp["tq_swa"]
    o_lat = _swa_attention(a_lat, a_lat, a_ctx, sink_perm, geo, tq, True)
    o_ctx = _swa_attention(a_ctx, a_lat, a_ctx, sink_perm, geo, tq, False)
    w_out = l1_w_out[col_perm, :].astype(BF16)
    x_lat = _out_projection(o_lat, a_lat, w_out, x_lat, mod, lat, tm, tn_d, "swa_out")
    x_ctx = _out_projection(o_ctx, a_ctx, w_out, x_ctx, mod, cx, tm, tn_d, "swa_out")

    mod = _modulation(c_rows, l2_mod_w, l2_mod_b)
    width = NA_HEADS * NA_HD
    w_in = jnp.concatenate([l2_w_in[:, 3 * width:], l2_w_in[:, :3 * width]], axis=1).astype(BF16)
    g2 = groups((width, PLAIN, 1.0), (width, PLAIN, NA_HD ** -0.5 * LOG2E), (2 * width, PLAIN, 1.0))
    a_lat, a_ctx = both(lambda xs, grp, is_lat: _projection(
        xs, 0, d, l2_norm, mod, grp, w_in, None, g2, tm, 1024, "na_in"))
    bias_tab = _na_bias_table(l2_rpb)
    o_lat = _na_attention(a_lat, a_lat, a_ctx, bias_tab, geo, 2 * tp["rq_na"], True)
    o_ctx = _na_attention(a_ctx, a_lat, a_ctx, bias_tab, geo, tp["rq_na"], False)
    w_out = l2_w_out.astype(BF16)
    x_lat = _out_projection(o_lat, a_lat, w_out, x_lat, mod, lat, tm, tn_d, "na_out")
    x_ctx = _out_projection(o_ctx, a_ctx, w_out, x_ctx, mod, cx, tm, tn_d, "na_out")

    mod = _modulation(c_rows, l3_mod_w, l3_mod_b)
    width = DIFF_HEADS * 2 * DIFF_HD
    w_in = jnp.concatenate([l3_w_in[:, 3 * width:], l3_w_in[:, :3 * width]], axis=1).astype(BF16)
    g3 = groups((width, PLAIN, 1.0), (width, ROPE128, DIFF_HD ** -0.5 * LOG2E),
                (width, ROPE128, 1.0), (width, PLAIN, 1.0))
    a_lat = _projection(x_lat, 0, d, l3_norm, mod, lat, w_in, rope128, g3, tm, 1024, "diff_in")
    kv_ctx = _projection(x_ctx, 0, d, l3_norm, mod, cx, w_in[:, 2 * width:], None, g3[2 * width // LANES:],
                         tm, 1024, "diff_in")
    lam_init = 0.8 - 0.6 * math.exp(-0.3 * 3)
    lam_rows = jnp.concatenate([jnp.stack([l3_lam_q1, l3_lam_k1, l3_lam_q2, l3_lam_k2]),
                                jnp.zeros((4, DIFF_HD), F32)], axis=0)
    o_lat = _diff_attention(a_lat, kv_ctx, lam_rows, l3_subln, geo, tp["tq_diff"], tp["tk"],
                            lam_init)
    x_lat = _out_projection(o_lat, a_lat, l3_w_out.astype(BF16), x_lat, mod, lat, tm, tn_d, "diff_out")

    return _final_norm(x_lat, final_norm, tm).reshape(b, n, d)
```

```python
import functools
import math

import jax
import jax.numpy as jnp
from jax import lax
from jax.experimental import pallas as pl
from jax.experimental.pallas import tpu as pltpu

F32 = jnp.float32
BF16 = jnp.bfloat16

LANES = 128
LOG2E = 1.4426950408889634
NEG_INF = -1e30
EPS = 1e-6
ROPE_BASE = 10000.0
GRID_W = 64
VMEM_LIMIT = 56 * 1024 * 1024

MLA_HEADS, MLA_NOPE, MLA_ROPE, MLA_V, MLA_RANK = 16, 128, 64, 128, 512
SWA_HEADS, SWA_KV_HEADS, SWA_HD, SWA_WINDOW = 32, 4, 64, 128
NA_HEADS, NA_HD, NA_WIN_R, NA_WIN_C = 32, 64, 8, 16
DIFF_HEADS, DIFF_HD = 8, 128

PLAIN, ROPE64, ROPE128 = 0, 1, 2
MLA_FLASH_SLOTS, DIFF_FLASH_SLOTS = 2, 2


def _cparams(sem):
    return pltpu.CompilerParams(dimension_semantics=sem, vmem_limit_bytes=VMEM_LIMIT)


def _dot_nt(a, b):
    return lax.dot_general(a, b, (((1,), (1,)), ((), ())), preferred_element_type=F32)


def _dot(a, b):
    return jnp.dot(a, b, preferred_element_type=F32)


def _rep(x, n):
    return x if n == 1 else jnp.concatenate([x] * n, axis=1)


def _largest_divisor(total, candidates):
    return next(c for c in candidates if total % c == 0)


def _mod_kernel(c_ref, w_ref, b_ref, o_ref):
    c = c_ref[...]
    s = c * jax.nn.sigmoid(c)
    o_ref[...] = _dot(s, w_ref[...]) + b_ref[...]


def _modulation(c_rows, mod_w, mod_b):
    r, d = c_rows.shape
    n = mod_w.shape[1]
    tn = math.gcd(n, 512)
    out = pl.pallas_call(
        _mod_kernel,
        out_shape=jax.ShapeDtypeStruct((r, n), F32),
        grid=(n // tn,),
        in_specs=[pl.BlockSpec((r, d), lambda j: (0, 0)),
                  pl.BlockSpec((d, tn), lambda j: (0, j)),
                  pl.BlockSpec((1, tn), lambda j: (0, j))],
        out_specs=pl.BlockSpec((r, tn), lambda j: (0, j)),
        compiler_params=_cparams(("arbitrary",)),
        name="modulation",
    )(c_rows, mod_w, mod_b.reshape(1, n))
    return out.reshape(r, 1, n)


def _mod_row(group):
    kind, val = group
    return (lambda i: i // val) if kind == "batch" else (lambda i: val)


def _rope_group(a, cos, sin, code):
    if code == ROPE64:
        lane = lax.broadcasted_iota(jnp.int32, a.shape, 1)
        first = (lane % 64) < 32
        partner = jnp.where(first, pltpu.roll(a, 96, 1), pltpu.roll(a, 32, 1))
    else:
        partner = pltpu.roll(a, 64, 1)
    return a * cos + partner * sin


def _proj_kernel(*refs, has_mod, has_rope, tile_patterns, d_mod):
    it = iter(refs)
    x_ref, g_ref = next(it), next(it)
    mod_ref = next(it) if has_mod else None
    w_ref = next(it)
    cos_ref, sin_ref = (next(it), next(it)) if has_rope else (None, None)
    o_ref, h_scr = next(it), next(it)
    j = pl.program_id(1)

    @pl.when(j == 0)
    def _():
        x = x_ref[...].astype(F32)
        y = x * lax.rsqrt(jnp.mean(x * x, axis=-1, keepdims=True) + EPS) * g_ref[...]
        if has_mod:
            shift = mod_ref[0, :, 0:d_mod]
            scale = mod_ref[0, :, d_mod:2 * d_mod]
            y = y * (1.0 + scale) + shift
        h_scr[...] = y.astype(BF16)

    def epilogue(pattern):
        acc = _dot(h_scr[...], w_ref[...])
        for gi, (code, scale) in enumerate(pattern):
            a = acc[:, gi * LANES:(gi + 1) * LANES]
            if code != PLAIN:
                a = _rope_group(a, cos_ref[...], sin_ref[...], code)
            if scale != 1.0:
                a = a * scale
            o_ref[:, gi * LANES:(gi + 1) * LANES] = a.astype(o_ref.dtype)

    distinct = sorted(set(tile_patterns), key=tile_patterns.index)
    if len(distinct) == 1:
        epilogue(distinct[0])
    else:
        for pat in distinct:
            tiles = [jj for jj, p in enumerate(tile_patterns) if p == pat]
            cond = functools.reduce(jnp.logical_or, [j == jj for jj in tiles])
            pl.when(cond)(functools.partial(epilogue, pat))


def _projection(x, col_blk, kd, gain, mod, group, w, rope, groups, tm, tn, name):
    m = x.shape[0]
    n_out = w.shape[1]
    tm = min(tm, m)
    if group is not None and group[0] == "const":
        name += "_ctx"
    assert m % tm == 0 and n_out % tn == 0 and tn % LANES == 0 and len(groups) == n_out // LANES
    if rope is None:
        groups = tuple((PLAIN, scale) for _, scale in groups)
    has_mod = mod is not None
    has_rope = any(code != PLAIN for code, _ in groups)
    gpt = tn // LANES
    tile_patterns = tuple(tuple(groups[t * gpt:(t + 1) * gpt]) for t in range(n_out // tn))

    in_specs = [pl.BlockSpec((tm, kd), lambda i, j: (i, col_blk)),
                pl.BlockSpec((1, kd), lambda i, j: (0, 0))]
    args = [x, gain.reshape(1, kd).astype(F32)]
    d_mod = 0
    if has_mod:
        d_mod = mod.shape[2] // 3
        row = _mod_row(group)
        in_specs.append(pl.BlockSpec((1, 1, 3 * d_mod), lambda i, j: (row(i), 0, 0)))
        args.append(mod)
    in_specs.append(pl.BlockSpec((kd, tn), lambda i, j: (0, j)))
    args.append(w)
    if has_rope:
        cos, sin, tiles_per_batch = rope
        in_specs += [pl.BlockSpec((tm, LANES), lambda i, j: (i % tiles_per_batch, 0))] * 2
        args += [cos, sin]
    kern = functools.partial(_proj_kernel, has_mod=has_mod, has_rope=has_rope,
                             tile_patterns=tile_patterns, d_mod=d_mod)
    return pl.pallas_call(
        kern,
        out_shape=jax.ShapeDtypeStruct((m, n_out), BF16),
        grid=(m // tm, n_out // tn),
        in_specs=in_specs,
        out_specs=pl.BlockSpec((tm, tn), lambda i, j: (i, j)),
        scratch_shapes=[pltpu.VMEM((tm, kd), BF16)],
        compiler_params=_cparams(("parallel", "arbitrary")),
        name=name,
    )(*args)


def _out_kernel(o_ref, gate_ref, w_ref, x_ref, gm_ref, y_ref, u_scr):
    @pl.when(pl.program_id(1) == 0)
    def _():
        g = gate_ref[...].astype(F32)
        u_scr[...] = (o_ref[...].astype(F32) * (g * jax.nn.sigmoid(g))).astype(BF16)

    y_ref[...] = x_ref[...] + gm_ref[0] * _dot(u_scr[...], w_ref[...])


def _out_projection(o, proj, w_out, x, mod, group, tm, tn, name):
    m, d = x.shape
    width = w_out.shape[0]
    tm = min(tm, m)
    if group[0] == "const":
        name += "_ctx"
    row = _mod_row(group)
    gate_blk = 2 * d // tn
    return pl.pallas_call(
        _out_kernel,
        out_shape=jax.ShapeDtypeStruct((m, d), F32),
        grid=(m // tm, d // tn),
        in_specs=[pl.BlockSpec((tm, width), lambda i, j: (i, 0)),
                  pl.BlockSpec((tm, width), lambda i, j: (i, 0)),
                  pl.BlockSpec((width, tn), lambda i, j: (0, j)),
                  pl.BlockSpec((tm, tn), lambda i, j: (i, j)),
                  pl.BlockSpec((1, 1, tn), lambda i, j: (row(i), 0, gate_blk + j))],
        out_specs=pl.BlockSpec((tm, tn), lambda i, j: (i, j)),
        scratch_shapes=[pltpu.VMEM((tm, width), BF16)],
        input_output_aliases={3: 0},
        compiler_params=_cparams(("parallel", "arbitrary")),
        name=name,
    )(o, proj, w_out, x, mod)


def _norm_kernel(x_ref, g_ref, o_ref):
    x = x_ref[...]
    o_ref[...] = x * lax.rsqrt(jnp.mean(x * x, axis=-1, keepdims=True) + EPS) * g_ref[...]


def _final_norm(x, gain, tm):
    m, d = x.shape
    return pl.pallas_call(
        _norm_kernel,
        out_shape=jax.ShapeDtypeStruct((m, d), F32),
        grid=(m // tm,),
        in_specs=[pl.BlockSpec((tm, d), lambda i: (i, 0)),
                  pl.BlockSpec((1, d), lambda i: (0, 0))],
        out_specs=pl.BlockSpec((tm, d), lambda i: (i, 0)),
        compiler_params=_cparams(("parallel",)),
        name="final_norm",
    )(x, gain.reshape(1, d))


def _flash(n_chunks, scores_t, values_t, stats, bufs):
    slots = bufs[0][0].shape[0]
    for m_scr, l_scr, acc in stats:
        m_scr[...] = jnp.full(m_scr.shape, NEG_INF, F32)
        l_scr[...] = jnp.zeros(l_scr.shape, F32)
        acc[...] = jnp.zeros(acc.shape, F32)

    def load_scores(c, slot):
        for s, (st, _, _) in zip(scores_t(c), bufs):
            st[slot] = s

    def softmax(slot):
        for (m_scr, l_scr, _), (st, pt, alpha_buf) in zip(stats, bufs):
            s = st[slot]
            m_prev = m_scr[...]
            m_new = jnp.maximum(m_prev, jnp.max(s, axis=0, keepdims=True))
            alpha = jnp.exp2(m_prev - m_new)
            p = jnp.exp2(s - m_new)
            l_scr[...] = alpha * l_scr[...] + jnp.sum(p, axis=0, keepdims=True)
            m_scr[...] = m_new
            alpha_buf[slot] = alpha
            pt[slot] = p.astype(pt.dtype)

    def weighted_values(c, slot):
        vts = values_t(c)
        if not isinstance(vts, tuple):
            vts = (vts,) * len(stats)
        for vt, (_, _, acc), (_, pt, alpha_buf) in zip(vts, stats, bufs):
            acc[...] = alpha_buf[slot] * acc[...] + _dot(vt, pt[slot])

    def step(c, k, first, last):
        if not last:
            load_scores(c + 1, (k + 1) % slots)
        if not first:
            weighted_values(c - 1, (k - 1) % slots)
        softmax(k)

    load_scores(0, 0)
    step(0, 0, True, n_chunks == 1)
    iters = max(n_chunks - 2, 0) // slots

    def body(i, carry):
        for k in range(slots):
            step(1 + slots * i + k, (1 + k) % slots, False, False)
        return carry
    lax.fori_loop(0, iters, body, 0)
    for c in range(1 + slots * iters, n_chunks):
        step(c, c % slots, False, c == n_chunks - 1)
    weighted_values(n_chunks - 1, (n_chunks - 1) % slots)


def _transpose_chunks(dst, src, tk):
    def body(c, carry):
        rows = pl.ds(pl.multiple_of(c * tk, tk), tk)
        dst[c] = src[rows, :].astype(F32).T.astype(dst.dtype)
        return carry
    lax.fori_loop(0, dst.shape[0], body, 0)


def _softmax_once(s, v):
    m = jnp.max(s, axis=-1, keepdims=True)
    p = jnp.exp2(s - m)
    return _dot(p.astype(BF16), v) / jnp.sum(p, axis=-1, keepdims=True)


def _gather_rows(dst, src_ctx, src_lat, dst_cols, src_cols, ctx, step):
    dst[0:ctx, dst_cols] = src_ctx[:, src_cols]

    def copy(c, carry):
        r = pl.multiple_of(c * step, step)
        dst[pl.ds(ctx + r, step), dst_cols] = src_lat[pl.ds(r, step), src_cols]
        return carry
    lax.fori_loop(0, src_lat.shape[0] // step, copy, 0)


def _mla_kernel(q_ref, kvl_ref, kvc_ref, krl_ref, krc_ref, o_ref,
                k_scr, v_scr, vt_scr, qt_scr, *stat_and_bufs, tk, ctx, heads):
    lo, hi = slice(0, LANES), slice(LANES, 2 * LANES)
    stats = [stat_and_bufs[3 * g:3 * g + 3] for g in range(heads)]
    bufs = [stat_and_bufs[3 * heads + 3 * g:3 * heads + 3 * g + 3] for g in range(heads)]

    @pl.when(pl.program_id(2) == 0)
    def _():
        for g in range(heads):
            kn = slice(2 * g * LANES, (2 * g + 1) * LANES)
            vv = slice((2 * g + 1) * LANES, (2 * g + 2) * LANES)
            _gather_rows(k_scr.at[g], kvc_ref, kvl_ref, lo, kn, ctx, ctx)
            _gather_rows(k_scr.at[g], krc_ref, krl_ref, hi, lo, ctx, ctx)
            _gather_rows(v_scr, kvc_ref, kvl_ref, lo, vv, ctx, ctx)
            _transpose_chunks(vt_scr.at[g], v_scr, tk)

    def chunk(c):
        return pl.ds(pl.multiple_of(c * tk, tk), tk)

    for g in range(heads):
        qt_scr[g] = q_ref[:, 2 * g * LANES:(2 * g + 2) * LANES].astype(F32).T.astype(BF16)
    _flash(vt_scr.shape[1],
           lambda c: tuple(_dot(k_scr[g, chunk(c), :], qt_scr[g]) for g in range(heads)),
           lambda c: tuple(vt_scr[g, c] for g in range(heads)),
           stats, bufs)
    for g, (_, l_scr, acc) in enumerate(stats):
        o_ref[:, g * MLA_V:(g + 1) * MLA_V] = (acc[...] / l_scr[...]).T.astype(o_ref.dtype)


def _flash_buffers(tq, tk, slots):
    return [pltpu.VMEM((slots, tk, tq), F32), pltpu.VMEM((slots, tk, tq), BF16),
            pltpu.VMEM((slots, 1, tq), F32)]


def _mla_attention(q_lat, kv_lat, kv_ctx, a_lat, a_ctx, kr_blk, geo, tq, tk, heads):
    b, n, ctx = geo
    h = MLA_HEADS
    nt = n // tq
    return pl.pallas_call(
        functools.partial(_mla_kernel, tk=tk, ctx=ctx, heads=heads),
        out_shape=jax.ShapeDtypeStruct((b * n, h * MLA_V), BF16),
        grid=(b, h // heads, nt),
        in_specs=[pl.BlockSpec((tq, heads * 256), lambda bi, hi, t: (bi * nt + t, hi)),
                  pl.BlockSpec((n, heads * 256), lambda bi, hi, t: (bi, hi)),
                  pl.BlockSpec((ctx, heads * 256), lambda bi, hi, t: (bi, hi)),
                  pl.BlockSpec((n, LANES), lambda bi, hi, t: (bi, kr_blk)),
                  pl.BlockSpec((ctx, LANES), lambda bi, hi, t: (bi, kr_blk))],
        out_specs=pl.BlockSpec((tq, heads * MLA_V), lambda bi, hi, t: (bi * nt + t, hi)),
        scratch_shapes=[pltpu.VMEM((heads, ctx + n, 256), BF16),
                        pltpu.VMEM((ctx + n, MLA_V), BF16),
                        pltpu.VMEM((heads, (ctx + n) // tk, MLA_V, tk), BF16),
                        pltpu.VMEM((heads, 256, tq), BF16)]
        + [pltpu.VMEM((1, tq), F32), pltpu.VMEM((1, tq), F32),
           pltpu.VMEM((MLA_V, tq), F32)] * heads
        + _flash_buffers(tq, tk, MLA_FLASH_SLOTS) * heads,
        compiler_params=_cparams(("parallel", "arbitrary", "arbitrary")),
        name="mla_attention",
    )(q_lat, kv_lat, kv_ctx, a_lat, a_ctx)


def _mla_ctx_kernel(q_ref, kv_ref, kr_ref, o_ref):
    q = q_ref[...]
    s = _dot_nt(q[:, 0:LANES], kv_ref[:, 0:LANES]) + _dot_nt(q[:, LANES:2 * LANES], kr_ref[...])
    o_ref[...] = _softmax_once(s, kv_ref[:, LANES:2 * LANES]).astype(o_ref.dtype)


def _mla_ctx_attention(q_ctx, kv_ctx, a_ctx, kr_blk, geo):
    b, n, ctx = geo
    h = MLA_HEADS
    return pl.pallas_call(
        _mla_ctx_kernel,
        out_shape=jax.ShapeDtypeStruct((b * ctx, h * MLA_V), BF16),
        grid=(b, h),
        in_specs=[pl.BlockSpec((ctx, 256), lambda bi, hi: (bi, hi)),
                  pl.BlockSpec((ctx, 256), lambda bi, hi: (bi, hi)),
                  pl.BlockSpec((ctx, LANES), lambda bi, hi: (bi, kr_blk))],
        out_specs=pl.BlockSpec((ctx, MLA_V), lambda bi, hi: (bi, hi)),
        compiler_params=_cparams(("parallel", "arbitrary")),
        name="mla_ctx_attention",
    )(q_ctx, kv_ctx, a_ctx)


def _diff_kernel(q_ref, kl_ref, vl_ref, kc_ref, vc_ref, lam_ref, sub_ref, o_ref,
                 k_scr, v_scr, vt_scr, qt_scr, m0, l0, a0, m1, l1, a1,
                 st0, pt0, alpha0, st1, pt1, alpha1, *, tk, ctx, lam_init):
    hd = DIFF_HD
    full = slice(0, 2 * hd)

    @pl.when(pl.program_id(2) == 0)
    def _():
        _gather_rows(k_scr, kc_ref, kl_ref, full, full, ctx, ctx)
        _gather_rows(v_scr, vc_ref, vl_ref, full, full, ctx, ctx)
        _transpose_chunks(vt_scr, v_scr, tk)

    def chunk(c):
        return pl.ds(pl.multiple_of(c * tk, tk), tk)

    qt_scr[...] = q_ref[...].astype(F32).T.astype(BF16)

    def scores_t(c):
        return tuple(_dot(k_scr[chunk(c), i * hd:(i + 1) * hd], qt_scr[i * hd:(i + 1) * hd, :])
                     for i in range(2))

    _flash(vt_scr.shape[0], scores_t, lambda c: vt_scr[c],
           [(m0, l0, a0), (m1, l1, a1)], [(st0, pt0, alpha0), (st1, pt1, alpha1)])

    lv = lam_ref[...]
    lam = (jnp.exp(jnp.sum(lv[0:1] * lv[1:2], axis=-1, keepdims=True))
           - jnp.exp(jnp.sum(lv[2:3] * lv[3:4], axis=-1, keepdims=True)) + lam_init)
    o = (a0[...] / l0[...] - lam * (a1[...] / l1[...])).T
    o = o * lax.rsqrt(jnp.mean(o * o, axis=-1, keepdims=True) + EPS) * sub_ref[...]
    o_ref[...] = (o * (1.0 - lam_init)).astype(o_ref.dtype)


def _diff_attention(a_lat, kv_ctx, lam_rows, subln, geo, tq, tk, lam_init):
    b, n, ctx = geo
    h = DIFF_HEADS
    nt = n // tq
    return pl.pallas_call(
        functools.partial(_diff_kernel, tk=tk, ctx=ctx, lam_init=lam_init),
        out_shape=jax.ShapeDtypeStruct((b * n, h * 256), BF16),
        grid=(b, h, nt),
        in_specs=[pl.BlockSpec((tq, 256), lambda bi, hi, t: (bi * nt + t, h + hi)),
                  pl.BlockSpec((n, 256), lambda bi, hi, t: (bi, 2 * h + hi)),
                  pl.BlockSpec((n, 256), lambda bi, hi, t: (bi, 3 * h + hi)),
                  pl.BlockSpec((ctx, 256), lambda bi, hi, t: (bi, hi)),
                  pl.BlockSpec((ctx, 256), lambda bi, hi, t: (bi, h + hi)),
                  pl.BlockSpec((8, LANES), lambda bi, hi, t: (0, 0)),
                  pl.BlockSpec((1, 256), lambda bi, hi, t: (0, 0))],
        out_specs=pl.BlockSpec((tq, 256), lambda bi, hi, t: (bi * nt + t, hi)),
        scratch_shapes=[pltpu.VMEM((ctx + n, 256), BF16), pltpu.VMEM((ctx + n, 256), BF16),
                        pltpu.VMEM(((ctx + n) // tk, 256, tk), BF16),
                        pltpu.VMEM((256, tq), BF16)]
        + [pltpu.VMEM((1, tq), F32), pltpu.VMEM((1, tq), F32), pltpu.VMEM((256, tq), F32)] * 2
        + _flash_buffers(tq, tk, DIFF_FLASH_SLOTS) * 2,
        compiler_params=_cparams(("parallel", "arbitrary", "arbitrary")),
        name="diff_attention",
    )(a_lat, a_lat, a_lat, kv_ctx, kv_ctx, lam_rows, subln.reshape(1, 256).astype(F32))


def _half_select(x, half, axis=1):
    idx = lax.broadcasted_iota(jnp.int32, x.shape, axis)
    keep = (idx < 64) if half == 0 else (idx >= 64)
    return jnp.where(keep, x, jnp.zeros_like(x))


def _swa_kernel(*refs, is_lat, tq, band):
    if is_lat:
        sink_ref, q_ref, kl_ref, vl_ref, kc_ref, vc_ref, o_ref = refs
    else:
        sink_ref, q_ref, kc_ref, vc_ref, o_ref = refs
    t = pl.program_id(1)
    per = SWA_HEADS // SWA_KV_HEADS
    if is_lat:
        n = kl_ref.shape[0]
        start = pl.multiple_of(jnp.clip(t * tq - SWA_WINDOW, 0, n - band), SWA_WINDOW)
        kpos = start + lax.broadcasted_iota(jnp.int32, (band, per * tq), 0)
        qpos = t * tq + lax.broadcasted_iota(jnp.int32, (band, per * tq), 1) % tq
        ok = jnp.abs(qpos - kpos) <= SWA_WINDOW
    for a in range(SWA_KV_HEADS // 2):
        cols = slice(a * LANES, (a + 1) * LANES)
        qs = jnp.concatenate([q_ref[:, (a * per + j) * LANES:(a * per + j + 1) * LANES]
                              for j in range(per)], axis=0)
        kc = kc_ref[:, cols]
        vct = vc_ref[:, cols].astype(F32).T.astype(BF16)
        if is_lat:
            kb = kl_ref[pl.ds(start, band), cols]
            vbt = vl_ref[pl.ds(start, band), cols].astype(F32).T.astype(BF16)
        out_t = None
        for half in range(2):
            sink = jnp.concatenate(
                [jnp.full((1, tq), sink_ref[(a * per + j) * 2 + half] * LOG2E, F32)
                 for j in range(per)], axis=1)
            st_c = _dot_nt(_half_select(kc, half, 1), qs)
            m = jnp.maximum(jnp.max(st_c, axis=0, keepdims=True), sink)
            if is_lat:
                st_b = jnp.where(ok, _dot_nt(_half_select(kb, half, 1), qs), NEG_INF)
                m = jnp.maximum(m, jnp.max(st_b, axis=0, keepdims=True))
            pt_c = jnp.exp2(st_c - m)
            l = jnp.sum(pt_c, axis=0, keepdims=True) + jnp.exp2(sink - m)
            o_t = _dot(_half_select(vct, half, 0), pt_c.astype(BF16))
            if is_lat:
                pt_b = jnp.exp2(st_b - m)
                l = l + jnp.sum(pt_b, axis=0, keepdims=True)
                o_t = o_t + _dot(_half_select(vbt, half, 0), pt_b.astype(BF16))
            o_t = o_t / l
            out_t = o_t if out_t is None else out_t + o_t
        out = out_t.T
        for j in range(per):
            o_ref[:, (a * per + j) * LANES:(a * per + j + 1) * LANES] = (
                out[j * tq:(j + 1) * tq].astype(o_ref.dtype))


def _swa_attention(a_q, a_lat, a_ctx, sink_perm, geo, tq, is_lat):
    b, n, ctx = geo
    width = SWA_HEADS * SWA_HD
    kvw = SWA_KV_HEADS * SWA_HD
    rows = n if is_lat else ctx
    nt = rows // tq
    kblk, vblk = 2 * width // kvw, 2 * width // kvw + 1
    in_specs = [pl.BlockSpec(memory_space=pltpu.SMEM),
                pl.BlockSpec((tq, width), lambda bi, t: (bi * nt + t, 1))]
    args = [sink_perm, a_q]
    if is_lat:
        in_specs += [pl.BlockSpec((n, kvw), lambda bi, t: (bi, kblk)),
                     pl.BlockSpec((n, kvw), lambda bi, t: (bi, vblk))]
        args += [a_lat, a_lat]
    in_specs += [pl.BlockSpec((ctx, kvw), lambda bi, t: (bi, kblk)),
                 pl.BlockSpec((ctx, kvw), lambda bi, t: (bi, vblk))]
    args += [a_ctx, a_ctx]
    return pl.pallas_call(
        functools.partial(_swa_kernel, is_lat=is_lat, tq=tq, band=tq + 2 * SWA_WINDOW),
        out_shape=jax.ShapeDtypeStruct((b * rows, width), BF16),
        grid=(b, nt),
        in_specs=in_specs,
        out_specs=pl.BlockSpec((tq, width), lambda bi, t: (bi * nt + t, 0)),
        compiler_params=_cparams(("parallel", "arbitrary")),
        name="swa_attention" if is_lat else "swa_ctx_attention",
    )(*args)


def _na_kernel(*refs, is_lat, rq, rk, rows):
    if is_lat:
        q_ref, kl_ref, vl_ref, kc_ref, vc_ref, bt_ref, o_ref, s_scr = refs
    else:
        q_ref, kc_ref, vc_ref, o_ref = refs
    t = pl.program_id(2)
    w = GRID_W
    band = rk * w
    q = q_ref[...]
    kc = kc_ref[...]
    vct = vc_ref[...].astype(F32).T.astype(BF16)
    if is_lat:
        r0 = t * rq
        srow = jnp.clip(r0 - NA_WIN_R // 2, 0, rows - rk)
        start = pl.multiple_of(srow * w, w)
        kb = kl_ref[pl.ds(start, band), :]
        vbt = vl_ref[pl.ds(start, band), :].astype(F32).T.astype(BF16)
        lo = lax.broadcasted_iota(jnp.int32, (w, LANES), 1) < w
    out_t = None
    for half in range(2):
        st_c = _dot_nt(_half_select(kc, half, 1), q)
        m = jnp.max(st_c, axis=0, keepdims=True)
        if is_lat:
            s_half = s_scr.at[half]
            s_half[...] = _dot_nt(_half_select(kb, half, 1), q)
            for kj in range(rk):
                kr = srow + kj
                for qp in range(rq // 2):
                    ra = r0 + 2 * qp
                    rsa = jnp.clip(ra - NA_WIN_R // 2, 0, rows - NA_WIN_R)
                    rsb = jnp.clip(ra + 1 - NA_WIN_R // 2, 0, rows - NA_WIN_R)
                    e = jnp.clip(kr - ra + NA_WIN_R - 1, 0, 2 * NA_WIN_R - 1)
                    pen0 = jnp.where((kr >= rsa) & (kr < rsa + NA_WIN_R), 0.0, NEG_INF)
                    pen1 = jnp.where((kr >= rsb) & (kr < rsb + NA_WIN_R), 0.0, NEG_INF)
                    blk = (slice(kj * w, (kj + 1) * w), slice(qp * LANES, (qp + 1) * LANES))
                    s_half[blk] = s_half[blk] + (bt_ref[half, e] * LOG2E + jnp.where(lo, pen0, pen1))
            st_b = s_half[...]
            m = jnp.maximum(m, jnp.max(st_b, axis=0, keepdims=True))
        pt_c = jnp.exp2(st_c - m)
        l = jnp.sum(pt_c, axis=0, keepdims=True)
        o_t = _dot(_half_select(vct, half, 0), pt_c.astype(BF16))
        if is_lat:
            pt_b = jnp.exp2(st_b - m)
            l = l + jnp.sum(pt_b, axis=0, keepdims=True)
            o_t = o_t + _dot(_half_select(vbt, half, 0), pt_b.astype(BF16))
        o_t = o_t / l
        out_t = o_t if out_t is None else out_t + o_t
    o_ref[...] = out_t.T.astype(o_ref.dtype)


def _na_attention(a_q, a_lat, a_ctx, bias_tab, geo, rq, is_lat):
    b, n, ctx = geo
    width = NA_HEADS * NA_HD
    pairs = width // LANES
    tq = rq * GRID_W
    rk = rq + NA_WIN_R
    grid_rows = n // GRID_W
    rows = n if is_lat else ctx
    assert grid_rows >= rk and rows % tq == 0
    nt = rows // tq
    qo, ko, vo = pairs, 2 * pairs, 3 * pairs
    in_specs = [pl.BlockSpec((tq, LANES), lambda bi, pi, t: (bi * nt + t, qo + pi))]
    args = [a_q]
    if is_lat:
        in_specs += [pl.BlockSpec((n, LANES), lambda bi, pi, t: (bi, ko + pi)),
                     pl.BlockSpec((n, LANES), lambda bi, pi, t: (bi, vo + pi))]
        args += [a_lat, a_lat]
    in_specs += [pl.BlockSpec((ctx, LANES), lambda bi, pi, t: (bi, ko + pi)),
                 pl.BlockSpec((ctx, LANES), lambda bi, pi, t: (bi, vo + pi))]
    args += [a_ctx, a_ctx]
    scratch = []
    if is_lat:
        in_specs.append(pl.BlockSpec((2, 2 * NA_WIN_R, GRID_W, LANES),
                                     lambda bi, pi, t: (pi, 0, 0, 0)))
        args.append(bias_tab)
        scratch.append(pltpu.VMEM((2, rk * GRID_W, tq), F32))
    return pl.pallas_call(
        functools.partial(_na_kernel, is_lat=is_lat, rq=rq, rk=rk, rows=grid_rows),
        out_shape=jax.ShapeDtypeStruct((b * rows, width), BF16),
        grid=(b, pairs, nt),
        in_specs=in_specs,
        out_specs=pl.BlockSpec((tq, LANES), lambda bi, pi, t: (bi * nt + t, pi)),
        scratch_shapes=scratch,
        compiler_params=_cparams(("parallel", "arbitrary", "arbitrary")),
        name="na_attention" if is_lat else "na_ctx_attention",
    )(*args)


def _rope_tables(n, d_rot):
    t = jnp.arange(n)
    row = (t // GRID_W).astype(F32)
    col = (t % GRID_W).astype(F32)
    d_axis = d_rot // 2
    inv = ROPE_BASE ** (-jnp.arange(0, d_axis, 2, dtype=F32) / d_axis)
    ang = jnp.concatenate([row[:, None] * inv, col[:, None] * inv], axis=-1)
    cos, sin = jnp.cos(ang), jnp.sin(ang)
    reps = LANES // d_rot
    return (jnp.tile(cos, (1, 2 * reps)),
            jnp.tile(jnp.concatenate([-sin, sin], axis=-1), (1, reps)))


def _na_bias_table(rpb):
    col = jnp.arange(GRID_W)
    dc = jnp.clip(col[:, None] - col[None, :], -(NA_WIN_C - 1), NA_WIN_C - 1) + (NA_WIN_C - 1)
    cs = jnp.clip(col - NA_WIN_C // 2, 0, GRID_W - NA_WIN_C)[None, :]
    col_ok = (col[:, None] >= cs) & (col[:, None] < cs + NA_WIN_C)
    bt = jnp.where(col_ok, rpb[:, :, dc], NEG_INF)
    bt = jnp.pad(bt, ((0, 0), (1, 1), (0, 0), (0, 0)))
    return jnp.concatenate([bt[:, 1:], bt[:, :-1]], axis=-1).astype(F32)


def _tile_plan(b, n, n_ctx, d):
    tm = _largest_divisor(math.gcd(n, b * n_ctx), (512, 256, 128))
    return dict(
        tm=tm,
        tn_d=_largest_divisor(d, (2048, 1024, 512, 256, 128)),
        tq_flash=_largest_divisor(n, (512, 256, 128)),
        tq_diff=_largest_divisor(n, (512, 256, 128)),
        tk=_largest_divisor(n + n_ctx, (768, 640, 512, 384, 256, 128)),
        tk_mla=_largest_divisor(n + n_ctx, (768, 640, 512, 384, 256, 128)),
        heads_mla=2,
        tq_swa=128,
        rq_na=4,
    )


def kernel(x, c, ctx, c_ctx, l0_mod_w, l0_mod_b, l0_norm, l0_w_in, l0_q_norm, l0_w_qb, l0_kv_norm, l0_w_kvb, l0_w_out, l1_mod_w, l1_mod_b, l1_norm, l1_w_in, l1_sink, l1_w_out, l2_mod_w, l2_mod_b, l2_norm, l2_w_in, l2_rpb, l2_w_out, l3_mod_w, l3_mod_b, l3_norm, l3_w_in, l3_lam_q1, l3_lam_k1, l3_lam_q2, l3_lam_k2, l3_subln, l3_w_out, final_norm):
    b, n, d = x.shape
    n_ctx = ctx.shape[1]
    geo = (b, n, n_ctx)
    assert b < 8 and n % GRID_W == 0
    tp = _tile_plan(b, n, n_ctx, d)
    tm, tn_d = tp["tm"], tp["tn_d"]
    lat, cx = ("batch", n // tm), ("const", b)

    x_lat = x.reshape(b * n, d)
    x_ctx = ctx.reshape(b * n_ctx, d)
    c_rows = jnp.concatenate([c, c_ctx[None, :], jnp.zeros((8 - b - 1, d), F32)], axis=0)
    rope64 = _rope_tables(n, 64) + (n // tm,)
    rope128 = _rope_tables(n, 128) + (n // tm,)

    def groups(*spans):
        out = []
        for cols, code, scale in spans:
            out += [(code, scale)] * (cols // LANES)
        return tuple(out)

    def both(fn):
        return fn(x_lat, lat, True), fn(x_ctx, cx, False)

    mod = _modulation(c_rows, l0_mod_w, l0_mod_b)
    r = MLA_RANK
    w_in = jnp.concatenate([l0_w_in[:, 2 * r + MLA_ROPE:], l0_w_in[:, :2 * r + MLA_ROPE],
                            jnp.zeros((d, LANES - MLA_ROPE), F32)], axis=1).astype(BF16)
    width = MLA_HEADS * MLA_V
    g0 = groups((width + 2 * r, PLAIN, 1.0), (LANES, ROPE64, 1.0))
    a_lat, a_ctx = both(lambda xs, grp, is_lat: _projection(
        xs, 0, d, l0_norm, mod, grp, w_in, rope64 if is_lat else None, g0, tm, 640, "mla_in"))
    qk = MLA_NOPE + MLA_ROPE
    w_q = l0_w_qb.reshape(r, MLA_HEADS, qk)
    w_q = jnp.pad(w_q, ((0, 0), (0, 0), (0, 256 - qk))).reshape(r, MLA_HEADS * 256).astype(BF16)
    q_scale = qk ** -0.5 * LOG2E
    gq = groups((LANES, PLAIN, q_scale), (LANES, ROPE64, q_scale)) * MLA_HEADS
    gkv = groups((MLA_HEADS * 256, PLAIN, 1.0))
    w_kv = l0_w_kvb.astype(BF16)
    q_lat, kv_lat = (
        _projection(a_lat, width // r, r, l0_q_norm, None, None, w_q, rope64, gq, tm, 1024, "mla_q"),
        _projection(a_lat, width // r + 1, r, l0_kv_norm, None, None, w_kv, None, gkv, tm, 1024,
                    "mla_kv"))
    q_ctx, kv_ctx = (
        _projection(a_ctx, width // r, r, l0_q_norm, None, None, w_q, None, gq, tm, 1024,
                    "mla_q_ctx"),
        _projection(a_ctx, width // r + 1, r, l0_kv_norm, None, None, w_kv, None, gkv, tm, 1024,
                    "mla_kv_ctx"))
    kr_blk = (width + 2 * r) // LANES
    o_lat = _mla_attention(q_lat, kv_lat, kv_ctx, a_lat, a_ctx, kr_blk, geo, tp["tq_flash"],
                           tp["tk_mla"], tp["heads_mla"])
    o_ctx = _mla_ctx_attention(q_ctx, kv_ctx, a_ctx, kr_blk, geo)
    w_out = l0_w_out.astype(BF16)
    x_lat = _out_projection(o_lat, a_lat, w_out, x_lat, mod, lat, tm, tn_d, "mla_out")
    x_ctx = _out_projection(o_ctx, a_ctx, w_out, x_ctx, mod, cx, tm, tn_d, "mla_out")

    mod = _modulation(c_rows, l1_mod_w, l1_mod_b)
    per = SWA_HEADS // SWA_KV_HEADS
    order = [g * per + j for a in range(SWA_KV_HEADS // 2) for j in range(per)
             for g in (2 * a, 2 * a + 1)]
    col_perm = jnp.asarray([hh * SWA_HD + e for hh in order for e in range(SWA_HD)])
    width = SWA_HEADS * SWA_HD
    kvw = SWA_KV_HEADS * SWA_HD
    w_q, w_k, w_v, w_g = (l1_w_in[:, :width], l1_w_in[:, width:width + kvw],
                          l1_w_in[:, width + kvw:width + 2 * kvw], l1_w_in[:, width + 2 * kvw:])
    w_in = jnp.concatenate([w_g[:, col_perm], w_q[:, col_perm], w_k, w_v], axis=1).astype(BF16)
    g1 = groups((width, PLAIN, 1.0), (width, ROPE64, SWA_HD ** -0.5 * LOG2E),
                (kvw, ROPE64, 1.0), (kvw, PLAIN, 1.0))
    a_lat, a_ctx = both(lambda xs, grp, is_lat: _projection(
        xs, 0, d, l1_norm, mod, grp, w_in, rope64 if is_lat else None, g1, tm, 2304, "swa_in"))
    sink_perm = l1_sink[jnp.asarray(order)]
    tq = tp["tq_swa"]
    o_lat = _swa_attention(a_lat, a_lat, a_ctx, sink_perm, geo, tq, True)
    o_ctx = _swa_attention(a_ctx, a_lat, a_ctx, sink_perm, geo, tq, False)
    w_out = l1_w_out[col_perm, :].astype(BF16)
    x_lat = _out_projection(o_lat, a_lat, w_out, x_lat, mod, lat, tm, tn_d, "swa_out")
    x_ctx = _out_projection(o_ctx, a_ctx, w_out, x_ctx, mod, cx, tm, tn_d, "swa_out")

    mod = _modulation(c_rows, l2_mod_w, l2_mod_b)
    width = NA_HEADS * NA_HD
    w_in = jnp.concatenate([l2_w_in[:, 3 * width:], l2_w_in[:, :3 * width]], axis=1).astype(BF16)
    g2 = groups((width, PLAIN, 1.0), (width, PLAIN, NA_HD ** -0.5 * LOG2E), (2 * width, PLAIN, 1.0))
    a_lat, a_ctx = both(lambda xs, grp, is_lat: _projection(
        xs, 0, d, l2_norm, mod, grp, w_in, None, g2, tm, 2048, "na_in"))
    bias_tab = _na_bias_table(l2_rpb)
    o_lat = _na_attention(a_lat, a_lat, a_ctx, bias_tab, geo, 2 * tp["rq_na"], True)
    o_ctx = _na_attention(a_ctx, a_lat, a_ctx, bias_tab, geo, tp["rq_na"], False)
    w_out = l2_w_out.astype(BF16)
    x_lat = _out_projection(o_lat, a_lat, w_out, x_lat, mod, lat, tm, tn_d, "na_out")
    x_ctx = _out_projection(o_ctx, a_ctx, w_out, x_ctx, mod, cx, tm, tn_d, "na_out")

    mod = _modulation(c_rows, l3_mod_w, l3_mod_b)
    width = DIFF_HEADS * 2 * DIFF_HD
    w_in = jnp.concatenate([l3_w_in[:, 3 * width:], l3_w_in[:, :3 * width]], axis=1).astype(BF16)
    g3 = groups((width, PLAIN, 1.0), (width, ROPE128, DIFF_HD ** -0.5 * LOG2E),
                (width, ROPE128, 1.0), (width, PLAIN, 1.0))
    a_lat = _projection(x_lat, 0, d, l3_norm, mod, lat, w_in, rope128, g3, tm, 2048, "diff_in")
    kv_ctx = _projection(x_ctx, 0, d, l3_norm, mod, cx, w_in[:, 2 * width:], None, g3[2 * width // LANES:],
                         tm, 1024, "diff_in")
    lam_init = 0.8 - 0.6 * math.exp(-0.3 * 3)
    lam_rows = jnp.concatenate([jnp.stack([l3_lam_q1, l3_lam_k1, l3_lam_q2, l3_lam_k2]),
                                jnp.zeros((4, DIFF_HD), F32)], axis=0)
    o_lat = _diff_attention(a_lat, kv_ctx, lam_rows, l3_subln, geo, tp["tq_diff"], tp["tk"],
                            lam_init)
    x_lat = _out_projection(o_lat, a_lat, l3_w_out.astype(BF16), x_lat, mod, lat, tm, tn_d, "diff_out")

    return _final_norm(x_lat, final_norm, tm).reshape(b, n, d)
```

```python
import functools
import math

import jax
import jax.numpy as jnp
from jax import lax
from jax.experimental import pallas as pl
from jax.experimental.pallas import tpu as pltpu

F32 = jnp.float32
BF16 = jnp.bfloat16

LANES = 128
LOG2E = 1.4426950408889634
NEG_INF = -1e30
EPS = 1e-6
ROPE_BASE = 10000.0
GRID_W = 64
VMEM_LIMIT = 56 * 1024 * 1024

MLA_HEADS, MLA_NOPE, MLA_ROPE, MLA_V, MLA_RANK = 16, 128, 64, 128, 512
SWA_HEADS, SWA_KV_HEADS, SWA_HD, SWA_WINDOW = 32, 4, 64, 128
NA_HEADS, NA_HD, NA_WIN_R, NA_WIN_C = 32, 64, 8, 16
DIFF_HEADS, DIFF_HD = 8, 128

PLAIN, ROPE64, ROPE128 = 0, 1, 2
MLA_FLASH_SLOTS, DIFF_FLASH_SLOTS = 2, 2


def _cparams(sem):
    return pltpu.CompilerParams(dimension_semantics=sem, vmem_limit_bytes=VMEM_LIMIT)


def _dot_nt(a, b):
    return lax.dot_general(a, b, (((1,), (1,)), ((), ())), preferred_element_type=F32)


def _dot(a, b):
    return jnp.dot(a, b, preferred_element_type=F32)


def _rep(x, n):
    return x if n == 1 else jnp.concatenate([x] * n, axis=1)


def _largest_divisor(total, candidates):
    return next(c for c in candidates if total % c == 0)


def _mod_kernel(c_ref, w_ref, b_ref, o_ref):
    c = c_ref[...]
    s = c * jax.nn.sigmoid(c)
    o_ref[...] = _dot(s, w_ref[...]) + b_ref[...]


def _modulation(c_rows, mod_w, mod_b):
    r, d = c_rows.shape
    n = mod_w.shape[1]
    tn = math.gcd(n, 512)
    out = pl.pallas_call(
        _mod_kernel,
        out_shape=jax.ShapeDtypeStruct((r, n), F32),
        grid=(n // tn,),
        in_specs=[pl.BlockSpec((r, d), lambda j: (0, 0)),
                  pl.BlockSpec((d, tn), lambda j: (0, j)),
                  pl.BlockSpec((1, tn), lambda j: (0, j))],
        out_specs=pl.BlockSpec((r, tn), lambda j: (0, j)),
        compiler_params=_cparams(("arbitrary",)),
        name="modulation",
    )(c_rows, mod_w, mod_b.reshape(1, n))
    return out.reshape(r, 1, n)


def _mod_row(group):
    kind, val = group
    return (lambda i: i // val) if kind == "batch" else (lambda i: val)


def _rope_group(a, cos, sin, code):
    if code == ROPE64:
        lane = lax.broadcasted_iota(jnp.int32, a.shape, 1)
        first = (lane % 64) < 32
        partner = jnp.where(first, pltpu.roll(a, 96, 1), pltpu.roll(a, 32, 1))
    else:
        partner = pltpu.roll(a, 64, 1)
    return a * cos + partner * sin


def _proj_kernel(*refs, has_mod, has_rope, tile_patterns, d_mod):
    it = iter(refs)
    x_ref, g_ref = next(it), next(it)
    mod_ref = next(it) if has_mod else None
    w_ref = next(it)
    cos_ref, sin_ref = (next(it), next(it)) if has_rope else (None, None)
    o_ref, h_scr = next(it), next(it)
    j = pl.program_id(1)

    @pl.when(j == 0)
    def _():
        x = x_ref[...].astype(F32)
        y = x * lax.rsqrt(jnp.mean(x * x, axis=-1, keepdims=True) + EPS) * g_ref[...]
        if has_mod:
            shift = mod_ref[0, :, 0:d_mod]
            scale = mod_ref[0, :, d_mod:2 * d_mod]
            y = y * (1.0 + scale) + shift
        h_scr[...] = y.astype(BF16)

    def epilogue(pattern):
        acc = _dot(h_scr[...], w_ref[...])
        for gi, (code, scale) in enumerate(pattern):
            a = acc[:, gi * LANES:(gi + 1) * LANES]
            if code != PLAIN:
                a = _rope_group(a, cos_ref[...], sin_ref[...], code)
            if scale != 1.0:
                a = a * scale
            o_ref[:, gi * LANES:(gi + 1) * LANES] = a.astype(o_ref.dtype)

    distinct = sorted(set(tile_patterns), key=tile_patterns.index)
    if len(distinct) == 1:
        epilogue(distinct[0])
    else:
        for pat in distinct:
            tiles = [jj for jj, p in enumerate(tile_patterns) if p == pat]
            cond = functools.reduce(jnp.logical_or, [j == jj for jj in tiles])
            pl.when(cond)(functools.partial(epilogue, pat))


def _projection(x, col_blk, kd, gain, mod, group, w, rope, groups, tm, tn, name):
    m = x.shape[0]
    n_out = w.shape[1]
    tm = min(tm, m)
    if group is not None and group[0] == "const":
        name += "_ctx"
    assert m % tm == 0 and n_out % tn == 0 and tn % LANES == 0 and len(groups) == n_out // LANES
    if rope is None:
        groups = tuple((PLAIN, scale) for _, scale in groups)
    has_mod = mod is not None
    has_rope = any(code != PLAIN for code, _ in groups)
    gpt = tn // LANES
    tile_patterns = tuple(tuple(groups[t * gpt:(t + 1) * gpt]) for t in range(n_out // tn))

    in_specs = [pl.BlockSpec((tm, kd), lambda i, j: (i, col_blk)),
                pl.BlockSpec((1, kd), lambda i, j: (0, 0))]
    args = [x, gain.reshape(1, kd).astype(F32)]
    d_mod = 0
    if has_mod:
        d_mod = mod.shape[2] // 3
        row = _mod_row(group)
        in_specs.append(pl.BlockSpec((1, 1, 3 * d_mod), lambda i, j: (row(i), 0, 0)))
        args.append(mod)
    in_specs.append(pl.BlockSpec((kd, tn), lambda i, j: (0, j)))
    args.append(w)
    if has_rope:
        cos, sin, tiles_per_batch = rope
        in_specs += [pl.BlockSpec((tm, LANES), lambda i, j: (i % tiles_per_batch, 0))] * 2
        args += [cos, sin]
    kern = functools.partial(_proj_kernel, has_mod=has_mod, has_rope=has_rope,
                             tile_patterns=tile_patterns, d_mod=d_mod)
    return pl.pallas_call(
        kern,
        out_shape=jax.ShapeDtypeStruct((m, n_out), BF16),
        grid=(m // tm, n_out // tn),
        in_specs=in_specs,
        out_specs=pl.BlockSpec((tm, tn), lambda i, j: (i, j)),
        scratch_shapes=[pltpu.VMEM((tm, kd), BF16)],
        compiler_params=_cparams(("parallel", "arbitrary")),
        name=name,
    )(*args)


def _out_kernel(o_ref, gate_ref, w_ref, x_ref, gm_ref, y_ref, u_scr):
    @pl.when(pl.program_id(1) == 0)
    def _():
        g = gate_ref[...].astype(F32)
        u_scr[...] = (o_ref[...].astype(F32) * (g * jax.nn.sigmoid(g))).astype(BF16)

    y_ref[...] = x_ref[...] + gm_ref[0] * _dot(u_scr[...], w_ref[...])


def _out_projection(o, proj, w_out, x, mod, group, tm, tn, name, in_place=True):
    m, d = x.shape
    width = w_out.shape[0]
    tm = min(tm, m)
    if group[0] == "const":
        name += "_ctx"
    row = _mod_row(group)
    gate_blk = 2 * d // tn
    return pl.pallas_call(
        _out_kernel,
        out_shape=jax.ShapeDtypeStruct((m, d), F32),
        grid=(m // tm, d // tn),
        in_specs=[pl.BlockSpec((tm, width), lambda i, j: (i, 0)),
                  pl.BlockSpec((tm, width), lambda i, j: (i, 0)),
                  pl.BlockSpec((width, tn), lambda i, j: (0, j)),
                  pl.BlockSpec((tm, tn), lambda i, j: (i, j)),
                  pl.BlockSpec((1, 1, tn), lambda i, j: (row(i), 0, gate_blk + j))],
        out_specs=pl.BlockSpec((tm, tn), lambda i, j: (i, j)),
        scratch_shapes=[pltpu.VMEM((tm, width), BF16)],
        input_output_aliases={3: 0} if in_place else {},
        compiler_params=_cparams(("parallel", "arbitrary")),
        name=name,
    )(o, proj, w_out, x, mod)


def _norm_kernel(x_ref, g_ref, o_ref):
    x = x_ref[...]
    o_ref[...] = x * lax.rsqrt(jnp.mean(x * x, axis=-1, keepdims=True) + EPS) * g_ref[...]


def _final_norm(x, gain, tm):
    m, d = x.shape
    return pl.pallas_call(
        _norm_kernel,
        out_shape=jax.ShapeDtypeStruct((m, d), F32),
        grid=(m // tm,),
        in_specs=[pl.BlockSpec((tm, d), lambda i: (i, 0)),
                  pl.BlockSpec((1, d), lambda i: (0, 0))],
        out_specs=pl.BlockSpec((tm, d), lambda i: (i, 0)),
        compiler_params=_cparams(("parallel",)),
        name="final_norm",
    )(x, gain.reshape(1, d))


def _flash(n_chunks, scores_t, values_t, stats, bufs):
    slots = bufs[0][0].shape[0]
    for m_scr, l_scr, acc in stats:
        m_scr[...] = jnp.full(m_scr.shape, NEG_INF, F32)
        l_scr[...] = jnp.zeros(l_scr.shape, F32)
        acc[...] = jnp.zeros(acc.shape, F32)

    def load_scores(c, slot):
        for s, (st, _, _) in zip(scores_t(c), bufs):
            st[slot] = s

    def softmax(slot):
        for (m_scr, l_scr, _), (st, pt, alpha_buf) in zip(stats, bufs):
            s = st[slot]
            m_prev = m_scr[...]
            m_new = jnp.maximum(m_prev, jnp.max(s, axis=0, keepdims=True))
            alpha = jnp.exp2(m_prev - m_new)
            p = jnp.exp2(s - m_new)
            l_scr[...] = alpha * l_scr[...] + jnp.sum(p, axis=0, keepdims=True)
            m_scr[...] = m_new
            alpha_buf[slot] = alpha
            pt[slot] = p.astype(pt.dtype)

    def weighted_values(c, slot):
        vts = values_t(c)
        if not isinstance(vts, tuple):
            vts = (vts,) * len(stats)
        for vt, (_, _, acc), (_, pt, alpha_buf) in zip(vts, stats, bufs):
            acc[...] = alpha_buf[slot] * acc[...] + _dot(vt, pt[slot])

    def step(c, k, first, last):
        if not last:
            load_scores(c + 1, (k + 1) % slots)
        if not first:
            weighted_values(c - 1, (k - 1) % slots)
        softmax(k)

    load_scores(0, 0)
    step(0, 0, True, n_chunks == 1)
    iters = max(n_chunks - 2, 0) // slots

    def body(i, carry):
        for k in range(slots):
            step(1 + slots * i + k, (1 + k) % slots, False, False)
        return carry
    lax.fori_loop(0, iters, body, 0)
    for c in range(1 + slots * iters, n_chunks):
        step(c, c % slots, False, c == n_chunks - 1)
    weighted_values(n_chunks - 1, (n_chunks - 1) % slots)


def _transpose_chunks(dst, src, tk):
    def body(c, carry):
        rows = pl.ds(pl.multiple_of(c * tk, tk), tk)
        dst[c] = src[rows, :].astype(F32).T.astype(dst.dtype)
        return carry
    lax.fori_loop(0, dst.shape[0], body, 0)


def _softmax_once(s, v):
    m = jnp.max(s, axis=-1, keepdims=True)
    p = jnp.exp2(s - m)
    return _dot(p.astype(BF16), v) / jnp.sum(p, axis=-1, keepdims=True)


def _gather_rows(dst, src_ctx, src_lat, dst_cols, src_cols, ctx, step):
    dst[0:ctx, dst_cols] = src_ctx[:, src_cols]

    def copy(c, carry):
        r = pl.multiple_of(c * step, step)
        dst[pl.ds(ctx + r, step), dst_cols] = src_lat[pl.ds(r, step), src_cols]
        return carry
    lax.fori_loop(0, src_lat.shape[0] // step, copy, 0)


def _mla_kernel(q_ref, kvl_ref, kvc_ref, krl_ref, krc_ref, o_ref,
                k_scr, v_scr, vt_scr, qt_scr, *stat_and_bufs, tk, ctx, heads):
    lo, hi = slice(0, LANES), slice(LANES, 2 * LANES)
    stats = [stat_and_bufs[3 * g:3 * g + 3] for g in range(heads)]
    bufs = [stat_and_bufs[3 * heads + 3 * g:3 * heads + 3 * g + 3] for g in range(heads)]

    @pl.when(pl.program_id(2) == 0)
    def _():
        for g in range(heads):
            kn = slice(2 * g * LANES, (2 * g + 1) * LANES)
            vv = slice((2 * g + 1) * LANES, (2 * g + 2) * LANES)
            _gather_rows(k_scr.at[g], kvc_ref, kvl_ref, lo, kn, ctx, ctx)
            _gather_rows(k_scr.at[g], krc_ref, krl_ref, hi, lo, ctx, ctx)
            _gather_rows(v_scr, kvc_ref, kvl_ref, lo, vv, ctx, ctx)
            _transpose_chunks(vt_scr.at[g], v_scr, tk)

    def chunk(c):
        return pl.ds(pl.multiple_of(c * tk, tk), tk)

    for g in range(heads):
        qt_scr[g] = q_ref[:, 2 * g * LANES:(2 * g + 2) * LANES].astype(F32).T.astype(BF16)
    _flash(vt_scr.shape[1],
           lambda c: tuple(_dot(k_scr[g, chunk(c), :], qt_scr[g]) for g in range(heads)),
           lambda c: tuple(vt_scr[g, c] for g in range(heads)),
           stats, bufs)
    for g, (_, l_scr, acc) in enumerate(stats):
        o_ref[:, g * MLA_V:(g + 1) * MLA_V] = (acc[...] / l_scr[...]).T.astype(o_ref.dtype)


def _flash_buffers(tq, tk, slots):
    return [pltpu.VMEM((slots, tk, tq), F32), pltpu.VMEM((slots, tk, tq), BF16),
            pltpu.VMEM((slots, 1, tq), F32)]


def _mla_attention(q_lat, kv_lat, kv_ctx, a_lat, a_ctx, kr_blk, geo, tq, tk, heads):
    b, n, ctx = geo
    h = MLA_HEADS
    nt = n // tq
    return pl.pallas_call(
        functools.partial(_mla_kernel, tk=tk, ctx=ctx, heads=heads),
        out_shape=jax.ShapeDtypeStruct((b * n, h * MLA_V), BF16),
        grid=(b, h // heads, nt),
        in_specs=[pl.BlockSpec((tq, heads * 256), lambda bi, hi, t: (bi * nt + t, hi)),
                  pl.BlockSpec((n, heads * 256), lambda bi, hi, t: (bi, hi)),
                  pl.BlockSpec((ctx, heads * 256), lambda bi, hi, t: (bi, hi)),
                  pl.BlockSpec((n, LANES), lambda bi, hi, t: (bi, kr_blk)),
                  pl.BlockSpec((ctx, LANES), lambda bi, hi, t: (bi, kr_blk))],
        out_specs=pl.BlockSpec((tq, heads * MLA_V), lambda bi, hi, t: (bi * nt + t, hi)),
        scratch_shapes=[pltpu.VMEM((heads, ctx + n, 256), BF16),
                        pltpu.VMEM((ctx + n, MLA_V), BF16),
                        pltpu.VMEM((heads, (ctx + n) // tk, MLA_V, tk), BF16),
                        pltpu.VMEM((heads, 256, tq), BF16)]
        + [pltpu.VMEM((1, tq), F32), pltpu.VMEM((1, tq), F32),
           pltpu.VMEM((MLA_V, tq), F32)] * heads
        + _flash_buffers(tq, tk, MLA_FLASH_SLOTS) * heads,
        compiler_params=_cparams(("parallel", "arbitrary", "arbitrary")),
        name="mla_attention",
    )(q_lat, kv_lat, kv_ctx, a_lat, a_ctx)


def _mla_ctx_kernel(q_ref, kv_ref, kr_ref, o_ref):
    q = q_ref[...]
    s = _dot_nt(q[:, 0:LANES], kv_ref[:, 0:LANES]) + _dot_nt(q[:, LANES:2 * LANES], kr_ref[...])
    o_ref[...] = _softmax_once(s, kv_ref[:, LANES:2 * LANES]).astype(o_ref.dtype)


def _mla_ctx_attention(q_ctx, kv_ctx, a_ctx, kr_blk, geo):
    b, n, ctx = geo
    h = MLA_HEADS
    return pl.pallas_call(
        _mla_ctx_kernel,
        out_shape=jax.ShapeDtypeStruct((b * ctx, h * MLA_V), BF16),
        grid=(b, h),
        in_specs=[pl.BlockSpec((ctx, 256), lambda bi, hi: (bi, hi)),
                  pl.BlockSpec((ctx, 256), lambda bi, hi: (bi, hi)),
                  pl.BlockSpec((ctx, LANES), lambda bi, hi: (bi, kr_blk))],
        out_specs=pl.BlockSpec((ctx, MLA_V), lambda bi, hi: (bi, hi)),
        compiler_params=_cparams(("parallel", "arbitrary")),
        name="mla_ctx_attention",
    )(q_ctx, kv_ctx, a_ctx)


def _diff_kernel(q_ref, kl_ref, vl_ref, kc_ref, vc_ref, lam_ref, sub_ref, o_ref,
                 k_scr, v_scr, vt_scr, qt_scr, m0, l0, a0, m1, l1, a1,
                 st0, pt0, alpha0, st1, pt1, alpha1, *, tk, ctx, lam_init):
    hd = DIFF_HD
    full = slice(0, 2 * hd)

    @pl.when(pl.program_id(2) == 0)
    def _():
        _gather_rows(k_scr, kc_ref, kl_ref, full, full, ctx, ctx)
        _gather_rows(v_scr, vc_ref, vl_ref, full, full, ctx, ctx)
        _transpose_chunks(vt_scr, v_scr, tk)

    def chunk(c):
        return pl.ds(pl.multiple_of(c * tk, tk), tk)

    qt_scr[...] = q_ref[...].astype(F32).T.astype(BF16)

    def scores_t(c):
        return tuple(_dot(k_scr[chunk(c), i * hd:(i + 1) * hd], qt_scr[i * hd:(i + 1) * hd, :])
                     for i in range(2))

    _flash(vt_scr.shape[0], scores_t, lambda c: vt_scr[c],
           [(m0, l0, a0), (m1, l1, a1)], [(st0, pt0, alpha0), (st1, pt1, alpha1)])

    lv = lam_ref[...]
    lam = (jnp.exp(jnp.sum(lv[0:1] * lv[1:2], axis=-1, keepdims=True))
           - jnp.exp(jnp.sum(lv[2:3] * lv[3:4], axis=-1, keepdims=True)) + lam_init)
    o = (a0[...] / l0[...] - lam * (a1[...] / l1[...])).T
    o = o * lax.rsqrt(jnp.mean(o * o, axis=-1, keepdims=True) + EPS) * sub_ref[...]
    o_ref[...] = (o * (1.0 - lam_init)).astype(o_ref.dtype)


def _diff_attention(a_lat, kv_ctx, lam_rows, subln, geo, tq, tk, lam_init):
    b, n, ctx = geo
    h = DIFF_HEADS
    nt = n // tq
    return pl.pallas_call(
        functools.partial(_diff_kernel, tk=tk, ctx=ctx, lam_init=lam_init),
        out_shape=jax.ShapeDtypeStruct((b * n, h * 256), BF16),
        grid=(b, h, nt),
        in_specs=[pl.BlockSpec((tq, 256), lambda bi, hi, t: (bi * nt + t, h + hi)),
                  pl.BlockSpec((n, 256), lambda bi, hi, t: (bi, 2 * h + hi)),
                  pl.BlockSpec((n, 256), lambda bi, hi, t: (bi, 3 * h + hi)),
                  pl.BlockSpec((ctx, 256), lambda bi, hi, t: (bi, hi)),
                  pl.BlockSpec((ctx, 256), lambda bi, hi, t: (bi, h + hi)),
                  pl.BlockSpec((8, LANES), lambda bi, hi, t: (0, 0)),
                  pl.BlockSpec((1, 256), lambda bi, hi, t: (0, 0))],
        out_specs=pl.BlockSpec((tq, 256), lambda bi, hi, t: (bi * nt + t, hi)),
        scratch_shapes=[pltpu.VMEM((ctx + n, 256), BF16), pltpu.VMEM((ctx + n, 256), BF16),
                        pltpu.VMEM(((ctx + n) // tk, 256, tk), BF16),
                        pltpu.VMEM((256, tq), BF16)]
        + [pltpu.VMEM((1, tq), F32), pltpu.VMEM((1, tq), F32), pltpu.VMEM((256, tq), F32)] * 2
        + _flash_buffers(tq, tk, DIFF_FLASH_SLOTS) * 2,
        compiler_params=_cparams(("parallel", "arbitrary", "arbitrary")),
        name="diff_attention",
    )(a_lat, a_lat, a_lat, kv_ctx, kv_ctx, lam_rows, subln.reshape(1, 256).astype(F32))


def _half_select(x, half, axis=1):
    idx = lax.broadcasted_iota(jnp.int32, x.shape, axis)
    keep = (idx < 64) if half == 0 else (idx >= 64)
    return jnp.where(keep, x, jnp.zeros_like(x))


def _swa_kernel(*refs, is_lat, tq, band):
    if is_lat:
        sink_ref, q_ref, kl_ref, vl_ref, kc_ref, vc_ref, o_ref = refs
    else:
        sink_ref, q_ref, kc_ref, vc_ref, o_ref = refs
    t = pl.program_id(1)
    per = SWA_HEADS // SWA_KV_HEADS
    if is_lat:
        n = kl_ref.shape[0]
        start = pl.multiple_of(jnp.clip(t * tq - SWA_WINDOW, 0, n - band), SWA_WINDOW)
        kpos = start + lax.broadcasted_iota(jnp.int32, (band, per * tq), 0)
        qpos = t * tq + lax.broadcasted_iota(jnp.int32, (band, per * tq), 1) % tq
        ok = jnp.abs(qpos - kpos) <= SWA_WINDOW
    for a in range(SWA_KV_HEADS // 2):
        cols = slice(a * LANES, (a + 1) * LANES)
        qs = jnp.concatenate([q_ref[:, (a * per + j) * LANES:(a * per + j + 1) * LANES]
                              for j in range(per)], axis=0)
        kc = kc_ref[:, cols]
        vct = vc_ref[:, cols].astype(F32).T.astype(BF16)
        if is_lat:
            kb = kl_ref[pl.ds(start, band), cols]
            vbt = vl_ref[pl.ds(start, band), cols].astype(F32).T.astype(BF16)
        out_t = None
        for half in range(2):
            sink = jnp.concatenate(
                [jnp.full((1, tq), sink_ref[(a * per + j) * 2 + half] * LOG2E, F32)
                 for j in range(per)], axis=1)
            st_c = _dot_nt(_half_select(kc, half, 1), qs)
            m = jnp.maximum(jnp.max(st_c, axis=0, keepdims=True), sink)
            if is_lat:
                st_b = jnp.where(ok, _dot_nt(_half_select(kb, half, 1), qs), NEG_INF)
                m = jnp.maximum(m, jnp.max(st_b, axis=0, keepdims=True))
            pt_c = jnp.exp2(st_c - m)
            l = jnp.sum(pt_c, axis=0, keepdims=True) + jnp.exp2(sink - m)
            o_t = _dot(_half_select(vct, half, 0), pt_c.astype(BF16))
            if is_lat:
                pt_b = jnp.exp2(st_b - m)
                l = l + jnp.sum(pt_b, axis=0, keepdims=True)
                o_t = o_t + _dot(_half_select(vbt, half, 0), pt_b.astype(BF16))
            o_t = o_t / l
            out_t = o_t if out_t is None else out_t + o_t
        out = out_t.T
        for j in range(per):
            o_ref[:, (a * per + j) * LANES:(a * per + j + 1) * LANES] = (
                out[j * tq:(j + 1) * tq].astype(o_ref.dtype))


def _swa_attention(a_q, a_lat, a_ctx, sink_perm, geo, tq, is_lat):
    b, n, ctx = geo
    width = SWA_HEADS * SWA_HD
    kvw = SWA_KV_HEADS * SWA_HD
    rows = n if is_lat else ctx
    nt = rows // tq
    kblk, vblk = 2 * width // kvw, 2 * width // kvw + 1
    in_specs = [pl.BlockSpec(memory_space=pltpu.SMEM),
                pl.BlockSpec((tq, width), lambda bi, t: (bi * nt + t, 1))]
    args = [sink_perm, a_q]
    if is_lat:
        in_specs += [pl.BlockSpec((n, kvw), lambda bi, t: (bi, kblk)),
                     pl.BlockSpec((n, kvw), lambda bi, t: (bi, vblk))]
        args += [a_lat, a_lat]
    in_specs += [pl.BlockSpec((ctx, kvw), lambda bi, t: (bi, kblk)),
                 pl.BlockSpec((ctx, kvw), lambda bi, t: (bi, vblk))]
    args += [a_ctx, a_ctx]
    return pl.pallas_call(
        functools.partial(_swa_kernel, is_lat=is_lat, tq=tq, band=tq + 2 * SWA_WINDOW),
        out_shape=jax.ShapeDtypeStruct((b * rows, width), BF16),
        grid=(b, nt),
        in_specs=in_specs,
        out_specs=pl.BlockSpec((tq, width), lambda bi, t: (bi * nt + t, 0)),
        compiler_params=_cparams(("parallel", "arbitrary")),
        name="swa_attention" if is_lat else "swa_ctx_attention",
    )(*args)


def _na_kernel(*refs, is_lat, rq, rk, rows):
    if is_lat:
        q_ref, kl_ref, vl_ref, kc_ref, vc_ref, bt_ref, o_ref, s_scr = refs
    else:
        q_ref, kc_ref, vc_ref, o_ref = refs
    t = pl.program_id(2)
    w = GRID_W
    band = rk * w
    q = q_ref[...]
    kc = kc_ref[...]
    vct = vc_ref[...].astype(F32).T.astype(BF16)
    if is_lat:
        r0 = t * rq
        srow = jnp.clip(r0 - NA_WIN_R // 2, 0, rows - rk)
        start = pl.multiple_of(srow * w, w)
        kb = kl_ref[pl.ds(start, band), :]
        vbt = vl_ref[pl.ds(start, band), :].astype(F32).T.astype(BF16)
        lo = lax.broadcasted_iota(jnp.int32, (w, LANES), 1) < w
    out_t = None
    for half in range(2):
        st_c = _dot_nt(_half_select(kc, half, 1), q)
        m = jnp.max(st_c, axis=0, keepdims=True)
        if is_lat:
            s_half = s_scr.at[half]
            s_half[...] = _dot_nt(_half_select(kb, half, 1), q)
            for kj in range(rk):
                kr = srow + kj
                for qp in range(rq // 2):
                    ra = r0 + 2 * qp
                    rsa = jnp.clip(ra - NA_WIN_R // 2, 0, rows - NA_WIN_R)
                    rsb = jnp.clip(ra + 1 - NA_WIN_R // 2, 0, rows - NA_WIN_R)
                    e = jnp.clip(kr - ra + NA_WIN_R - 1, 0, 2 * NA_WIN_R - 1)
                    pen0 = jnp.where((kr >= rsa) & (kr < rsa + NA_WIN_R), 0.0, NEG_INF)
                    pen1 = jnp.where((kr >= rsb) & (kr < rsb + NA_WIN_R), 0.0, NEG_INF)
                    blk = (slice(kj * w, (kj + 1) * w), slice(qp * LANES, (qp + 1) * LANES))
                    s_half[blk] = s_half[blk] + (bt_ref[half, e] * LOG2E + jnp.where(lo, pen0, pen1))
            st_b = s_half[...]
            m = jnp.maximum(m, jnp.max(st_b, axis=0, keepdims=True))
        pt_c = jnp.exp2(st_c - m)
        l = jnp.sum(pt_c, axis=0, keepdims=True)
        o_t = _dot(_half_select(vct, half, 0), pt_c.astype(BF16))
        if is_lat:
            pt_b = jnp.exp2(st_b - m)
            l = l + jnp.sum(pt_b, axis=0, keepdims=True)
            o_t = o_t + _dot(_half_select(vbt, half, 0), pt_b.astype(BF16))
        o_t = o_t / l
        out_t = o_t if out_t is None else out_t + o_t
    o_ref[...] = out_t.T.astype(o_ref.dtype)


def _na_attention(a_q, a_lat, a_ctx, bias_tab, geo, rq, is_lat):
    b, n, ctx = geo
    width = NA_HEADS * NA_HD
    pairs = width // LANES
    tq = rq * GRID_W
    rk = rq + NA_WIN_R
    grid_rows = n // GRID_W
    rows = n if is_lat else ctx
    assert grid_rows >= rk and rows % tq == 0
    nt = rows // tq
    qo, ko, vo = pairs, 2 * pairs, 3 * pairs
    in_specs = [pl.BlockSpec((tq, LANES), lambda bi, pi, t: (bi * nt + t, qo + pi))]
    args = [a_q]
    if is_lat:
        in_specs += [pl.BlockSpec((n, LANES), lambda bi, pi, t: (bi, ko + pi)),
                     pl.BlockSpec((n, LANES), lambda bi, pi, t: (bi, vo + pi))]
        args += [a_lat, a_lat]
    in_specs += [pl.BlockSpec((ctx, LANES), lambda bi, pi, t: (bi, ko + pi)),
                 pl.BlockSpec((ctx, LANES), lambda bi, pi, t: (bi, vo + pi))]
    args += [a_ctx, a_ctx]
    scratch = []
    if is_lat:
        in_specs.append(pl.BlockSpec((2, 2 * NA_WIN_R, GRID_W, LANES),
                                     lambda bi, pi, t: (pi, 0, 0, 0)))
        args.append(bias_tab)
        scratch.append(pltpu.VMEM((2, rk * GRID_W, tq), F32))
    return pl.pallas_call(
        functools.partial(_na_kernel, is_lat=is_lat, rq=rq, rk=rk, rows=grid_rows),
        out_shape=jax.ShapeDtypeStruct((b * rows, width), BF16),
        grid=(b, pairs, nt),
        in_specs=in_specs,
        out_specs=pl.BlockSpec((tq, LANES), lambda bi, pi, t: (bi * nt + t, pi)),
        scratch_shapes=scratch,
        compiler_params=_cparams(("parallel", "arbitrary", "arbitrary")),
        name="na_attention" if is_lat else "na_ctx_attention",
    )(*args)


def _rope_tables(n, d_rot):
    t = jnp.arange(n)
    row = (t // GRID_W).astype(F32)
    col = (t % GRID_W).astype(F32)
    d_axis = d_rot // 2
    inv = ROPE_BASE ** (-jnp.arange(0, d_axis, 2, dtype=F32) / d_axis)
    ang = jnp.concatenate([row[:, None] * inv, col[:, None] * inv], axis=-1)
    cos, sin = jnp.cos(ang), jnp.sin(ang)
    reps = LANES // d_rot
    return (jnp.tile(cos, (1, 2 * reps)),
            jnp.tile(jnp.concatenate([-sin, sin], axis=-1), (1, reps)))


def _na_bias_table(rpb):
    col = jnp.arange(GRID_W)
    dc = jnp.clip(col[:, None] - col[None, :], -(NA_WIN_C - 1), NA_WIN_C - 1) + (NA_WIN_C - 1)
    cs = jnp.clip(col - NA_WIN_C // 2, 0, GRID_W - NA_WIN_C)[None, :]
    col_ok = (col[:, None] >= cs) & (col[:, None] < cs + NA_WIN_C)
    bt = jnp.where(col_ok, rpb[:, :, dc], NEG_INF)
    bt = jnp.pad(bt, ((0, 0), (1, 1), (0, 0), (0, 0)))
    return jnp.concatenate([bt[:, 1:], bt[:, :-1]], axis=-1).astype(F32)


def _tile_plan(b, n, n_ctx, d):
    tm = _largest_divisor(math.gcd(n, b * n_ctx), (512, 256, 128))
    return dict(
        tm=tm,
        tn_d=_largest_divisor(d, (2048, 1024, 512, 256, 128)),
        tq_flash=_largest_divisor(n, (512, 256, 128)),
        tq_diff=_largest_divisor(n, (512, 256, 128)),
        tk=_largest_divisor(n + n_ctx, (768, 640, 512, 384, 256, 128)),
        tk_mla=_largest_divisor(n + n_ctx, (768, 640, 512, 384, 256, 128)),
        heads_mla=2,
        tq_swa=128,
        rq_na=4,
    )


def kernel(x, c, ctx, c_ctx, l0_mod_w, l0_mod_b, l0_norm, l0_w_in, l0_q_norm, l0_w_qb, l0_kv_norm, l0_w_kvb, l0_w_out, l1_mod_w, l1_mod_b, l1_norm, l1_w_in, l1_sink, l1_w_out, l2_mod_w, l2_mod_b, l2_norm, l2_w_in, l2_rpb, l2_w_out, l3_mod_w, l3_mod_b, l3_norm, l3_w_in, l3_lam_q1, l3_lam_k1, l3_lam_q2, l3_lam_k2, l3_subln, l3_w_out, final_norm):
    b, n, d = x.shape
    n_ctx = ctx.shape[1]
    geo = (b, n, n_ctx)
    assert b < 8 and n % GRID_W == 0
    tp = _tile_plan(b, n, n_ctx, d)
    tm, tn_d = tp["tm"], tp["tn_d"]
    lat, cx = ("batch", n // tm), ("const", b)

    x_lat = x.reshape(b * n, d)
    x_ctx = ctx.reshape(b * n_ctx, d)
    c_rows = jnp.concatenate([c, c_ctx[None, :], jnp.zeros((8 - b - 1, d), F32)], axis=0)
    rope64 = _rope_tables(n, 64) + (n // tm,)
    rope128 = _rope_tables(n, 128) + (n // tm,)

    def groups(*spans):
        out = []
        for cols, code, scale in spans:
            out += [(code, scale)] * (cols // LANES)
        return tuple(out)

    def both(fn):
        return fn(x_lat, lat, True), fn(x_ctx, cx, False)

    mod = _modulation(c_rows, l0_mod_w, l0_mod_b)
    r = MLA_RANK
    w_in = jnp.concatenate([l0_w_in[:, 2 * r + MLA_ROPE:], l0_w_in[:, :2 * r + MLA_ROPE],
                            jnp.zeros((d, 2 * LANES - MLA_ROPE), F32)], axis=1).astype(BF16)
    width = MLA_HEADS * MLA_V
    g0 = groups((width + 2 * r, PLAIN, 1.0), (LANES, ROPE64, 1.0), (LANES, PLAIN, 1.0))
    a_lat, a_ctx = both(lambda xs, grp, is_lat: _projection(
        xs, 0, d, l0_norm, mod, grp, w_in, rope64 if is_lat else None, g0, tm, 13 * LANES,
        "mla_in"))
    qk = MLA_NOPE + MLA_ROPE
    w_q = l0_w_qb.reshape(r, MLA_HEADS, qk)
    w_q = jnp.pad(w_q, ((0, 0), (0, 0), (0, 256 - qk))).reshape(r, MLA_HEADS * 256).astype(BF16)
    q_scale = qk ** -0.5 * LOG2E
    gq = groups((LANES, PLAIN, q_scale), (LANES, ROPE64, q_scale)) * MLA_HEADS
    gkv = groups((MLA_HEADS * 256, PLAIN, 1.0))
    w_kv = l0_w_kvb.astype(BF16)
    q_lat, kv_lat = (
        _projection(a_lat, width // r, r, l0_q_norm, None, None, w_q, rope64, gq, tm, 4096, "mla_q"),
        _projection(a_lat, width // r + 1, r, l0_kv_norm, None, None, w_kv, None, gkv, tm, 4096,
                    "mla_kv"))
    q_ctx, kv_ctx = (
        _projection(a_ctx, width // r, r, l0_q_norm, None, None, w_q, None, gq, tm, 4096,
                    "mla_q_ctx"),
        _projection(a_ctx, width // r + 1, r, l0_kv_norm, None, None, w_kv, None, gkv, tm, 4096,
                    "mla_kv_ctx"))
    kr_blk = (width + 2 * r) // LANES
    o_lat = _mla_attention(q_lat, kv_lat, kv_ctx, a_lat, a_ctx, kr_blk, geo, tp["tq_flash"],
                           tp["tk_mla"], tp["heads_mla"])
    o_ctx = _mla_ctx_attention(q_ctx, kv_ctx, a_ctx, kr_blk, geo)
    w_out = l0_w_out.astype(BF16)
    x_lat = _out_projection(o_lat, a_lat, w_out, x_lat, mod, lat, tm, tn_d, "mla_out", False)
    x_ctx = _out_projection(o_ctx, a_ctx, w_out, x_ctx, mod, cx, tm, tn_d, "mla_out", False)

    mod = _modulation(c_rows, l1_mod_w, l1_mod_b)
    per = SWA_HEADS // SWA_KV_HEADS
    pairs_kv = SWA_KV_HEADS // 2

    def pair_heads(w, axis):
        shape = w.shape
        w = w.reshape(shape[:axis] + (pairs_kv, 2, per, SWA_HD) + shape[axis + 1:])
        return jnp.swapaxes(w, axis + 1, axis + 2).reshape(shape)

    width = SWA_HEADS * SWA_HD
    kvw = SWA_KV_HEADS * SWA_HD
    w_q, w_k, w_v, w_g = (l1_w_in[:, :width], l1_w_in[:, width:width + kvw],
                          l1_w_in[:, width + kvw:width + 2 * kvw], l1_w_in[:, width + 2 * kvw:])
    w_in = jnp.concatenate([pair_heads(w_g, 1), pair_heads(w_q, 1), w_k, w_v],
                           axis=1).astype(BF16)
    g1 = groups((width, PLAIN, 1.0), (width, ROPE64, SWA_HD ** -0.5 * LOG2E),
                (kvw, ROPE64, 1.0), (kvw, PLAIN, 1.0))
    a_lat, a_ctx = both(lambda xs, grp, is_lat: _projection(
        xs, 0, d, l1_norm, mod, grp, w_in, rope64 if is_lat else None, g1, tm, 2304, "swa_in"))
    sink_perm = jnp.swapaxes(l1_sink.reshape(pairs_kv, 2, per), 1, 2).reshape(SWA_HEADS)
    tq = tp["tq_swa"]
    o_lat = _swa_attention(a_lat, a_lat, a_ctx, sink_perm, geo, tq, True)
    o_ctx = _swa_attention(a_ctx, a_lat, a_ctx, sink_perm, geo, tq, False)
    w_out = pair_heads(l1_w_out, 0).astype(BF16)
    x_lat = _out_projection(o_lat, a_lat, w_out, x_lat, mod, lat, tm, tn_d, "swa_out")
    x_ctx = _out_projection(o_ctx, a_ctx, w_out, x_ctx, mod, cx, tm, tn_d, "swa_out")

    mod = _modulation(c_rows, l2_mod_w, l2_mod_b)
    width = NA_HEADS * NA_HD
    w_in = jnp.concatenate([l2_w_in[:, 3 * width:], l2_w_in[:, :3 * width]], axis=1).astype(BF16)
    g2 = groups((width, PLAIN, 1.0), (width, PLAIN, NA_HD ** -0.5 * LOG2E), (2 * width, PLAIN, 1.0))
    a_lat, a_ctx = both(lambda xs, grp, is_lat: _projection(
        xs, 0, d, l2_norm, mod, grp, w_in, None, g2, tm, 2048, "na_in"))
    bias_tab = _na_bias_table(l2_rpb)
    o_lat = _na_attention(a_lat, a_lat, a_ctx, bias_tab, geo, 2 * tp["rq_na"], True)
    o_ctx = _na_attention(a_ctx, a_lat, a_ctx, bias_tab, geo, tp["rq_na"], False)
    w_out = l2_w_out.astype(BF16)
    x_lat = _out_projection(o_lat, a_lat, w_out, x_lat, mod, lat, tm, tn_d, "na_out")
    x_ctx = _out_projection(o_ctx, a_ctx, w_out, x_ctx, mod, cx, tm, tn_d, "na_out")

    mod = _modulation(c_rows, l3_mod_w, l3_mod_b)
    width = DIFF_HEADS * 2 * DIFF_HD
    w_in = jnp.concatenate([l3_w_in[:, 3 * width:], l3_w_in[:, :3 * width]], axis=1).astype(BF16)
    g3 = groups((width, PLAIN, 1.0), (width, ROPE128, DIFF_HD ** -0.5 * LOG2E),
                (width, ROPE128, 1.0), (width, PLAIN, 1.0))
    a_lat = _projection(x_lat, 0, d, l3_norm, mod, lat, w_in, rope128, g3, tm, 2048, "diff_in")
    kv_ctx = _projection(x_ctx, 0, d, l3_norm, mod, cx, w_in[:, 2 * width:], None, g3[2 * width // LANES:],
                         tm, 1024, "diff_in")
    lam_init = 0.8 - 0.6 * math.exp(-0.3 * 3)
    lam_rows = jnp.concatenate([jnp.stack([l3_lam_q1, l3_lam_k1, l3_lam_q2, l3_lam_k2]),
                                jnp.zeros((4, DIFF_HD), F32)], axis=0)
    o_lat = _diff_attention(a_lat, kv_ctx, lam_rows, l3_subln, geo, tp["tq_diff"], tp["tk"],
                            lam_init)
    x_lat = _out_projection(o_lat, a_lat, l3_w_out.astype(BF16), x_lat, mod, lat, tm, tn_d, "diff_out")

    return _final_norm(x_lat, final_norm, tm).reshape(b, n, d)
```

```python
import functools
import math

import jax
import jax.numpy as jnp
from jax import lax
from jax.experimental import pallas as pl
from jax.experimental.pallas import tpu as pltpu

F32 = jnp.float32
BF16 = jnp.bfloat16

LANES = 128
LOG2E = 1.4426950408889634
NEG_INF = -1e30
EPS = 1e-6
ROPE_BASE = 10000.0
GRID_W = 64
VMEM_LIMIT = 56 * 1024 * 1024

MLA_HEADS, MLA_NOPE, MLA_ROPE, MLA_V, MLA_RANK = 16, 128, 64, 128, 512
SWA_HEADS, SWA_KV_HEADS, SWA_HD, SWA_WINDOW = 32, 4, 64, 128
NA_HEADS, NA_HD, NA_WIN_R, NA_WIN_C = 32, 64, 8, 16
DIFF_HEADS, DIFF_HD = 8, 128

PLAIN, ROPE64, ROPE128 = 0, 1, 2
MLA_FLASH_SLOTS, DIFF_FLASH_SLOTS = 2, 2


def _cparams(sem):
    return pltpu.CompilerParams(dimension_semantics=sem, vmem_limit_bytes=VMEM_LIMIT)


def _dot_nt(a, b):
    return lax.dot_general(a, b, (((1,), (1,)), ((), ())), preferred_element_type=F32)


def _dot(a, b):
    return jnp.dot(a, b, preferred_element_type=F32)


def _rep(x, n):
    return x if n == 1 else jnp.concatenate([x] * n, axis=1)


def _largest_divisor(total, candidates):
    return next(c for c in candidates if total % c == 0)


def _mod_kernel(c_ref, w_ref, b_ref, o_ref):
    c = c_ref[...]
    s = c * jax.nn.sigmoid(c)
    o_ref[...] = _dot(s, w_ref[...]) + b_ref[...]


def _modulation(c_rows, mod_w, mod_b):
    r, d = c_rows.shape
    n = mod_w.shape[1]
    tn = math.gcd(n, 512)
    out = pl.pallas_call(
        _mod_kernel,
        out_shape=jax.ShapeDtypeStruct((r, n), F32),
        grid=(n // tn,),
        in_specs=[pl.BlockSpec((r, d), lambda j: (0, 0)),
                  pl.BlockSpec((d, tn), lambda j: (0, j)),
                  pl.BlockSpec((1, tn), lambda j: (0, j))],
        out_specs=pl.BlockSpec((r, tn), lambda j: (0, j)),
        compiler_params=_cparams(("arbitrary",)),
        name="modulation",
    )(c_rows, mod_w, mod_b.reshape(1, n))
    return out.reshape(r, 1, n)


def _mod_row(group):
    kind, val = group
    return (lambda i: i // val) if kind == "batch" else (lambda i: val)


def _rope_group(a, cos, sin, code):
    if code == ROPE64:
        lane = lax.broadcasted_iota(jnp.int32, a.shape, 1)
        first = (lane % 64) < 32
        partner = jnp.where(first, pltpu.roll(a, 96, 1), pltpu.roll(a, 32, 1))
    else:
        partner = pltpu.roll(a, 64, 1)
    return a * cos + partner * sin


def _proj_kernel(*refs, has_norm, has_mod, has_rope, tile_patterns, d_mod):
    it = iter(refs)
    x_ref = next(it)
    g_ref = next(it) if has_norm else None
    mod_ref = next(it) if has_mod else None
    w_ref = next(it)
    cos_ref, sin_ref = (next(it), next(it)) if has_rope else (None, None)
    o_ref = next(it)
    h_ref = next(it) if has_norm else x_ref
    j = pl.program_id(1)

    if has_norm:
        @pl.when(j == 0)
        def _():
            x = x_ref[...].astype(F32)
            y = x * lax.rsqrt(jnp.mean(x * x, axis=-1, keepdims=True) + EPS) * g_ref[...]
            if has_mod:
                shift = mod_ref[0, :, 0:d_mod]
                scale = mod_ref[0, :, d_mod:2 * d_mod]
                y = y * (1.0 + scale) + shift
            h_ref[...] = y.astype(BF16)

    def epilogue(pattern):
        acc = _dot(h_ref[...], w_ref[...])
        for gi, (code, scale) in enumerate(pattern):
            a = acc[:, gi * LANES:(gi + 1) * LANES]
            if code != PLAIN:
                a = _rope_group(a, cos_ref[...], sin_ref[...], code)
            if scale != 1.0:
                a = a * scale
            o_ref[:, gi * LANES:(gi + 1) * LANES] = a.astype(o_ref.dtype)

    distinct = sorted(set(tile_patterns), key=tile_patterns.index)
    if len(distinct) == 1:
        epilogue(distinct[0])
    else:
        for pat in distinct:
            tiles = [jj for jj, p in enumerate(tile_patterns) if p == pat]
            cond = functools.reduce(jnp.logical_or, [j == jj for jj in tiles])
            pl.when(cond)(functools.partial(epilogue, pat))


def _projection(x, col_blk, kd, gain, mod, group, w, rope, groups, tm, tn, name):
    m = x.shape[0]
    n_out = w.shape[1]
    tm = min(tm, m)
    if group is not None and group[0] == "const":
        name += "_ctx"
    assert m % tm == 0 and n_out % tn == 0 and tn % LANES == 0 and len(groups) == n_out // LANES
    if rope is None:
        groups = tuple((PLAIN, scale) for _, scale in groups)
    has_norm = gain is not None
    has_mod = mod is not None
    assert has_norm or (not has_mod and x.dtype == BF16)
    has_rope = any(code != PLAIN for code, _ in groups)
    gpt = tn // LANES
    tile_patterns = tuple(tuple(groups[t * gpt:(t + 1) * gpt]) for t in range(n_out // tn))

    in_specs = [pl.BlockSpec((tm, kd), lambda i, j: (i, col_blk))]
    args = [x]
    if has_norm:
        in_specs.append(pl.BlockSpec((1, kd), lambda i, j: (0, 0)))
        args.append(gain.reshape(1, kd).astype(F32))
    d_mod = 0
    if has_mod:
        d_mod = mod.shape[2] // 3
        row = _mod_row(group)
        in_specs.append(pl.BlockSpec((1, 1, 3 * d_mod), lambda i, j: (row(i), 0, 0)))
        args.append(mod)
    in_specs.append(pl.BlockSpec((kd, tn), lambda i, j: (0, j)))
    args.append(w)
    if has_rope:
        cos, sin, tiles_per_batch = rope
        in_specs += [pl.BlockSpec((tm, LANES), lambda i, j: (i % tiles_per_batch, 0))] * 2
        args += [cos, sin]
    kern = functools.partial(_proj_kernel, has_norm=has_norm, has_mod=has_mod, has_rope=has_rope,
                             tile_patterns=tile_patterns, d_mod=d_mod)
    return pl.pallas_call(
        kern,
        out_shape=jax.ShapeDtypeStruct((m, n_out), BF16),
        grid=(m // tm, n_out // tn),
        in_specs=in_specs,
        out_specs=pl.BlockSpec((tm, tn), lambda i, j: (i, j)),
        scratch_shapes=[pltpu.VMEM((tm, kd), BF16)] if has_norm else [],
        compiler_params=_cparams(("parallel", "arbitrary")),
        name=name,
    )(*args)


def _out_kernel(*refs, follow, d_mod):
    o_ref, gate_ref, w_ref, x_ref, gm_ref = refs[:5]
    g = gate_ref[...].astype(F32)
    u = (o_ref[...].astype(F32) * (g * jax.nn.sigmoid(g))).astype(BF16)
    y = x_ref[...] + gm_ref[0] * _dot(u, w_ref[...])
    if follow is None:
        refs[5][...] = y
        return
    yn = y * lax.rsqrt(jnp.mean(y * y, axis=-1, keepdims=True) + EPS) * refs[5][...]
    if follow == "final":
        refs[6][...] = yn
    else:
        nmod_ref, y_ref, h_ref = refs[6:9]
        y_ref[...] = y
        h_ref[...] = (yn * (1.0 + nmod_ref[0, :, d_mod:2 * d_mod])
                      + nmod_ref[0, :, 0:d_mod]).astype(h_ref.dtype)


def _out_projection(o, proj, w_out, x, mod, group, tm, name, in_place=True,
                    next_gain=None, next_mod=None):
    m, d = x.shape
    width = w_out.shape[0]
    tm = min(tm, m)
    if group[0] == "const":
        name += "_ctx"
    row = _mod_row(group)
    follow = None if next_gain is None else ("final" if next_mod is None else "layer")
    in_specs = [pl.BlockSpec((tm, width), lambda i: (i, 0)),
                pl.BlockSpec((tm, width), lambda i: (i, 0)),
                pl.BlockSpec((width, d), lambda i: (0, 0)),
                pl.BlockSpec((tm, d), lambda i: (i, 0)),
                pl.BlockSpec((1, 1, d), lambda i: (row(i), 0, 2))]
    args = [o, proj, w_out, x, mod]
    rows_f32 = jax.ShapeDtypeStruct((m, d), F32)
    row_spec = pl.BlockSpec((tm, d), lambda i: (i, 0))
    out_shape, out_specs = rows_f32, row_spec
    if follow is not None:
        in_specs.append(pl.BlockSpec((1, d), lambda i: (0, 0)))
        args.append(next_gain.reshape(1, d).astype(F32))
    if follow == "layer":
        in_specs.append(pl.BlockSpec((1, 1, 3 * d), lambda i: (row(i), 0, 0)))
        args.append(next_mod)
        out_shape = (rows_f32, jax.ShapeDtypeStruct((m, d), BF16))
        out_specs = (row_spec, row_spec)
    return pl.pallas_call(
        functools.partial(_out_kernel, follow=follow, d_mod=d),
        out_shape=out_shape,
        grid=(m // tm,),
        in_specs=in_specs,
        out_specs=out_specs,
        input_output_aliases={3: 0} if in_place and follow != "final" else {},
        compiler_params=_cparams(("parallel",)),
        name=name,
    )(*args)


def _flash(n_chunks, scores_t, values_t, stats, bufs):
    slots = bufs[0][0].shape[0]
    for m_scr, l_scr, acc in stats:
        m_scr[...] = jnp.full(m_scr.shape, NEG_INF, F32)
        l_scr[...] = jnp.zeros(l_scr.shape, F32)
        acc[...] = jnp.zeros(acc.shape, F32)

    def load_scores(c, slot):
        for s, (st, _, _) in zip(scores_t(c), bufs):
            st[slot] = s

    def softmax(slot):
        for (m_scr, l_scr, _), (st, pt, alpha_buf) in zip(stats, bufs):
            s = st[slot]
            m_prev = m_scr[...]
            m_new = jnp.maximum(m_prev, jnp.max(s, axis=0, keepdims=True))
            alpha = jnp.exp2(m_prev - m_new)
            p = jnp.exp2(s - m_new)
            l_scr[...] = alpha * l_scr[...] + jnp.sum(p, axis=0, keepdims=True)
            m_scr[...] = m_new
            alpha_buf[slot] = alpha
            pt[slot] = p.astype(pt.dtype)

    def weighted_values(c, slot):
        vts = values_t(c)
        if not isinstance(vts, tuple):
            vts = (vts,) * len(stats)
        for vt, (_, _, acc), (_, pt, alpha_buf) in zip(vts, stats, bufs):
            acc[...] = alpha_buf[slot] * acc[...] + _dot(vt, pt[slot])

    def step(c, k, first, last):
        if not last:
            load_scores(c + 1, (k + 1) % slots)
        if not first:
            weighted_values(c - 1, (k - 1) % slots)
        softmax(k)

    load_scores(0, 0)
    step(0, 0, True, n_chunks == 1)
    iters = max(n_chunks - 2, 0) // slots

    def body(i, carry):
        for k in range(slots):
            step(1 + slots * i + k, (1 + k) % slots, False, False)
        return carry
    lax.fori_loop(0, iters, body, 0)
    for c in range(1 + slots * iters, n_chunks):
        step(c, c % slots, False, c == n_chunks - 1)
    weighted_values(n_chunks - 1, (n_chunks - 1) % slots)


def _transpose_chunks(dst, src, tk):
    def body(c, carry):
        rows = pl.ds(pl.multiple_of(c * tk, tk), tk)
        dst[c] = src[rows, :].astype(F32).T.astype(dst.dtype)
        return carry
    lax.fori_loop(0, dst.shape[0], body, 0)


def _softmax_once(s, v):
    m = jnp.max(s, axis=-1, keepdims=True)
    p = jnp.exp2(s - m)
    return _dot(p.astype(BF16), v) / jnp.sum(p, axis=-1, keepdims=True)


def _gather_rows(dst, src_ctx, src_lat, dst_cols, src_cols, ctx, step):
    dst[0:ctx, dst_cols] = src_ctx[:, src_cols]

    def copy(c, carry):
        r = pl.multiple_of(c * step, step)
        dst[pl.ds(ctx + r, step), dst_cols] = src_lat[pl.ds(r, step), src_cols]
        return carry
    lax.fori_loop(0, src_lat.shape[0] // step, copy, 0)


def _mla_kernel(q_ref, kvl_ref, kvc_ref, krl_ref, krc_ref, o_ref,
                k_scr, v_scr, vt_scr, qt_scr, *stat_and_bufs, tk, ctx, heads):
    lo, hi = slice(0, LANES), slice(LANES, 2 * LANES)
    stats = [stat_and_bufs[3 * g:3 * g + 3] for g in range(heads)]
    bufs = [stat_and_bufs[3 * heads + 3 * g:3 * heads + 3 * g + 3] for g in range(heads)]

    @pl.when(pl.program_id(2) == 0)
    def _():
        for g in range(heads):
            kn = slice(2 * g * LANES, (2 * g + 1) * LANES)
            vv = slice((2 * g + 1) * LANES, (2 * g + 2) * LANES)
            _gather_rows(k_scr.at[g], kvc_ref, kvl_ref, lo, kn, ctx, ctx)
            _gather_rows(k_scr.at[g], krc_ref, krl_ref, hi, lo, ctx, ctx)
            _gather_rows(v_scr, kvc_ref, kvl_ref, lo, vv, ctx, ctx)
            _transpose_chunks(vt_scr.at[g], v_scr, tk)

    def chunk(c):
        return pl.ds(pl.multiple_of(c * tk, tk), tk)

    for g in range(heads):
        qt_scr[g] = q_ref[:, 2 * g * LANES:(2 * g + 2) * LANES].astype(F32).T.astype(BF16)
    _flash(vt_scr.shape[1],
           lambda c: tuple(_dot(k_scr[g, chunk(c), :], qt_scr[g]) for g in range(heads)),
           lambda c: tuple(vt_scr[g, c] for g in range(heads)),
           stats, bufs)
    for g, (_, l_scr, acc) in enumerate(stats):
        o_ref[:, g * MLA_V:(g + 1) * MLA_V] = (acc[...] / l_scr[...]).T.astype(o_ref.dtype)


def _flash_buffers(tq, tk, slots):
    return [pltpu.VMEM((slots, tk, tq), F32), pltpu.VMEM((slots, tk, tq), BF16),
            pltpu.VMEM((slots, 1, tq), F32)]


def _mla_attention(q_lat, kv_lat, kv_ctx, a_lat, a_ctx, kr_blk, geo, tq, tk, heads):
    b, n, ctx = geo
    h = MLA_HEADS
    nt = n // tq
    return pl.pallas_call(
        functools.partial(_mla_kernel, tk=tk, ctx=ctx, heads=heads),
        out_shape=jax.ShapeDtypeStruct((b * n, h * MLA_V), BF16),
        grid=(b, h // heads, nt),
        in_specs=[pl.BlockSpec((tq, heads * 256), lambda bi, hi, t: (bi * nt + t, hi)),
                  pl.BlockSpec((n, heads * 256), lambda bi, hi, t: (bi, hi)),
                  pl.BlockSpec((ctx, heads * 256), lambda bi, hi, t: (bi, hi)),
                  pl.BlockSpec((n, LANES), lambda bi, hi, t: (bi, kr_blk)),
                  pl.BlockSpec((ctx, LANES), lambda bi, hi, t: (bi, kr_blk))],
        out_specs=pl.BlockSpec((tq, heads * MLA_V), lambda bi, hi, t: (bi * nt + t, hi)),
        scratch_shapes=[pltpu.VMEM((heads, ctx + n, 256), BF16),
                        pltpu.VMEM((ctx + n, MLA_V), BF16),
                        pltpu.VMEM((heads, (ctx + n) // tk, MLA_V, tk), BF16),
                        pltpu.VMEM((heads, 256, tq), BF16)]
        + [pltpu.VMEM((1, tq), F32), pltpu.VMEM((1, tq), F32),
           pltpu.VMEM((MLA_V, tq), F32)] * heads
        + _flash_buffers(tq, tk, MLA_FLASH_SLOTS) * heads,
        compiler_params=_cparams(("parallel", "arbitrary", "arbitrary")),
        name="mla_attention",
    )(q_lat, kv_lat, kv_ctx, a_lat, a_ctx)


def _mla_ctx_kernel(q_ref, kv_ref, kr_ref, o_ref):
    q = q_ref[...]
    s = _dot_nt(q[:, 0:LANES], kv_ref[:, 0:LANES]) + _dot_nt(q[:, LANES:2 * LANES], kr_ref[...])
    o_ref[...] = _softmax_once(s, kv_ref[:, LANES:2 * LANES]).astype(o_ref.dtype)


def _mla_ctx_attention(q_ctx, kv_ctx, a_ctx, kr_blk, geo):
    b, n, ctx = geo
    h = MLA_HEADS
    return pl.pallas_call(
        _mla_ctx_kernel,
        out_shape=jax.ShapeDtypeStruct((b * ctx, h * MLA_V), BF16),
        grid=(b, h),
        in_specs=[pl.BlockSpec((ctx, 256), lambda bi, hi: (bi, hi)),
                  pl.BlockSpec((ctx, 256), lambda bi, hi: (bi, hi)),
                  pl.BlockSpec((ctx, LANES), lambda bi, hi: (bi, kr_blk))],
        out_specs=pl.BlockSpec((ctx, MLA_V), lambda bi, hi: (bi, hi)),
        compiler_params=_cparams(("parallel", "arbitrary")),
        name="mla_ctx_attention",
    )(q_ctx, kv_ctx, a_ctx)


def _diff_kernel(q_ref, kl_ref, vl_ref, kc_ref, vc_ref, lam_ref, sub_ref, o_ref,
                 k_scr, v_scr, vt_scr, qt_scr, m0, l0, a0, m1, l1, a1,
                 st0, pt0, alpha0, st1, pt1, alpha1, *, tk, ctx, lam_init):
    hd = DIFF_HD
    full = slice(0, 2 * hd)

    @pl.when(pl.program_id(2) == 0)
    def _():
        _gather_rows(k_scr, kc_ref, kl_ref, full, full, ctx, ctx)
        _gather_rows(v_scr, vc_ref, vl_ref, full, full, ctx, ctx)
        _transpose_chunks(vt_scr, v_scr, tk)

    def chunk(c):
        return pl.ds(pl.multiple_of(c * tk, tk), tk)

    qt_scr[...] = q_ref[...].astype(F32).T.astype(BF16)

    def scores_t(c):
        return tuple(_dot(k_scr[chunk(c), i * hd:(i + 1) * hd], qt_scr[i * hd:(i + 1) * hd, :])
                     for i in range(2))

    _flash(vt_scr.shape[0], scores_t, lambda c: vt_scr[c],
           [(m0, l0, a0), (m1, l1, a1)], [(st0, pt0, alpha0), (st1, pt1, alpha1)])

    lv = lam_ref[...]
    lam = (jnp.exp(jnp.sum(lv[0:1] * lv[1:2], axis=-1, keepdims=True))
           - jnp.exp(jnp.sum(lv[2:3] * lv[3:4], axis=-1, keepdims=True)) + lam_init)
    o = (a0[...] / l0[...] - lam * (a1[...] / l1[...])).T
    o = o * lax.rsqrt(jnp.mean(o * o, axis=-1, keepdims=True) + EPS) * sub_ref[...]
    o_ref[...] = (o * (1.0 - lam_init)).astype(o_ref.dtype)


def _diff_attention(a_lat, kv_ctx, lam_rows, subln, geo, tq, tk, lam_init):
    b, n, ctx = geo
    h = DIFF_HEADS
    nt = n // tq
    return pl.pallas_call(
        functools.partial(_diff_kernel, tk=tk, ctx=ctx, lam_init=lam_init),
        out_shape=jax.ShapeDtypeStruct((b * n, h * 256), BF16),
        grid=(b, h, nt),
        in_specs=[pl.BlockSpec((tq, 256), lambda bi, hi, t: (bi * nt + t, h + hi)),
                  pl.BlockSpec((n, 256), lambda bi, hi, t: (bi, 2 * h + hi)),
                  pl.BlockSpec((n, 256), lambda bi, hi, t: (bi, 3 * h + hi)),
                  pl.BlockSpec((ctx, 256), lambda bi, hi, t: (bi, hi)),
                  pl.BlockSpec((ctx, 256), lambda bi, hi, t: (bi, h + hi)),
                  pl.BlockSpec((8, LANES), lambda bi, hi, t: (0, 0)),
                  pl.BlockSpec((1, 256), lambda bi, hi, t: (0, 0))],
        out_specs=pl.BlockSpec((tq, 256), lambda bi, hi, t: (bi * nt + t, hi)),
        scratch_shapes=[pltpu.VMEM((ctx + n, 256), BF16), pltpu.VMEM((ctx + n, 256), BF16),
                        pltpu.VMEM(((ctx + n) // tk, 256, tk), BF16),
                        pltpu.VMEM((256, tq), BF16)]
        + [pltpu.VMEM((1, tq), F32), pltpu.VMEM((1, tq), F32), pltpu.VMEM((256, tq), F32)] * 2
        + _flash_buffers(tq, tk, DIFF_FLASH_SLOTS) * 2,
        compiler_params=_cparams(("parallel", "arbitrary", "arbitrary")),
        name="diff_attention",
    )(a_lat, a_lat, a_lat, kv_ctx, kv_ctx, lam_rows, subln.reshape(1, 256).astype(F32))


def _half_select(x, half, axis=1):
    idx = lax.broadcasted_iota(jnp.int32, x.shape, axis)
    keep = (idx < 64) if half == 0 else (idx >= 64)
    return jnp.where(keep, x, jnp.zeros_like(x))


def _swa_kernel(*refs, is_lat, tq, band):
    if is_lat:
        sink_ref, q_ref, kl_ref, vl_ref, kc_ref, vc_ref, o_ref = refs
    else:
        sink_ref, q_ref, kc_ref, vc_ref, o_ref = refs
    t = pl.program_id(1)
    per = SWA_HEADS // SWA_KV_HEADS
    if is_lat:
        n = kl_ref.shape[0]
        start = pl.multiple_of(jnp.clip(t * tq - SWA_WINDOW, 0, n - band), SWA_WINDOW)
        kpos = start + lax.broadcasted_iota(jnp.int32, (band, per * tq), 0)
        qpos = t * tq + lax.broadcasted_iota(jnp.int32, (band, per * tq), 1) % tq
        ok = jnp.abs(qpos - kpos) <= SWA_WINDOW
    for a in range(SWA_KV_HEADS // 2):
        cols = slice(a * LANES, (a + 1) * LANES)
        qs = jnp.concatenate([q_ref[:, (a * per + j) * LANES:(a * per + j + 1) * LANES]
                              for j in range(per)], axis=0)
        kc = kc_ref[:, cols]
        vct = vc_ref[:, cols].astype(F32).T.astype(BF16)
        if is_lat:
            kb = kl_ref[pl.ds(start, band), cols]
            vbt = vl_ref[pl.ds(start, band), cols].astype(F32).T.astype(BF16)
        out_t = None
        for half in range(2):
            sink = jnp.concatenate(
                [jnp.full((1, tq), sink_ref[(a * per + j) * 2 + half] * LOG2E, F32)
                 for j in range(per)], axis=1)
            st_c = _dot_nt(_half_select(kc, half, 1), qs)
            m = jnp.maximum(jnp.max(st_c, axis=0, keepdims=True), sink)
            if is_lat:
                st_b = jnp.where(ok, _dot_nt(_half_select(kb, half, 1), qs), NEG_INF)
                m = jnp.maximum(m, jnp.max(st_b, axis=0, keepdims=True))
            pt_c = jnp.exp2(st_c - m)
            l = jnp.sum(pt_c, axis=0, keepdims=True) + jnp.exp2(sink - m)
            o_t = _dot(_half_select(vct, half, 0), pt_c.astype(BF16))
            if is_lat:
                pt_b = jnp.exp2(st_b - m)
                l = l + jnp.sum(pt_b, axis=0, keepdims=True)
                o_t = o_t + _dot(_half_select(vbt, half, 0), pt_b.astype(BF16))
            o_t = o_t / l
            out_t = o_t if out_t is None else out_t + o_t
        out = out_t.T
        for j in range(per):
            o_ref[:, (a * per + j) * LANES:(a * per + j + 1) * LANES] = (
                out[j * tq:(j + 1) * tq].astype(o_ref.dtype))


def _swa_attention(a_q, a_lat, a_ctx, sink_perm, geo, tq, is_lat):
    b, n, ctx = geo
    width = SWA_HEADS * SWA_HD
    kvw = SWA_KV_HEADS * SWA_HD
    rows = n if is_lat else ctx
    nt = rows // tq
    kblk, vblk = 2 * width // kvw, 2 * width // kvw + 1
    in_specs = [pl.BlockSpec(memory_space=pltpu.SMEM),
                pl.BlockSpec((tq, width), lambda bi, t: (bi * nt + t, 1))]
    args = [sink_perm, a_q]
    if is_lat:
        in_specs += [pl.BlockSpec((n, kvw), lambda bi, t: (bi, kblk)),
                     pl.BlockSpec((n, kvw), lambda bi, t: (bi, vblk))]
        args += [a_lat, a_lat]
    in_specs += [pl.BlockSpec((ctx, kvw), lambda bi, t: (bi, kblk)),
                 pl.BlockSpec((ctx, kvw), lambda bi, t: (bi, vblk))]
    args += [a_ctx, a_ctx]
    return pl.pallas_call(
        functools.partial(_swa_kernel, is_lat=is_lat, tq=tq, band=tq + 2 * SWA_WINDOW),
        out_shape=jax.ShapeDtypeStruct((b * rows, width), BF16),
        grid=(b, nt),
        in_specs=in_specs,
        out_specs=pl.BlockSpec((tq, width), lambda bi, t: (bi * nt + t, 0)),
        compiler_params=_cparams(("parallel", "arbitrary")),
        name="swa_attention" if is_lat else "swa_ctx_attention",
    )(*args)


def _na_kernel(*refs, is_lat, rq, rk, rows):
    if is_lat:
        q_ref, kl_ref, vl_ref, kc_ref, vc_ref, bt_ref, o_ref, s_scr = refs
    else:
        q_ref, kc_ref, vc_ref, o_ref = refs
    t = pl.program_id(2)
    w = GRID_W
    band = rk * w
    q = q_ref[...]
    kc = kc_ref[...]
    vct = vc_ref[...].astype(F32).T.astype(BF16)
    if is_lat:
        r0 = t * rq
        srow = jnp.clip(r0 - NA_WIN_R // 2, 0, rows - rk)
        start = pl.multiple_of(srow * w, w)
        kb = kl_ref[pl.ds(start, band), :]
        vbt = vl_ref[pl.ds(start, band), :].astype(F32).T.astype(BF16)
        lo = lax.broadcasted_iota(jnp.int32, (w, LANES), 1) < w
    out_t = None
    for half in range(2):
        st_c = _dot_nt(_half_select(kc, half, 1), q)
        m = jnp.max(st_c, axis=0, keepdims=True)
        if is_lat:
            s_half = s_scr.at[half]
            s_half[...] = _dot_nt(_half_select(kb, half, 1), q)
            for kj in range(rk):
                kr = srow + kj
                for qp in range(rq // 2):
                    ra = r0 + 2 * qp
                    rsa = jnp.clip(ra - NA_WIN_R // 2, 0, rows - NA_WIN_R)
                    rsb = jnp.clip(ra + 1 - NA_WIN_R // 2, 0, rows - NA_WIN_R)
                    e = jnp.clip(kr - ra + NA_WIN_R - 1, 0, 2 * NA_WIN_R - 1)
                    pen0 = jnp.where((kr >= rsa) & (kr < rsa + NA_WIN_R), 0.0, NEG_INF)
                    pen1 = jnp.where((kr >= rsb) & (kr < rsb + NA_WIN_R), 0.0, NEG_INF)
                    blk = (slice(kj * w, (kj + 1) * w), slice(qp * LANES, (qp + 1) * LANES))
                    s_half[blk] = s_half[blk] + (bt_ref[half, e] * LOG2E + jnp.where(lo, pen0, pen1))
            st_b = s_half[...]
            m = jnp.maximum(m, jnp.max(st_b, axis=0, keepdims=True))
        pt_c = jnp.exp2(st_c - m)
        l = jnp.sum(pt_c, axis=0, keepdims=True)
        o_t = _dot(_half_select(vct, half, 0), pt_c.astype(BF16))
        if is_lat:
            pt_b = jnp.exp2(st_b - m)
            l = l + jnp.sum(pt_b, axis=0, keepdims=True)
            o_t = o_t + _dot(_half_select(vbt, half, 0), pt_b.astype(BF16))
        o_t = o_t / l
        out_t = o_t if out_t is None else out_t + o_t
    o_ref[...] = out_t.T.astype(o_ref.dtype)


def _na_attention(a_q, a_lat, a_ctx, bias_tab, geo, rq, is_lat):
    b, n, ctx = geo
    width = NA_HEADS * NA_HD
    pairs = width // LANES
    tq = rq * GRID_W
    rk = rq + NA_WIN_R
    grid_rows = n // GRID_W
    rows = n if is_lat else ctx
    assert grid_rows >= rk and rows % tq == 0
    nt = rows // tq
    qo, ko, vo = pairs, 2 * pairs, 3 * pairs
    in_specs = [pl.BlockSpec((tq, LANES), lambda bi, pi, t: (bi * nt + t, qo + pi))]
    args = [a_q]
    if is_lat:
        in_specs += [pl.BlockSpec((n, LANES), lambda bi, pi, t: (bi, ko + pi)),
                     pl.BlockSpec((n, LANES), lambda bi, pi, t: (bi, vo + pi))]
        args += [a_lat, a_lat]
    in_specs += [pl.BlockSpec((ctx, LANES), lambda bi, pi, t: (bi, ko + pi)),
                 pl.BlockSpec((ctx, LANES), lambda bi, pi, t: (bi, vo + pi))]
    args += [a_ctx, a_ctx]
    scratch = []
    if is_lat:
        in_specs.append(pl.BlockSpec((2, 2 * NA_WIN_R, GRID_W, LANES),
                                     lambda bi, pi, t: (pi, 0, 0, 0)))
        args.append(bias_tab)
        scratch.append(pltpu.VMEM((2, rk * GRID_W, tq), F32))
    return pl.pallas_call(
        functools.partial(_na_kernel, is_lat=is_lat, rq=rq, rk=rk, rows=grid_rows),
        out_shape=jax.ShapeDtypeStruct((b * rows, width), BF16),
        grid=(b, pairs, nt),
        in_specs=in_specs,
        out_specs=pl.BlockSpec((tq, LANES), lambda bi, pi, t: (bi * nt + t, pi)),
        scratch_shapes=scratch,
        compiler_params=_cparams(("parallel", "arbitrary", "arbitrary")),
        name="na_attention" if is_lat else "na_ctx_attention",
    )(*args)


def _rope_tables(n, d_rot):
    t = jnp.arange(n)
    row = (t // GRID_W).astype(F32)
    col = (t % GRID_W).astype(F32)
    d_axis = d_rot // 2
    inv = ROPE_BASE ** (-jnp.arange(0, d_axis, 2, dtype=F32) / d_axis)
    ang = jnp.concatenate([row[:, None] * inv, col[:, None] * inv], axis=-1)
    cos, sin = jnp.cos(ang), jnp.sin(ang)
    reps = LANES // d_rot
    return (jnp.tile(cos, (1, 2 * reps)),
            jnp.tile(jnp.concatenate([-sin, sin], axis=-1), (1, reps)))


def _na_bias_table(rpb):
    col = jnp.arange(GRID_W)
    dc = jnp.clip(col[:, None] - col[None, :], -(NA_WIN_C - 1), NA_WIN_C - 1) + (NA_WIN_C - 1)
    cs = jnp.clip(col - NA_WIN_C // 2, 0, GRID_W - NA_WIN_C)[None, :]
    col_ok = (col[:, None] >= cs) & (col[:, None] < cs + NA_WIN_C)
    bt = jnp.where(col_ok, rpb[:, :, dc], NEG_INF)
    bt = jnp.pad(bt, ((0, 0), (1, 1), (0, 0), (0, 0)))
    return jnp.concatenate([bt[:, 1:], bt[:, :-1]], axis=-1).astype(F32)


def _tile_plan(b, n, n_ctx, d):
    tm = _largest_divisor(math.gcd(n, b * n_ctx), (512, 256, 128))
    return dict(
        tm=tm,
        tq_flash=_largest_divisor(n, (512, 256, 128)),
        tq_diff=_largest_divisor(n, (512, 256, 128)),
        tk=_largest_divisor(n + n_ctx, (768, 640, 512, 384, 256, 128)),
        tk_mla=_largest_divisor(n + n_ctx, (768, 640, 512, 384, 256, 128)),
        heads_mla=2,
        tq_swa=128,
        rq_na=4,
    )


def kernel(x, c, ctx, c_ctx, l0_mod_w, l0_mod_b, l0_norm, l0_w_in, l0_q_norm, l0_w_qb, l0_kv_norm, l0_w_kvb, l0_w_out, l1_mod_w, l1_mod_b, l1_norm, l1_w_in, l1_sink, l1_w_out, l2_mod_w, l2_mod_b, l2_norm, l2_w_in, l2_rpb, l2_w_out, l3_mod_w, l3_mod_b, l3_norm, l3_w_in, l3_lam_q1, l3_lam_k1, l3_lam_q2, l3_lam_k2, l3_subln, l3_w_out, final_norm):
    b, n, d = x.shape
    n_ctx = ctx.shape[1]
    geo = (b, n, n_ctx)
    assert b < 8 and n % GRID_W == 0
    tp = _tile_plan(b, n, n_ctx, d)
    tm = tp["tm"]
    lat, cx = ("batch", n // tm), ("const", b)

    x_lat = x.reshape(b * n, d)
    x_ctx = ctx.reshape(b * n_ctx, d)
    c_rows = jnp.concatenate([c, c_ctx[None, :], jnp.zeros((8 - b - 1, d), F32)], axis=0)
    rope64 = _rope_tables(n, 64) + (n // tm,)
    rope128 = _rope_tables(n, 128) + (n // tm,)

    def groups(*spans):
        out = []
        for cols, code, scale in spans:
            out += [(code, scale)] * (cols // LANES)
        return tuple(out)

    def both(fn):
        return fn(x_lat, lat, True), fn(x_ctx, cx, False)

    def project_normed(h_lat, h_ctx, w, rope, grp_cfg, tn, name):
        return (_projection(h_lat, 0, d, None, None, None, w, rope, grp_cfg, tm, tn, name),
                _projection(h_ctx, 0, d, None, None, None, w, None, grp_cfg, tm, tn, name + "_ctx"))

    mods = [_modulation(c_rows, w, bias) for w, bias in
            ((l0_mod_w, l0_mod_b), (l1_mod_w, l1_mod_b), (l2_mod_w, l2_mod_b), (l3_mod_w, l3_mod_b))]

    mod = mods[0]
    r = MLA_RANK
    w_in = jnp.concatenate([l0_w_in[:, 2 * r + MLA_ROPE:], l0_w_in[:, :2 * r + MLA_ROPE],
                            jnp.zeros((d, 2 * LANES - MLA_ROPE), F32)], axis=1).astype(BF16)
    width = MLA_HEADS * MLA_V
    g0 = groups((width + 2 * r, PLAIN, 1.0), (LANES, ROPE64, 1.0), (LANES, PLAIN, 1.0))
    a_lat, a_ctx = both(lambda xs, grp, is_lat: _projection(
        xs, 0, d, l0_norm, mod, grp, w_in, rope64 if is_lat else None, g0, tm, 13 * LANES,
        "mla_in"))
    qk = MLA_NOPE + MLA_ROPE
    w_q = l0_w_qb.reshape(r, MLA_HEADS, qk)
    w_q = jnp.pad(w_q, ((0, 0), (0, 0), (0, 256 - qk))).reshape(r, MLA_HEADS * 256).astype(BF16)
    q_scale = qk ** -0.5 * LOG2E
    gq = groups((LANES, PLAIN, q_scale), (LANES, ROPE64, q_scale)) * MLA_HEADS
    gkv = groups((MLA_HEADS * 256, PLAIN, 1.0))
    w_kv = l0_w_kvb.astype(BF16)
    q_lat, kv_lat = (
        _projection(a_lat, width // r, r, l0_q_norm, None, None, w_q, rope64, gq, tm, 4096, "mla_q"),
        _projection(a_lat, width // r + 1, r, l0_kv_norm, None, None, w_kv, None, gkv, tm, 4096,
                    "mla_kv"))
    q_ctx, kv_ctx = (
        _projection(a_ctx, width // r, r, l0_q_norm, None, None, w_q, None, gq, tm, 4096,
                    "mla_q_ctx"),
        _projection(a_ctx, width // r + 1, r, l0_kv_norm, None, None, w_kv, None, gkv, tm, 4096,
                    "mla_kv_ctx"))
    kr_blk = (width + 2 * r) // LANES
    o_lat = _mla_attention(q_lat, kv_lat, kv_ctx, a_lat, a_ctx, kr_blk, geo, tp["tq_flash"],
                           tp["tk_mla"], tp["heads_mla"])
    o_ctx = _mla_ctx_attention(q_ctx, kv_ctx, a_ctx, kr_blk, geo)
    w_out = l0_w_out.astype(BF16)
    x_lat, h_lat = _out_projection(o_lat, a_lat, w_out, x_lat, mod, lat, tm, "mla_out", False,
                                   l1_norm, mods[1])
    x_ctx, h_ctx = _out_projection(o_ctx, a_ctx, w_out, x_ctx, mod, cx, tm, "mla_out", False,
                                   l1_norm, mods[1])

    mod = mods[1]
    per = SWA_HEADS // SWA_KV_HEADS
    pairs_kv = SWA_KV_HEADS // 2

    def pair_heads(w, axis):
        shape = w.shape
        w = w.reshape(shape[:axis] + (pairs_kv, 2, per, SWA_HD) + shape[axis + 1:])
        return jnp.swapaxes(w, axis + 1, axis + 2).reshape(shape)

    width = SWA_HEADS * SWA_HD
    kvw = SWA_KV_HEADS * SWA_HD
    w_q, w_k, w_v, w_g = (l1_w_in[:, :width], l1_w_in[:, width:width + kvw],
                          l1_w_in[:, width + kvw:width + 2 * kvw], l1_w_in[:, width + 2 * kvw:])
    w_in = jnp.concatenate([pair_heads(w_g, 1), pair_heads(w_q, 1), w_k, w_v],
                           axis=1).astype(BF16)
    g1 = groups((width, PLAIN, 1.0), (width, ROPE64, SWA_HD ** -0.5 * LOG2E),
                (kvw, ROPE64, 1.0), (kvw, PLAIN, 1.0))
    a_lat, a_ctx = project_normed(h_lat, h_ctx, w_in, rope64, g1, 2304, "swa_in")
    sink_perm = jnp.swapaxes(l1_sink.reshape(pairs_kv, 2, per), 1, 2).reshape(SWA_HEADS)
    tq = tp["tq_swa"]
    o_lat = _swa_attention(a_lat, a_lat, a_ctx, sink_perm, geo, tq, True)
    o_ctx = _swa_attention(a_ctx, a_lat, a_ctx, sink_perm, geo, tq, False)
    w_out = pair_heads(l1_w_out, 0).astype(BF16)
    x_lat, h_lat = _out_projection(o_lat, a_lat, w_out, x_lat, mod, lat, tm, "swa_out", True,
                                   l2_norm, mods[2])
    x_ctx, h_ctx = _out_projection(o_ctx, a_ctx, w_out, x_ctx, mod, cx, tm, "swa_out", True,
                                   l2_norm, mods[2])

    mod = mods[2]
    width = NA_HEADS * NA_HD
    w_in = jnp.concatenate([l2_w_in[:, 3 * width:], l2_w_in[:, :3 * width]], axis=1).astype(BF16)
    g2 = groups((width, PLAIN, 1.0), (width, PLAIN, NA_HD ** -0.5 * LOG2E), (2 * width, PLAIN, 1.0))
    a_lat, a_ctx = project_normed(h_lat, h_ctx, w_in, None, g2, 2048, "na_in")
    bias_tab = _na_bias_table(l2_rpb)
    o_lat = _na_attention(a_lat, a_lat, a_ctx, bias_tab, geo, 2 * tp["rq_na"], True)
    o_ctx = _na_attention(a_ctx, a_lat, a_ctx, bias_tab, geo, tp["rq_na"], False)
    w_out = l2_w_out.astype(BF16)
    x_lat, h_lat = _out_projection(o_lat, a_lat, w_out, x_lat, mod, lat, tm, "na_out", True,
                                   l3_norm, mods[3])
    _, h_ctx = _out_projection(o_ctx, a_ctx, w_out, x_ctx, mod, cx, tm, "na_out", True,
                               l3_norm, mods[3])

    mod = mods[3]
    width = DIFF_HEADS * 2 * DIFF_HD
    w_in = jnp.concatenate([l3_w_in[:, 3 * width:], l3_w_in[:, :3 * width]], axis=1).astype(BF16)
    g3 = groups((width, PLAIN, 1.0), (width, ROPE128, DIFF_HD ** -0.5 * LOG2E),
                (width, ROPE128, 1.0), (width, PLAIN, 1.0))
    a_lat = _projection(h_lat, 0, d, None, None, None, w_in, rope128, g3, tm, 2048, "diff_in")
    kv_ctx = _projection(h_ctx, 0, d, None, None, None, w_in[:, 2 * width:], None,
                         g3[2 * width // LANES:], tm, 1024, "diff_in_ctx")
    lam_init = 0.8 - 0.6 * math.exp(-0.3 * 3)
    lam_rows = jnp.concatenate([jnp.stack([l3_lam_q1, l3_lam_k1, l3_lam_q2, l3_lam_k2]),
                                jnp.zeros((4, DIFF_HD), F32)], axis=0)
    o_lat = _diff_attention(a_lat, kv_ctx, lam_rows, l3_subln, geo, tp["tq_diff"], tp["tk"],
                            lam_init)
    out = _out_projection(o_lat, a_lat, l3_w_out.astype(BF16), x_lat, mod, lat, tm, "diff_out",
                          False, final_norm)
    return out.reshape(b, n, d)
```

```python
import functools
import math

import jax
import jax.numpy as jnp
from jax import lax
from jax.experimental import pallas as pl
from jax.experimental.pallas import tpu as pltpu

F32 = jnp.float32
BF16 = jnp.bfloat16

LANES = 128
LOG2E = 1.4426950408889634
NEG_INF = -1e30
EPS = 1e-6
ROPE_BASE = 10000.0
GRID_W = 64
VMEM_LIMIT = 56 * 1024 * 1024

MLA_HEADS, MLA_NOPE, MLA_ROPE, MLA_V, MLA_RANK = 16, 128, 64, 128, 512
SWA_HEADS, SWA_KV_HEADS, SWA_HD, SWA_WINDOW = 32, 4, 64, 128
NA_HEADS, NA_HD, NA_WIN_R, NA_WIN_C = 32, 64, 8, 16
DIFF_HEADS, DIFF_HD = 8, 128

PLAIN, ROPE64, ROPE128 = 0, 1, 2
MLA_FLASH_SLOTS, DIFF_FLASH_SLOTS = 2, 2


def _cparams(sem):
    return pltpu.CompilerParams(dimension_semantics=sem, vmem_limit_bytes=VMEM_LIMIT)


def _dot_nt(a, b):
    return lax.dot_general(a, b, (((1,), (1,)), ((), ())), preferred_element_type=F32)


def _dot(a, b):
    return jnp.dot(a, b, preferred_element_type=F32)


def _rep(x, n):
    return x if n == 1 else jnp.concatenate([x] * n, axis=1)


def _largest_divisor(total, candidates):
    return next(c for c in candidates if total % c == 0)


def _mod_kernel(c_ref, w_ref, b_ref, o_ref):
    c = c_ref[...]
    s = c * jax.nn.sigmoid(c)
    o_ref[...] = _dot(s, w_ref[...]) + b_ref[...]


def _modulation(c_rows, mod_w, mod_b):
    r, d = c_rows.shape
    n = mod_w.shape[1]
    tn = math.gcd(n, 512)
    out = pl.pallas_call(
        _mod_kernel,
        out_shape=jax.ShapeDtypeStruct((r, n), F32),
        grid=(n // tn,),
        in_specs=[pl.BlockSpec((r, d), lambda j: (0, 0)),
                  pl.BlockSpec((d, tn), lambda j: (0, j)),
                  pl.BlockSpec((1, tn), lambda j: (0, j))],
        out_specs=pl.BlockSpec((r, tn), lambda j: (0, j)),
        compiler_params=_cparams(("arbitrary",)),
        name="modulation",
    )(c_rows, mod_w, mod_b.reshape(1, n))
    return out.reshape(r, 1, n)


def _mod_row(group):
    kind, val = group
    return (lambda i: i // val) if kind == "batch" else (lambda i: val)


def _rope_group(a, cos, sin, code):
    if code == ROPE64:
        lane = lax.broadcasted_iota(jnp.int32, a.shape, 1)
        first = (lane % 64) < 32
        partner = jnp.where(first, pltpu.roll(a, 96, 1), pltpu.roll(a, 32, 1))
    else:
        partner = pltpu.roll(a, 64, 1)
    return a * cos + partner * sin


def _proj_kernel(*refs, has_norm, has_mod, has_rope, tile_patterns, d_mod):
    it = iter(refs)
    x_ref = next(it)
    g_ref = next(it) if has_norm else None
    mod_ref = next(it) if has_mod else None
    w_ref = next(it)
    cos_ref, sin_ref = (next(it), next(it)) if has_rope else (None, None)
    o_ref = next(it)
    h_ref = next(it) if has_norm else x_ref
    j = pl.program_id(1)

    if has_norm:
        @pl.when(j == 0)
        def _():
            x = x_ref[...].astype(F32)
            y = x * lax.rsqrt(jnp.mean(x * x, axis=-1, keepdims=True) + EPS) * g_ref[...]
            if has_mod:
                shift = mod_ref[0, :, 0:d_mod]
                scale = mod_ref[0, :, d_mod:2 * d_mod]
                y = y * (1.0 + scale) + shift
            h_ref[...] = y.astype(BF16)

    def epilogue(pattern):
        acc = _dot(h_ref[...], w_ref[...])
        for gi, (code, scale) in enumerate(pattern):
            a = acc[:, gi * LANES:(gi + 1) * LANES]
            if code != PLAIN:
                a = _rope_group(a, cos_ref[...], sin_ref[...], code)
            if scale != 1.0:
                a = a * scale
            o_ref[:, gi * LANES:(gi + 1) * LANES] = a.astype(o_ref.dtype)

    distinct = sorted(set(tile_patterns), key=tile_patterns.index)
    if len(distinct) == 1:
        epilogue(distinct[0])
    else:
        for pat in distinct:
            tiles = [jj for jj, p in enumerate(tile_patterns) if p == pat]
            cond = functools.reduce(jnp.logical_or, [j == jj for jj in tiles])
            pl.when(cond)(functools.partial(epilogue, pat))


def _projection(x, col_blk, kd, gain, mod, group, w, rope, groups, tm, tn, name):
    m = x.shape[0]
    n_out = w.shape[1]
    tm = min(tm, m)
    if group is not None and group[0] == "const":
        name += "_ctx"
    assert m % tm == 0 and n_out % tn == 0 and tn % LANES == 0 and len(groups) == n_out // LANES
    if rope is None:
        groups = tuple((PLAIN, scale) for _, scale in groups)
    has_norm = gain is not None
    has_mod = mod is not None
    assert has_norm or (not has_mod and x.dtype == BF16)
    has_rope = any(code != PLAIN for code, _ in groups)
    gpt = tn // LANES
    tile_patterns = tuple(tuple(groups[t * gpt:(t + 1) * gpt]) for t in range(n_out // tn))

    in_specs = [pl.BlockSpec((tm, kd), lambda i, j: (i, col_blk))]
    args = [x]
    if has_norm:
        in_specs.append(pl.BlockSpec((1, kd), lambda i, j: (0, 0)))
        args.append(gain.reshape(1, kd).astype(F32))
    d_mod = 0
    if has_mod:
        d_mod = mod.shape[2] // 3
        row = _mod_row(group)
        in_specs.append(pl.BlockSpec((1, 1, 3 * d_mod), lambda i, j: (row(i), 0, 0)))
        args.append(mod)
    in_specs.append(pl.BlockSpec((kd, tn), lambda i, j: (0, j)))
    args.append(w)
    if has_rope:
        cos, sin, tiles_per_batch = rope
        in_specs += [pl.BlockSpec((tm, LANES), lambda i, j: (i % tiles_per_batch, 0))] * 2
        args += [cos, sin]
    kern = functools.partial(_proj_kernel, has_norm=has_norm, has_mod=has_mod, has_rope=has_rope,
                             tile_patterns=tile_patterns, d_mod=d_mod)
    return pl.pallas_call(
        kern,
        out_shape=jax.ShapeDtypeStruct((m, n_out), BF16),
        grid=(m // tm, n_out // tn),
        in_specs=in_specs,
        out_specs=pl.BlockSpec((tm, tn), lambda i, j: (i, j)),
        scratch_shapes=[pltpu.VMEM((tm, kd), BF16)] if has_norm else [],
        compiler_params=_cparams(("parallel", "arbitrary")),
        name=name,
    )(*args)


def _out_kernel(*refs, follow, d_mod):
    o_ref, gate_ref, w_ref, x_ref, gm_ref = refs[:5]
    g = gate_ref[...].astype(F32)
    u = (o_ref[...].astype(F32) * (g * jax.nn.sigmoid(g))).astype(BF16)
    y = x_ref[...] + gm_ref[0] * _dot(u, w_ref[...])
    if follow is None:
        refs[5][...] = y
        return
    yn = y * lax.rsqrt(jnp.mean(y * y, axis=-1, keepdims=True) + EPS) * refs[5][...]
    if follow == "final":
        refs[6][...] = yn
    else:
        nmod_ref, y_ref, h_ref = refs[6:9]
        y_ref[...] = y
        h_ref[...] = (yn * (1.0 + nmod_ref[0, :, d_mod:2 * d_mod])
                      + nmod_ref[0, :, 0:d_mod]).astype(h_ref.dtype)


def _out_projection(o, proj, w_out, x, mod, group, tm, name, in_place=True,
                    next_gain=None, next_mod=None):
    m, d = x.shape
    width = w_out.shape[0]
    tm = min(tm, m)
    if group[0] == "const":
        name += "_ctx"
    row = _mod_row(group)
    follow = None if next_gain is None else ("final" if next_mod is None else "layer")
    in_specs = [pl.BlockSpec((tm, width), lambda i: (i, 0)),
                pl.BlockSpec((tm, width), lambda i: (i, 0)),
                pl.BlockSpec((width, d), lambda i: (0, 0)),
                pl.BlockSpec((tm, d), lambda i: (i, 0)),
                pl.BlockSpec((1, 1, d), lambda i: (row(i), 0, 2))]
    args = [o, proj, w_out, x, mod]
    rows_f32 = jax.ShapeDtypeStruct((m, d), F32)
    row_spec = pl.BlockSpec((tm, d), lambda i: (i, 0))
    out_shape, out_specs = rows_f32, row_spec
    if follow is not None:
        in_specs.append(pl.BlockSpec((1, d), lambda i: (0, 0)))
        args.append(next_gain.reshape(1, d).astype(F32))
    if follow == "layer":
        in_specs.append(pl.BlockSpec((1, 1, 3 * d), lambda i: (row(i), 0, 0)))
        args.append(next_mod)
        out_shape = (rows_f32, jax.ShapeDtypeStruct((m, d), BF16))
        out_specs = (row_spec, row_spec)
    return pl.pallas_call(
        functools.partial(_out_kernel, follow=follow, d_mod=d),
        out_shape=out_shape,
        grid=(m // tm,),
        in_specs=in_specs,
        out_specs=out_specs,
        input_output_aliases={3: 0} if in_place and follow != "final" else {},
        compiler_params=_cparams(("parallel",)),
        name=name,
    )(*args)


def _flash(n_chunks, scores_t, values_t, stats, bufs):
    slots = bufs[0][0].shape[0]
    for m_scr, l_scr, acc in stats:
        m_scr[...] = jnp.full(m_scr.shape, NEG_INF, F32)
        l_scr[...] = jnp.zeros(l_scr.shape, F32)
        acc[...] = jnp.zeros(acc.shape, F32)

    def load_scores(c, slot):
        for s, (st, _, _) in zip(scores_t(c), bufs):
            st[slot] = s

    def softmax(slot):
        for (m_scr, l_scr, _), (st, pt, alpha_buf) in zip(stats, bufs):
            s = st[slot]
            m_prev = m_scr[...]
            m_new = jnp.maximum(m_prev, jnp.max(s, axis=0, keepdims=True))
            alpha = jnp.exp2(m_prev - m_new)
            p = jnp.exp2(s - m_new)
            l_scr[...] = alpha * l_scr[...] + jnp.sum(p, axis=0, keepdims=True)
            m_scr[...] = m_new
            alpha_buf[slot] = alpha
            pt[slot] = p.astype(pt.dtype)

    def weighted_values(c, slot):
        vts = values_t(c)
        if not isinstance(vts, tuple):
            vts = (vts,) * len(stats)
        for vt, (_, _, acc), (_, pt, alpha_buf) in zip(vts, stats, bufs):
            acc[...] = alpha_buf[slot] * acc[...] + _dot(vt, pt[slot])

    def step(c, k, first, last):
        if not last:
            load_scores(c + 1, (k + 1) % slots)
        if not first:
            weighted_values(c - 1, (k - 1) % slots)
        softmax(k)

    load_scores(0, 0)
    step(0, 0, True, n_chunks == 1)
    iters = max(n_chunks - 2, 0) // slots

    def body(i, carry):
        for k in range(slots):
            step(1 + slots * i + k, (1 + k) % slots, False, False)
        return carry
    lax.fori_loop(0, iters, body, 0)
    for c in range(1 + slots * iters, n_chunks):
        step(c, c % slots, False, c == n_chunks - 1)
    weighted_values(n_chunks - 1, (n_chunks - 1) % slots)


def _transpose_chunks(dst, src, tk):
    def body(c, carry):
        rows = pl.ds(pl.multiple_of(c * tk, tk), tk)
        dst[c] = src[rows, :].astype(F32).T.astype(dst.dtype)
        return carry
    lax.fori_loop(0, dst.shape[0], body, 0)


def _softmax_once(s, v):
    m = jnp.max(s, axis=-1, keepdims=True)
    p = jnp.exp2(s - m)
    return _dot(p.astype(BF16), v) / jnp.sum(p, axis=-1, keepdims=True)


def _gather_rows(dst, src_ctx, src_lat, dst_cols, src_cols, ctx, step):
    dst[0:ctx, dst_cols] = src_ctx[:, src_cols]

    def copy(c, carry):
        r = pl.multiple_of(c * step, step)
        dst[pl.ds(ctx + r, step), dst_cols] = src_lat[pl.ds(r, step), src_cols]
        return carry
    lax.fori_loop(0, src_lat.shape[0] // step, copy, 0)


def _mla_kernel(q_ref, kvl_ref, kvc_ref, krl_ref, krc_ref, o_ref,
                k_scr, v_scr, vt_scr, qt_scr, *stat_and_bufs, tk, ctx, heads):
    lo, hi = slice(0, LANES), slice(LANES, 2 * LANES)
    stats = [stat_and_bufs[3 * g:3 * g + 3] for g in range(heads)]
    bufs = [stat_and_bufs[3 * heads + 3 * g:3 * heads + 3 * g + 3] for g in range(heads)]

    @pl.when(pl.program_id(2) == 0)
    def _():
        for g in range(heads):
            kn = slice(2 * g * LANES, (2 * g + 1) * LANES)
            vv = slice((2 * g + 1) * LANES, (2 * g + 2) * LANES)
            _gather_rows(k_scr.at[g], kvc_ref, kvl_ref, lo, kn, ctx, ctx)
            _gather_rows(k_scr.at[g], krc_ref, krl_ref, hi, lo, ctx, ctx)
            _gather_rows(v_scr, kvc_ref, kvl_ref, lo, vv, ctx, ctx)
            _transpose_chunks(vt_scr.at[g], v_scr, tk)

    def chunk(c):
        return pl.ds(pl.multiple_of(c * tk, tk), tk)

    for g in range(heads):
        qt_scr[g] = q_ref[:, 2 * g * LANES:(2 * g + 2) * LANES].astype(F32).T.astype(BF16)
    _flash(vt_scr.shape[1],
           lambda c: tuple(_dot(k_scr[g, chunk(c), :], qt_scr[g]) for g in range(heads)),
           lambda c: tuple(vt_scr[g, c] for g in range(heads)),
           stats, bufs)
    for g, (_, l_scr, acc) in enumerate(stats):
        o_ref[:, g * MLA_V:(g + 1) * MLA_V] = (acc[...] / l_scr[...]).T.astype(o_ref.dtype)


def _flash_buffers(tq, tk, slots):
    return [pltpu.VMEM((slots, tk, tq), F32), pltpu.VMEM((slots, tk, tq), BF16),
            pltpu.VMEM((slots, 1, tq), F32)]


def _mla_attention(q_lat, kv_lat, kv_ctx, a_lat, a_ctx, kr_blk, geo, tq, tk, heads):
    b, n, ctx = geo
    h = MLA_HEADS
    nt = n // tq
    return pl.pallas_call(
        functools.partial(_mla_kernel, tk=tk, ctx=ctx, heads=heads),
        out_shape=jax.ShapeDtypeStruct((b * n, h * MLA_V), BF16),
        grid=(b, h // heads, nt),
        in_specs=[pl.BlockSpec((tq, heads * 256), lambda bi, hi, t: (bi * nt + t, hi)),
                  pl.BlockSpec((n, heads * 256), lambda bi, hi, t: (bi, hi)),
                  pl.BlockSpec((ctx, heads * 256), lambda bi, hi, t: (bi, hi)),
                  pl.BlockSpec((n, LANES), lambda bi, hi, t: (bi, kr_blk)),
                  pl.BlockSpec((ctx, LANES), lambda bi, hi, t: (bi, kr_blk))],
        out_specs=pl.BlockSpec((tq, heads * MLA_V), lambda bi, hi, t: (bi * nt + t, hi)),
        scratch_shapes=[pltpu.VMEM((heads, ctx + n, 256), BF16),
                        pltpu.VMEM((ctx + n, MLA_V), BF16),
                        pltpu.VMEM((heads, (ctx + n) // tk, MLA_V, tk), BF16),
                        pltpu.VMEM((heads, 256, tq), BF16)]
        + [pltpu.VMEM((1, tq), F32), pltpu.VMEM((1, tq), F32),
           pltpu.VMEM((MLA_V, tq), F32)] * heads
        + _flash_buffers(tq, tk, MLA_FLASH_SLOTS) * heads,
        compiler_params=_cparams(("parallel", "arbitrary", "arbitrary")),
        name="mla_attention",
    )(q_lat, kv_lat, kv_ctx, a_lat, a_ctx)


def _mla_ctx_kernel(q_ref, kv_ref, kr_ref, o_ref):
    q = q_ref[...]
    s = _dot_nt(q[:, 0:LANES], kv_ref[:, 0:LANES]) + _dot_nt(q[:, LANES:2 * LANES], kr_ref[...])
    o_ref[...] = _softmax_once(s, kv_ref[:, LANES:2 * LANES]).astype(o_ref.dtype)


def _mla_ctx_attention(q_ctx, kv_ctx, a_ctx, kr_blk, geo):
    b, n, ctx = geo
    h = MLA_HEADS
    return pl.pallas_call(
        _mla_ctx_kernel,
        out_shape=jax.ShapeDtypeStruct((b * ctx, h * MLA_V), BF16),
        grid=(b, h),
        in_specs=[pl.BlockSpec((ctx, 256), lambda bi, hi: (bi, hi)),
                  pl.BlockSpec((ctx, 256), lambda bi, hi: (bi, hi)),
                  pl.BlockSpec((ctx, LANES), lambda bi, hi: (bi, kr_blk))],
        out_specs=pl.BlockSpec((ctx, MLA_V), lambda bi, hi: (bi, hi)),
        compiler_params=_cparams(("parallel", "arbitrary")),
        name="mla_ctx_attention",
    )(q_ctx, kv_ctx, a_ctx)


def _diff_kernel(q_ref, kl_ref, vl_ref, kc_ref, vc_ref, lam_ref, sub_ref, o_ref,
                 k_scr, v_scr, vt_scr, qt_scr, m0, l0, a0, m1, l1, a1,
                 st0, pt0, alpha0, st1, pt1, alpha1, *, tk, ctx, lam_init):
    hd = DIFF_HD
    full = slice(0, 2 * hd)

    @pl.when(pl.program_id(2) == 0)
    def _():
        _gather_rows(k_scr, kc_ref, kl_ref, full, full, ctx, ctx)
        _gather_rows(v_scr, vc_ref, vl_ref, full, full, ctx, ctx)
        _transpose_chunks(vt_scr, v_scr, tk)

    def chunk(c):
        return pl.ds(pl.multiple_of(c * tk, tk), tk)

    qt_scr[...] = q_ref[...].astype(F32).T.astype(BF16)

    def scores_t(c):
        return tuple(_dot(k_scr[chunk(c), i * hd:(i + 1) * hd], qt_scr[i * hd:(i + 1) * hd, :])
                     for i in range(2))

    _flash(vt_scr.shape[0], scores_t, lambda c: vt_scr[c],
           [(m0, l0, a0), (m1, l1, a1)], [(st0, pt0, alpha0), (st1, pt1, alpha1)])

    lv = lam_ref[...]
    lam = (jnp.exp(jnp.sum(lv[0:1] * lv[1:2], axis=-1, keepdims=True))
           - jnp.exp(jnp.sum(lv[2:3] * lv[3:4], axis=-1, keepdims=True)) + lam_init)
    o = (a0[...] / l0[...] - lam * (a1[...] / l1[...])).T
    o = o * lax.rsqrt(jnp.mean(o * o, axis=-1, keepdims=True) + EPS) * sub_ref[...]
    o_ref[...] = (o * (1.0 - lam_init)).astype(o_ref.dtype)


def _diff_attention(a_lat, kv_ctx, lam_rows, subln, geo, tq, tk, lam_init):
    b, n, ctx = geo
    h = DIFF_HEADS
    nt = n // tq
    return pl.pallas_call(
        functools.partial(_diff_kernel, tk=tk, ctx=ctx, lam_init=lam_init),
        out_shape=jax.ShapeDtypeStruct((b * n, h * 256), BF16),
        grid=(b, h, nt),
        in_specs=[pl.BlockSpec((tq, 256), lambda bi, hi, t: (bi * nt + t, h + hi)),
                  pl.BlockSpec((n, 256), lambda bi, hi, t: (bi, 2 * h + hi)),
                  pl.BlockSpec((n, 256), lambda bi, hi, t: (bi, 3 * h + hi)),
                  pl.BlockSpec((ctx, 256), lambda bi, hi, t: (bi, hi)),
                  pl.BlockSpec((ctx, 256), lambda bi, hi, t: (bi, h + hi)),
                  pl.BlockSpec((8, LANES), lambda bi, hi, t: (0, 0)),
                  pl.BlockSpec((1, 256), lambda bi, hi, t: (0, 0))],
        out_specs=pl.BlockSpec((tq, 256), lambda bi, hi, t: (bi * nt + t, hi)),
        scratch_shapes=[pltpu.VMEM((ctx + n, 256), BF16), pltpu.VMEM((ctx + n, 256), BF16),
                        pltpu.VMEM(((ctx + n) // tk, 256, tk), BF16),
                        pltpu.VMEM((256, tq), BF16)]
        + [pltpu.VMEM((1, tq), F32), pltpu.VMEM((1, tq), F32), pltpu.VMEM((256, tq), F32)] * 2
        + _flash_buffers(tq, tk, DIFF_FLASH_SLOTS) * 2,
        compiler_params=_cparams(("parallel", "arbitrary", "arbitrary")),
        name="diff_attention",
    )(a_lat, a_lat, a_lat, kv_ctx, kv_ctx, lam_rows, subln.reshape(1, 256).astype(F32))


def _half_select(x, half, axis=1):
    idx = lax.broadcasted_iota(jnp.int32, x.shape, axis)
    keep = (idx < 64) if half == 0 else (idx >= 64)
    return jnp.where(keep, x, jnp.zeros_like(x))


def _swa_kernel(*refs, is_lat, tq, band):
    if is_lat:
        sink_ref, q_ref, kl_ref, vl_ref, kc_ref, vc_ref, o_ref = refs
    else:
        sink_ref, q_ref, kc_ref, vc_ref, o_ref = refs
    t = pl.program_id(1)
    per = SWA_HEADS // SWA_KV_HEADS
    if is_lat:
        n = kl_ref.shape[0]
        start = pl.multiple_of(jnp.clip(t * tq - SWA_WINDOW, 0, n - band), SWA_WINDOW)
        kpos = start + lax.broadcasted_iota(jnp.int32, (band, per * tq), 0)
        qpos = t * tq + lax.broadcasted_iota(jnp.int32, (band, per * tq), 1) % tq
        ok = jnp.abs(qpos - kpos) <= SWA_WINDOW
    for a in range(SWA_KV_HEADS // 2):
        cols = slice(a * LANES, (a + 1) * LANES)
        qs = jnp.concatenate([q_ref[:, (a * per + j) * LANES:(a * per + j + 1) * LANES]
                              for j in range(per)], axis=0)
        kc = kc_ref[:, cols]
        vct = vc_ref[:, cols].astype(F32).T.astype(BF16)
        if is_lat:
            kb = kl_ref[pl.ds(start, band), cols]
            vbt = vl_ref[pl.ds(start, band), cols].astype(F32).T.astype(BF16)
        out_t = None
        for half in range(2):
            sink = jnp.concatenate(
                [jnp.full((1, tq), sink_ref[(a * per + j) * 2 + half] * LOG2E, F32)
                 for j in range(per)], axis=1)
            st_c = _dot_nt(_half_select(kc, half, 1), qs)
            m = jnp.maximum(jnp.max(st_c, axis=0, keepdims=True), sink)
            if is_lat:
                st_b = jnp.where(ok, _dot_nt(_half_select(kb, half, 1), qs), NEG_INF)
                m = jnp.maximum(m, jnp.max(st_b, axis=0, keepdims=True))
            pt_c = jnp.exp2(st_c - m)
            l = jnp.sum(pt_c, axis=0, keepdims=True) + jnp.exp2(sink - m)
            o_t = _dot(_half_select(vct, half, 0), pt_c.astype(BF16))
            if is_lat:
                pt_b = jnp.exp2(st_b - m)
                l = l + jnp.sum(pt_b, axis=0, keepdims=True)
                o_t = o_t + _dot(_half_select(vbt, half, 0), pt_b.astype(BF16))
            o_t = o_t / l
            out_t = o_t if out_t is None else out_t + o_t
        out = out_t.T
        for j in range(per):
            o_ref[:, (a * per + j) * LANES:(a * per + j + 1) * LANES] = (
                out[j * tq:(j + 1) * tq].astype(o_ref.dtype))


def _swa_attention(a_q, a_lat, a_ctx, sink_perm, geo, tq, is_lat):
    b, n, ctx = geo
    width = SWA_HEADS * SWA_HD
    kvw = SWA_KV_HEADS * SWA_HD
    rows = n if is_lat else ctx
    nt = rows // tq
    kblk, vblk = 2 * width // kvw, 2 * width // kvw + 1
    in_specs = [pl.BlockSpec(memory_space=pltpu.SMEM),
                pl.BlockSpec((tq, width), lambda bi, t: (bi * nt + t, 1))]
    args = [sink_perm, a_q]
    if is_lat:
        in_specs += [pl.BlockSpec((n, kvw), lambda bi, t: (bi, kblk)),
                     pl.BlockSpec((n, kvw), lambda bi, t: (bi, vblk))]
        args += [a_lat, a_lat]
    in_specs += [pl.BlockSpec((ctx, kvw), lambda bi, t: (bi, kblk)),
                 pl.BlockSpec((ctx, kvw), lambda bi, t: (bi, vblk))]
    args += [a_ctx, a_ctx]
    return pl.pallas_call(
        functools.partial(_swa_kernel, is_lat=is_lat, tq=tq, band=tq + 2 * SWA_WINDOW),
        out_shape=jax.ShapeDtypeStruct((b * rows, width), BF16),
        grid=(b, nt),
        in_specs=in_specs,
        out_specs=pl.BlockSpec((tq, width), lambda bi, t: (bi * nt + t, 0)),
        compiler_params=_cparams(("parallel", "arbitrary")),
        name="swa_attention" if is_lat else "swa_ctx_attention",
    )(*args)


def _na_kernel(*refs, is_lat, rq, rk, rows):
    if is_lat:
        q_ref, kl_ref, vl_ref, kc_ref, vc_ref, bt_ref, o_ref, s_scr, p_scr = refs
    else:
        q_ref, kc_ref, vc_ref, o_ref = refs
    t = pl.program_id(2)
    w = GRID_W
    band = rk * w
    q = q_ref[...]
    kc = kc_ref[...]
    vct = vc_ref[...].astype(F32).T.astype(BF16)
    if is_lat:
        r0 = t * rq
        srow = jnp.clip(r0 - NA_WIN_R // 2, 0, rows - rk)
        start = pl.multiple_of(srow * w, w)
        kb = kl_ref[pl.ds(start, band), :]
        vbt = vl_ref[pl.ds(start, band), :].astype(F32).T.astype(BF16)
        lo = lax.broadcasted_iota(jnp.int32, (w, LANES), 1) < w

    def bias_and_mask(s_half, half):
        for kj in range(rk):
            for qp in range(rq // 2):
                blk = (slice(kj * w, (kj + 1) * w), slice(qp * LANES, (qp + 1) * LANES))
                kr = srow + kj
                ra = r0 + 2 * qp
                rsa = jnp.clip(ra - NA_WIN_R // 2, 0, rows - NA_WIN_R)
                rsb = jnp.clip(ra + 1 - NA_WIN_R // 2, 0, rows - NA_WIN_R)
                e = jnp.clip(kr - ra + NA_WIN_R - 1, 0, 2 * NA_WIN_R - 1)
                pen0 = jnp.where((kr >= rsa) & (kr < rsa + NA_WIN_R), 0.0, NEG_INF)
                pen1 = jnp.where((kr >= rsb) & (kr < rsb + NA_WIN_R), 0.0, NEG_INF)
                s_half[blk] = s_half[blk] + (bt_ref[half, e] * LOG2E + jnp.where(lo, pen0, pen1))

    def attend():
        out_t = None
        for half in range(2):
            st_c = _dot_nt(_half_select(kc, half, 1), q)
            m = jnp.max(st_c, axis=0, keepdims=True)
            if is_lat:
                s_half = s_scr.at[half]
                s_half[...] = _dot_nt(_half_select(kb, half, 1), q)
                bias_and_mask(s_half, half)
                st_b = s_half[...]
                m = jnp.maximum(m, jnp.max(st_b, axis=0, keepdims=True))
            pt_c = jnp.exp2(st_c - m)
            l = jnp.sum(pt_c, axis=0, keepdims=True)
            o_t = _dot(_half_select(vct, half, 0), pt_c.astype(BF16))
            if is_lat:
                pt_b = jnp.exp2(st_b - m)
                l = l + jnp.sum(pt_b, axis=0, keepdims=True)
                o_t = o_t + _dot(_half_select(vbt, half, 0), pt_b.astype(BF16))
            o_t = o_t / l
            out_t = o_t if out_t is None else out_t + o_t
        o_ref[...] = out_t.T.astype(o_ref.dtype)

    def attend_interior():
        nv = NA_WIN_R + 1
        out_t = None
        for half in range(2):
            st_c = _dot_nt(_half_select(kc, half, 1), q)
            m_c = jnp.max(st_c, axis=0, keepdims=True)
            s_half, p_half = s_scr.at[half], p_scr.at[half]
            s_half[...] = _dot_nt(_half_select(kb, half, 1), q)
            m_parts, l_parts = [], []
            for qp in range(rq // 2):
                strip = slice(qp * LANES, (qp + 1) * LANES)
                k0 = 2 * qp
                blocks = []
                for dk in range(nv):
                    bias = bt_ref[half, dk + NA_WIN_R // 2 - 1] * LOG2E
                    if dk == 0:
                        bias = jnp.where(lo, bias, NEG_INF)
                    elif dk == NA_WIN_R:
                        bias = jnp.where(lo, NEG_INF, bias)
                    blocks.append(s_half[(k0 + dk) * w:(k0 + dk + 1) * w, strip] + bias)
                s_v = jnp.concatenate(blocks, axis=0)
                m_s = jnp.maximum(m_c[:, strip], jnp.max(s_v, axis=0, keepdims=True))
                p_v = jnp.exp2(s_v - m_s)
                m_parts.append(m_s)
                l_parts.append(jnp.sum(p_v, axis=0, keepdims=True))
                p_half[k0 * w:(k0 + nv) * w, strip] = p_v.astype(BF16)
                if k0 > 0:
                    p_half[0:k0 * w, strip] = jnp.zeros((k0 * w, LANES), BF16)
                if k0 + nv < rk:
                    p_half[(k0 + nv) * w:rk * w, strip] = jnp.zeros(((rk - k0 - nv) * w, LANES), BF16)
            m = jnp.concatenate(m_parts, axis=1)
            pt_c = jnp.exp2(st_c - m)
            l = jnp.sum(pt_c, axis=0, keepdims=True) + jnp.concatenate(l_parts, axis=1)
            o_t = (_dot(_half_select(vct, half, 0), pt_c.astype(BF16))
                   + _dot(_half_select(vbt, half, 0), p_half[...]))
            o_t = o_t / l
            out_t = o_t if out_t is None else out_t + o_t
        o_ref[...] = out_t.T.astype(o_ref.dtype)

    if is_lat:
        interior = (r0 >= NA_WIN_R // 2) & (r0 <= rows - rq - NA_WIN_R // 2)
        pl.when(interior)(attend_interior)
        pl.when(jnp.logical_not(interior))(attend)
    else:
        attend()


def _na_attention(a_q, a_lat, a_ctx, bias_tab, geo, rq, is_lat):
    b, n, ctx = geo
    width = NA_HEADS * NA_HD
    pairs = width // LANES
    tq = rq * GRID_W
    rk = rq + NA_WIN_R
    grid_rows = n // GRID_W
    rows = n if is_lat else ctx
    assert grid_rows >= rk and rows % tq == 0
    nt = rows // tq
    qo, ko, vo = pairs, 2 * pairs, 3 * pairs
    in_specs = [pl.BlockSpec((tq, LANES), lambda bi, pi, t: (bi * nt + t, qo + pi))]
    args = [a_q]
    if is_lat:
        in_specs += [pl.BlockSpec((n, LANES), lambda bi, pi, t: (bi, ko + pi)),
                     pl.BlockSpec((n, LANES), lambda bi, pi, t: (bi, vo + pi))]
        args += [a_lat, a_lat]
    in_specs += [pl.BlockSpec((ctx, LANES), lambda bi, pi, t: (bi, ko + pi)),
                 pl.BlockSpec((ctx, LANES), lambda bi, pi, t: (bi, vo + pi))]
    args += [a_ctx, a_ctx]
    scratch = []
    if is_lat:
        in_specs.append(pl.BlockSpec((2, 2 * NA_WIN_R, GRID_W, LANES),
                                     lambda bi, pi, t: (pi, 0, 0, 0)))
        args.append(bias_tab)
        scratch += [pltpu.VMEM((2, rk * GRID_W, tq), F32),
                    pltpu.VMEM((2, rk * GRID_W, tq), BF16)]
    return pl.pallas_call(
        functools.partial(_na_kernel, is_lat=is_lat, rq=rq, rk=rk, rows=grid_rows),
        out_shape=jax.ShapeDtypeStruct((b * rows, width), BF16),
        grid=(b, pairs, nt),
        in_specs=in_specs,
        out_specs=pl.BlockSpec((tq, LANES), lambda bi, pi, t: (bi * nt + t, pi)),
        scratch_shapes=scratch,
        compiler_params=_cparams(("parallel", "arbitrary", "arbitrary")),
        name="na_attention" if is_lat else "na_ctx_attention",
    )(*args)


def _rope_tables(n, d_rot):
    t = jnp.arange(n)
    row = (t // GRID_W).astype(F32)
    col = (t % GRID_W).astype(F32)
    d_axis = d_rot // 2
    inv = ROPE_BASE ** (-jnp.arange(0, d_axis, 2, dtype=F32) / d_axis)
    ang = jnp.concatenate([row[:, None] * inv, col[:, None] * inv], axis=-1)
    cos, sin = jnp.cos(ang), jnp.sin(ang)
    reps = LANES // d_rot
    return (jnp.tile(cos, (1, 2 * reps)),
            jnp.tile(jnp.concatenate([-sin, sin], axis=-1), (1, reps)))


def _na_bias_table(rpb):
    col = jnp.arange(GRID_W)
    dc = jnp.clip(col[:, None] - col[None, :], -(NA_WIN_C - 1), NA_WIN_C - 1) + (NA_WIN_C - 1)
    cs = jnp.clip(col - NA_WIN_C // 2, 0, GRID_W - NA_WIN_C)[None, :]
    col_ok = (col[:, None] >= cs) & (col[:, None] < cs + NA_WIN_C)
    bt = jnp.where(col_ok, rpb[:, :, dc], NEG_INF)
    bt = jnp.pad(bt, ((0, 0), (1, 1), (0, 0), (0, 0)))
    return jnp.concatenate([bt[:, 1:], bt[:, :-1]], axis=-1).astype(F32)


def _tile_plan(b, n, n_ctx, d):
    tm = _largest_divisor(math.gcd(n, b * n_ctx), (512, 256, 128))
    return dict(
        tm=tm,
        tq_flash=_largest_divisor(n, (512, 256, 128)),
        tq_diff=_largest_divisor(n, (512, 256, 128)),
        tk=_largest_divisor(n + n_ctx, (768, 640, 512, 384, 256, 128)),
        tk_mla=_largest_divisor(n + n_ctx, (768, 640, 512, 384, 256, 128)),
        heads_mla=2,
        tq_swa=128,
        rq_na=4,
    )


def kernel(x, c, ctx, c_ctx, l0_mod_w, l0_mod_b, l0_norm, l0_w_in, l0_q_norm, l0_w_qb, l0_kv_norm, l0_w_kvb, l0_w_out, l1_mod_w, l1_mod_b, l1_norm, l1_w_in, l1_sink, l1_w_out, l2_mod_w, l2_mod_b, l2_norm, l2_w_in, l2_rpb, l2_w_out, l3_mod_w, l3_mod_b, l3_norm, l3_w_in, l3_lam_q1, l3_lam_k1, l3_lam_q2, l3_lam_k2, l3_subln, l3_w_out, final_norm):
    b, n, d = x.shape
    n_ctx = ctx.shape[1]
    geo = (b, n, n_ctx)
    assert b < 8 and n % GRID_W == 0
    tp = _tile_plan(b, n, n_ctx, d)
    tm = tp["tm"]
    lat, cx = ("batch", n // tm), ("const", b)

    x_lat = x.reshape(b * n, d)
    x_ctx = ctx.reshape(b * n_ctx, d)
    c_rows = jnp.concatenate([c, c_ctx[None, :], jnp.zeros((8 - b - 1, d), F32)], axis=0)
    rope64 = _rope_tables(n, 64) + (n // tm,)
    rope128 = _rope_tables(n, 128) + (n // tm,)

    def groups(*spans):
        out = []
        for cols, code, scale in spans:
            out += [(code, scale)] * (cols // LANES)
        return tuple(out)

    def both(fn):
        return fn(x_lat, lat, True), fn(x_ctx, cx, False)

    def project_normed(h_lat, h_ctx, w, rope, grp_cfg, tn, name):
        return (_projection(h_lat, 0, d, None, None, None, w, rope, grp_cfg, tm, tn, name),
                _projection(h_ctx, 0, d, None, None, None, w, None, grp_cfg, tm, tn, name + "_ctx"))

    mods = [_modulation(c_rows, w, bias) for w, bias in
            ((l0_mod_w, l0_mod_b), (l1_mod_w, l1_mod_b), (l2_mod_w, l2_mod_b), (l3_mod_w, l3_mod_b))]

    mod = mods[0]
    r = MLA_RANK
    w_in = jnp.concatenate([l0_w_in[:, 2 * r + MLA_ROPE:], l0_w_in[:, :2 * r + MLA_ROPE],
                            jnp.zeros((d, 2 * LANES - MLA_ROPE), F32)], axis=1).astype(BF16)
    width = MLA_HEADS * MLA_V
    g0 = groups((width + 2 * r, PLAIN, 1.0), (LANES, ROPE64, 1.0), (LANES, PLAIN, 1.0))
    a_lat, a_ctx = both(lambda xs, grp, is_lat: _projection(
        xs, 0, d, l0_norm, mod, grp, w_in, rope64 if is_lat else None, g0, tm, 13 * LANES,
        "mla_in"))
    qk = MLA_NOPE + MLA_ROPE
    w_q = l0_w_qb.reshape(r, MLA_HEADS, qk)
    w_q = jnp.pad(w_q, ((0, 0), (0, 0), (0, 256 - qk))).reshape(r, MLA_HEADS * 256).astype(BF16)
    q_scale = qk ** -0.5 * LOG2E
    gq = groups((LANES, PLAIN, q_scale), (LANES, ROPE64, q_scale)) * MLA_HEADS
    gkv = groups((MLA_HEADS * 256, PLAIN, 1.0))
    w_kv = l0_w_kvb.astype(BF16)
    q_lat, kv_lat = (
        _projection(a_lat, width // r, r, l0_q_norm, None, None, w_q, rope64, gq, tm, 4096, "mla_q"),
        _projection(a_lat, width // r + 1, r, l0_kv_norm, None, None, w_kv, None, gkv, tm, 4096,
                    "mla_kv"))
    q_ctx, kv_ctx = (
        _projection(a_ctx, width // r, r, l0_q_norm, None, None, w_q, None, gq, tm, 4096,
                    "mla_q_ctx"),
        _projection(a_ctx, width // r + 1, r, l0_kv_norm, None, None, w_kv, None, gkv, tm, 4096,
                    "mla_kv_ctx"))
    kr_blk = (width + 2 * r) // LANES
    o_lat = _mla_attention(q_lat, kv_lat, kv_ctx, a_lat, a_ctx, kr_blk, geo, tp["tq_flash"],
                           tp["tk_mla"], tp["heads_mla"])
    o_ctx = _mla_ctx_attention(q_ctx, kv_ctx, a_ctx, kr_blk, geo)
    w_out = l0_w_out.astype(BF16)
    x_lat, h_lat = _out_projection(o_lat, a_lat, w_out, x_lat, mod, lat, tm, "mla_out", False,
                                   l1_norm, mods[1])
    x_ctx, h_ctx = _out_projection(o_ctx, a_ctx, w_out, x_ctx, mod, cx, tm, "mla_out", False,
                                   l1_norm, mods[1])

    mod = mods[1]
    per = SWA_HEADS // SWA_KV_HEADS
    pairs_kv = SWA_KV_HEADS // 2

    def pair_heads(w, axis):
        shape = w.shape
        w = w.reshape(shape[:axis] + (pairs_kv, 2, per, SWA_HD) + shape[axis + 1:])
        return jnp.swapaxes(w, axis + 1, axis + 2).reshape(shape)

    width = SWA_HEADS * SWA_HD
    kvw = SWA_KV_HEADS * SWA_HD
    w_q, w_k, w_v, w_g = (l1_w_in[:, :width], l1_w_in[:, width:width + kvw],
                          l1_w_in[:, width + kvw:width + 2 * kvw], l1_w_in[:, width + 2 * kvw:])
    w_in = jnp.concatenate([pair_heads(w_g, 1), pair_heads(w_q, 1), w_k, w_v],
                           axis=1).astype(BF16)
    g1 = groups((width, PLAIN, 1.0), (width, ROPE64, SWA_HD ** -0.5 * LOG2E),
                (kvw, ROPE64, 1.0), (kvw, PLAIN, 1.0))
    a_lat, a_ctx = project_normed(h_lat, h_ctx, w_in, rope64, g1, 2304, "swa_in")
    sink_perm = jnp.swapaxes(l1_sink.reshape(pairs_kv, 2, per), 1, 2).reshape(SWA_HEADS)
    tq = tp["tq_swa"]
    o_lat = _swa_attention(a_lat, a_lat, a_ctx, sink_perm, geo, tq, True)
    o_ctx = _swa_attention(a_ctx, a_lat, a_ctx, sink_perm, geo, tq, False)
    w_out = pair_heads(l1_w_out, 0).astype(BF16)
    x_lat, h_lat = _out_projection(o_lat, a_lat, w_out, x_lat, mod, lat, tm, "swa_out", True,
                                   l2_norm, mods[2])
    x_ctx, h_ctx = _out_projection(o_ctx, a_ctx, w_out, x_ctx, mod, cx, tm, "swa_out", True,
                                   l2_norm, mods[2])

    mod = mods[2]
    width = NA_HEADS * NA_HD
    w_in = jnp.concatenate([l2_w_in[:, 3 * width:], l2_w_in[:, :3 * width]], axis=1).astype(BF16)
    g2 = groups((width, PLAIN, 1.0), (width, PLAIN, NA_HD ** -0.5 * LOG2E), (2 * width, PLAIN, 1.0))
    a_lat, a_ctx = project_normed(h_lat, h_ctx, w_in, None, g2, 2048, "na_in")
    bias_tab = _na_bias_table(l2_rpb)
    o_lat = _na_attention(a_lat, a_lat, a_ctx, bias_tab, geo, 2 * tp["rq_na"], True)
    o_ctx = _na_attention(a_ctx, a_lat, a_ctx, bias_tab, geo, tp["rq_na"], False)
    w_out = l2_w_out.astype(BF16)
    x_lat, h_lat = _out_projection(o_lat, a_lat, w_out, x_lat, mod, lat, tm, "na_out", True,
                                   l3_norm, mods[3])
    _, h_ctx = _out_projection(o_ctx, a_ctx, w_out, x_ctx, mod, cx, tm, "na_out", True,
                               l3_norm, mods[3])

    mod = mods[3]
    width = DIFF_HEADS * 2 * DIFF_HD
    w_in = jnp.concatenate([l3_w_in[:, 3 * width:], l3_w_in[:, :3 * width]], axis=1).astype(BF16)
    g3 = groups((width, PLAIN, 1.0), (width, ROPE128, DIFF_HD ** -0.5 * LOG2E),
                (width, ROPE128, 1.0), (width, PLAIN, 1.0))
    a_lat = _projection(h_lat, 0, d, None, None, None, w_in, rope128, g3, tm, 2048, "diff_in")
    kv_ctx = _projection(h_ctx, 0, d, None, None, None, w_in[:, 2 * width:], None,
                         g3[2 * width // LANES:], tm, 1024, "diff_in_ctx")
    lam_init = 0.8 - 0.6 * math.exp(-0.3 * 3)
    lam_rows = jnp.concatenate([jnp.stack([l3_lam_q1, l3_lam_k1, l3_lam_q2, l3_lam_k2]),
                                jnp.zeros((4, DIFF_HD), F32)], axis=0)
    o_lat = _diff_attention(a_lat, kv_ctx, lam_rows, l3_subln, geo, tp["tq_diff"], tp["tk"],
                            lam_init)
    out = _out_projection(o_lat, a_lat, l3_w_out.astype(BF16), x_lat, mod, lat, tm, "diff_out",
                          False, final_norm)
    return out.reshape(b, n, d)
```

```python
import functools
import math

import jax
import jax.numpy as jnp
from jax import lax
from jax.experimental import pallas as pl
from jax.experimental.pallas import tpu as pltpu

F32 = jnp.float32
BF16 = jnp.bfloat16

LANES = 128
LOG2E = 1.4426950408889634
NEG_INF = -1e30
EPS = 1e-6
ROPE_BASE = 10000.0
GRID_W = 64
VMEM_LIMIT = 56 * 1024 * 1024

MLA_HEADS, MLA_NOPE, MLA_ROPE, MLA_V, MLA_RANK = 16, 128, 64, 128, 512
SWA_HEADS, SWA_KV_HEADS, SWA_HD, SWA_WINDOW = 32, 4, 64, 128
NA_HEADS, NA_HD, NA_WIN_R, NA_WIN_C = 32, 64, 8, 16
DIFF_HEADS, DIFF_HD = 8, 128

PLAIN, ROPE64, ROPE128 = 0, 1, 2
MLA_FLASH_SLOTS, DIFF_FLASH_SLOTS = 2, 2


def _cparams(sem):
    return pltpu.CompilerParams(dimension_semantics=sem, vmem_limit_bytes=VMEM_LIMIT)


def _dot_nt(a, b):
    return lax.dot_general(a, b, (((1,), (1,)), ((), ())), preferred_element_type=F32)


def _dot(a, b):
    return jnp.dot(a, b, preferred_element_type=F32)


def _largest_divisor(total, candidates):
    return next(c for c in candidates if total % c == 0)


def _mod_kernel(c_ref, w_ref, b_ref, o_ref):
    c = c_ref[...]
    s = c * jax.nn.sigmoid(c)
    o_ref[...] = _dot(s, w_ref[...]) + b_ref[...]


def _modulation(c_rows, mod_w, mod_b):
    r, d = c_rows.shape
    n = mod_w.shape[1]
    tn = math.gcd(n, 512)
    out = pl.pallas_call(
        _mod_kernel,
        out_shape=jax.ShapeDtypeStruct((r, n), F32),
        grid=(n // tn,),
        in_specs=[pl.BlockSpec((r, d), lambda j: (0, 0)),
                  pl.BlockSpec((d, tn), lambda j: (0, j)),
                  pl.BlockSpec((1, tn), lambda j: (0, j))],
        out_specs=pl.BlockSpec((r, tn), lambda j: (0, j)),
        compiler_params=_cparams(("arbitrary",)),
        name="modulation",
    )(c_rows, mod_w, mod_b.reshape(1, n))
    return out.reshape(r, 1, n)


def _mod_row(group):
    kind, val = group
    return (lambda i: i // val) if kind == "batch" else (lambda i: val)


def _rope_group(a, cos, sin, code):
    if code == ROPE64:
        lane = lax.broadcasted_iota(jnp.int32, a.shape, 1)
        first = (lane % 64) < 32
        partner = jnp.where(first, pltpu.roll(a, 96, 1), pltpu.roll(a, 32, 1))
    else:
        partner = pltpu.roll(a, 64, 1)
    return a * cos + partner * sin


def _proj_kernel(*refs, has_norm, has_mod, has_rope, tile_patterns, d_mod):
    it = iter(refs)
    x_ref = next(it)
    g_ref = next(it) if has_norm else None
    mod_ref = next(it) if has_mod else None
    w_ref = next(it)
    cos_ref, sin_ref = (next(it), next(it)) if has_rope else (None, None)
    o_ref = next(it)
    h_ref = next(it) if has_norm else x_ref
    j = pl.program_id(1)

    if has_norm:
        @pl.when(j == 0)
        def _():
            x = x_ref[...].astype(F32)
            y = x * lax.rsqrt(jnp.mean(x * x, axis=-1, keepdims=True) + EPS) * g_ref[...]
            if has_mod:
                shift = mod_ref[0, :, 0:d_mod]
                scale = mod_ref[0, :, d_mod:2 * d_mod]
                y = y * (1.0 + scale) + shift
            h_ref[...] = y.astype(BF16)

    def epilogue(pattern):
        acc = _dot(h_ref[...], w_ref[...])
        for gi, (code, scale) in enumerate(pattern):
            a = acc[:, gi * LANES:(gi + 1) * LANES]
            if code != PLAIN:
                a = _rope_group(a, cos_ref[...], sin_ref[...], code)
            if scale != 1.0:
                a = a * scale
            o_ref[:, gi * LANES:(gi + 1) * LANES] = a.astype(o_ref.dtype)

    distinct = sorted(set(tile_patterns), key=tile_patterns.index)
    if len(distinct) == 1:
        epilogue(distinct[0])
    else:
        for pat in distinct:
            tiles = [jj for jj, p in enumerate(tile_patterns) if p == pat]
            cond = functools.reduce(jnp.logical_or, [j == jj for jj in tiles])
            pl.when(cond)(functools.partial(epilogue, pat))


def _projection(x, col_blk, kd, gain, mod, group, w, rope, groups, tm, tn, name):
    m = x.shape[0]
    n_out = w.shape[1]
    tm = min(tm, m)
    if group is not None and group[0] == "const":
        name += "_ctx"
    assert m % tm == 0 and n_out % tn == 0 and tn % LANES == 0 and len(groups) == n_out // LANES
    if rope is None:
        groups = tuple((PLAIN, scale) for _, scale in groups)
    has_norm = gain is not None
    has_mod = mod is not None
    assert has_norm or (not has_mod and x.dtype == BF16)
    has_rope = any(code != PLAIN for code, _ in groups)
    gpt = tn // LANES
    tile_patterns = tuple(tuple(groups[t * gpt:(t + 1) * gpt]) for t in range(n_out // tn))

    in_specs = [pl.BlockSpec((tm, kd), lambda i, j: (i, col_blk))]
    args = [x]
    if has_norm:
        in_specs.append(pl.BlockSpec((1, kd), lambda i, j: (0, 0)))
        args.append(gain.reshape(1, kd).astype(F32))
    d_mod = 0
    if has_mod:
        d_mod = mod.shape[2] // 3
        row = _mod_row(group)
        in_specs.append(pl.BlockSpec((1, 1, 3 * d_mod), lambda i, j: (row(i), 0, 0)))
        args.append(mod)
    in_specs.append(pl.BlockSpec((kd, tn), lambda i, j: (0, j)))
    args.append(w)
    if has_rope:
        cos, sin, tiles_per_batch = rope
        in_specs += [pl.BlockSpec((tm, LANES), lambda i, j: (i % tiles_per_batch, 0))] * 2
        args += [cos, sin]
    kern = functools.partial(_proj_kernel, has_norm=has_norm, has_mod=has_mod, has_rope=has_rope,
                             tile_patterns=tile_patterns, d_mod=d_mod)
    return pl.pallas_call(
        kern,
        out_shape=jax.ShapeDtypeStruct((m, n_out), BF16),
        grid=(m // tm, n_out // tn),
        in_specs=in_specs,
        out_specs=pl.BlockSpec((tm, tn), lambda i, j: (i, j)),
        scratch_shapes=[pltpu.VMEM((tm, kd), BF16)] if has_norm else [],
        compiler_params=_cparams(("parallel", "arbitrary")),
        name=name,
    )(*args)


def _out_kernel(*refs, follow, d_mod):
    o_ref, gate_ref, w_ref, x_ref, gm_ref = refs[:5]
    g = gate_ref[...].astype(F32)
    u = (o_ref[...].astype(F32) * (g * jax.nn.sigmoid(g))).astype(BF16)
    y = x_ref[...] + gm_ref[0] * _dot(u, w_ref[...])
    if follow is None:
        refs[5][...] = y
        return
    yn = y * lax.rsqrt(jnp.mean(y * y, axis=-1, keepdims=True) + EPS) * refs[5][...]
    if follow == "final":
        refs[6][...] = yn
    else:
        nmod_ref, y_ref, h_ref = refs[6:9]
        y_ref[...] = y
        h_ref[...] = (yn * (1.0 + nmod_ref[0, :, d_mod:2 * d_mod])
                      + nmod_ref[0, :, 0:d_mod]).astype(h_ref.dtype)


def _out_projection(o, proj, w_out, x, mod, group, tm, name, in_place=True,
                    next_gain=None, next_mod=None):
    m, d = x.shape
    width = w_out.shape[0]
    tm = min(tm, m)
    if group[0] == "const":
        name += "_ctx"
    row = _mod_row(group)
    follow = None if next_gain is None else ("final" if next_mod is None else "layer")
    in_specs = [pl.BlockSpec((tm, width), lambda i: (i, 0)),
                pl.BlockSpec((tm, width), lambda i: (i, 0)),
                pl.BlockSpec((width, d), lambda i: (0, 0)),
                pl.BlockSpec((tm, d), lambda i: (i, 0)),
                pl.BlockSpec((1, 1, d), lambda i: (row(i), 0, 2))]
    args = [o, proj, w_out, x, mod]
    rows_f32 = jax.ShapeDtypeStruct((m, d), F32)
    row_spec = pl.BlockSpec((tm, d), lambda i: (i, 0))
    out_shape, out_specs = rows_f32, row_spec
    if follow is not None:
        in_specs.append(pl.BlockSpec((1, d), lambda i: (0, 0)))
        args.append(next_gain.reshape(1, d).astype(F32))
    if follow == "layer":
        in_specs.append(pl.BlockSpec((1, 1, 3 * d), lambda i: (row(i), 0, 0)))
        args.append(next_mod)
        out_shape = (rows_f32, jax.ShapeDtypeStruct((m, d), BF16))
        out_specs = (row_spec, row_spec)
    return pl.pallas_call(
        functools.partial(_out_kernel, follow=follow, d_mod=d),
        out_shape=out_shape,
        grid=(m // tm,),
        in_specs=in_specs,
        out_specs=out_specs,
        input_output_aliases={3: 0} if in_place and follow != "final" else {},
        compiler_params=_cparams(("parallel",)),
        name=name,
    )(*args)


def _flash(n_chunks, scores_t, values_t, stats, bufs):
    slots = bufs[0][0].shape[0]
    for m_scr, l_scr, acc in stats:
        m_scr[...] = jnp.full(m_scr.shape, NEG_INF, F32)
        l_scr[...] = jnp.zeros(l_scr.shape, F32)
        acc[...] = jnp.zeros(acc.shape, F32)

    def load_scores(c, slot):
        for s, (st, _, _) in zip(scores_t(c), bufs):
            st[slot] = s

    def softmax(slot):
        for (m_scr, l_scr, _), (st, pt, alpha_buf) in zip(stats, bufs):
            s = st[slot]
            m_prev = m_scr[...]
            m_new = jnp.maximum(m_prev, jnp.max(s, axis=0, keepdims=True))
            alpha = jnp.exp2(m_prev - m_new)
            p = jnp.exp2(s - m_new)
            l_scr[...] = alpha * l_scr[...] + jnp.sum(p, axis=0, keepdims=True)
            m_scr[...] = m_new
            alpha_buf[slot] = alpha
            pt[slot] = p.astype(pt.dtype)

    def weighted_values(c, slot):
        vts = values_t(c)
        if not isinstance(vts, tuple):
            vts = (vts,) * len(stats)
        for vt, (_, _, acc), (_, pt, alpha_buf) in zip(vts, stats, bufs):
            acc[...] = alpha_buf[slot] * acc[...] + _dot(vt, pt[slot])

    def step(c, k, first, last):
        if not last:
            load_scores(c + 1, (k + 1) % slots)
        if not first:
            weighted_values(c - 1, (k - 1) % slots)
        softmax(k)

    load_scores(0, 0)
    step(0, 0, True, n_chunks == 1)
    iters = max(n_chunks - 2, 0) // slots

    def body(i, carry):
        for k in range(slots):
            step(1 + slots * i + k, (1 + k) % slots, False, False)
        return carry
    lax.fori_loop(0, iters, body, 0)
    for c in range(1 + slots * iters, n_chunks):
        step(c, c % slots, False, c == n_chunks - 1)
    weighted_values(n_chunks - 1, (n_chunks - 1) % slots)


def _transpose_chunks(dst, src, tk):
    def body(c, carry):
        rows = pl.ds(pl.multiple_of(c * tk, tk), tk)
        dst[c] = src[rows, :].astype(F32).T.astype(dst.dtype)
        return carry
    lax.fori_loop(0, dst.shape[0], body, 0)


def _softmax_once(s, v):
    m = jnp.max(s, axis=-1, keepdims=True)
    p = jnp.exp2(s - m)
    return _dot(p.astype(BF16), v) / jnp.sum(p, axis=-1, keepdims=True)


def _gather_rows(dst, src_ctx, src_lat, dst_cols, src_cols, ctx, step):
    dst[0:ctx, dst_cols] = src_ctx[:, src_cols]

    def copy(c, carry):
        r = pl.multiple_of(c * step, step)
        dst[pl.ds(ctx + r, step), dst_cols] = src_lat[pl.ds(r, step), src_cols]
        return carry
    lax.fori_loop(0, src_lat.shape[0] // step, copy, 0)


def _mla_kernel(q_ref, kvl_ref, kvc_ref, krl_ref, krc_ref, o_ref,
                k_scr, v_scr, vt_scr, qt_scr, *stat_and_bufs, tk, ctx, heads):
    lo, hi = slice(0, LANES), slice(LANES, 2 * LANES)
    stats = [stat_and_bufs[3 * g:3 * g + 3] for g in range(heads)]
    bufs = [stat_and_bufs[3 * heads + 3 * g:3 * heads + 3 * g + 3] for g in range(heads)]

    @pl.when(pl.program_id(2) == 0)
    def _():
        for g in range(heads):
            kn = slice(2 * g * LANES, (2 * g + 1) * LANES)
            vv = slice((2 * g + 1) * LANES, (2 * g + 2) * LANES)
            _gather_rows(k_scr.at[g], kvc_ref, kvl_ref, lo, kn, ctx, ctx)
            _gather_rows(k_scr.at[g], krc_ref, krl_ref, hi, lo, ctx, ctx)
            _gather_rows(v_scr, kvc_ref, kvl_ref, lo, vv, ctx, ctx)
            _transpose_chunks(vt_scr.at[g], v_scr, tk)

    def chunk(c):
        return pl.ds(pl.multiple_of(c * tk, tk), tk)

    for g in range(heads):
        qt_scr[g] = q_ref[:, 2 * g * LANES:(2 * g + 2) * LANES].astype(F32).T.astype(BF16)
    _flash(vt_scr.shape[1],
           lambda c: tuple(_dot(k_scr[g, chunk(c), :], qt_scr[g]) for g in range(heads)),
           lambda c: tuple(vt_scr[g, c] for g in range(heads)),
           stats, bufs)
    for g, (_, l_scr, acc) in enumerate(stats):
        o_ref[:, g * MLA_V:(g + 1) * MLA_V] = (acc[...] / l_scr[...]).T.astype(o_ref.dtype)


def _flash_buffers(tq, tk, slots):
    return [pltpu.VMEM((slots, tk, tq), F32), pltpu.VMEM((slots, tk, tq), BF16),
            pltpu.VMEM((slots, 1, tq), F32)]


def _mla_attention(q_lat, kv_lat, kv_ctx, a_lat, a_ctx, kr_blk, geo, tq, tk, heads):
    b, n, ctx = geo
    h = MLA_HEADS
    nt = n // tq
    return pl.pallas_call(
        functools.partial(_mla_kernel, tk=tk, ctx=ctx, heads=heads),
        out_shape=jax.ShapeDtypeStruct((b * n, h * MLA_V), BF16),
        grid=(b, h // heads, nt),
        in_specs=[pl.BlockSpec((tq, heads * 256), lambda bi, hi, t: (bi * nt + t, hi)),
                  pl.BlockSpec((n, heads * 256), lambda bi, hi, t: (bi, hi)),
                  pl.BlockSpec((ctx, heads * 256), lambda bi, hi, t: (bi, hi)),
                  pl.BlockSpec((n, LANES), lambda bi, hi, t: (bi, kr_blk)),
                  pl.BlockSpec((ctx, LANES), lambda bi, hi, t: (bi, kr_blk))],
        out_specs=pl.BlockSpec((tq, heads * MLA_V), lambda bi, hi, t: (bi * nt + t, hi)),
        scratch_shapes=[pltpu.VMEM((heads, ctx + n, 256), BF16),
                        pltpu.VMEM((ctx + n, MLA_V), BF16),
                        pltpu.VMEM((heads, (ctx + n) // tk, MLA_V, tk), BF16),
                        pltpu.VMEM((heads, 256, tq), BF16)]
        + [pltpu.VMEM((1, tq), F32), pltpu.VMEM((1, tq), F32),
           pltpu.VMEM((MLA_V, tq), F32)] * heads
        + _flash_buffers(tq, tk, MLA_FLASH_SLOTS) * heads,
        compiler_params=_cparams(("parallel", "arbitrary", "arbitrary")),
        name="mla_attention",
    )(q_lat, kv_lat, kv_ctx, a_lat, a_ctx)


def _mla_ctx_kernel(q_ref, kv_ref, kr_ref, o_ref):
    q = q_ref[...]
    s = _dot_nt(q[:, 0:LANES], kv_ref[:, 0:LANES]) + _dot_nt(q[:, LANES:2 * LANES], kr_ref[...])
    o_ref[...] = _softmax_once(s, kv_ref[:, LANES:2 * LANES]).astype(o_ref.dtype)


def _mla_ctx_attention(q_ctx, kv_ctx, a_ctx, kr_blk, geo):
    b, n, ctx = geo
    h = MLA_HEADS
    return pl.pallas_call(
        _mla_ctx_kernel,
        out_shape=jax.ShapeDtypeStruct((b * ctx, h * MLA_V), BF16),
        grid=(b, h),
        in_specs=[pl.BlockSpec((ctx, 256), lambda bi, hi: (bi, hi)),
                  pl.BlockSpec((ctx, 256), lambda bi, hi: (bi, hi)),
                  pl.BlockSpec((ctx, LANES), lambda bi, hi: (bi, kr_blk))],
        out_specs=pl.BlockSpec((ctx, MLA_V), lambda bi, hi: (bi, hi)),
        compiler_params=_cparams(("parallel", "arbitrary")),
        name="mla_ctx_attention",
    )(q_ctx, kv_ctx, a_ctx)


def _diff_kernel(q_ref, kl_ref, vl_ref, kc_ref, vc_ref, lam_ref, sub_ref, o_ref,
                 k_scr, v_scr, vt_scr, qt_scr, m0, l0, a0, m1, l1, a1,
                 st0, pt0, alpha0, st1, pt1, alpha1, *, tk, ctx, lam_init):
    hd = DIFF_HD
    full = slice(0, 2 * hd)

    @pl.when(pl.program_id(2) == 0)
    def _():
        _gather_rows(k_scr, kc_ref, kl_ref, full, full, ctx, ctx)
        _gather_rows(v_scr, vc_ref, vl_ref, full, full, ctx, ctx)
        _transpose_chunks(vt_scr, v_scr, tk)

    def chunk(c):
        return pl.ds(pl.multiple_of(c * tk, tk), tk)

    qt_scr[...] = q_ref[...].astype(F32).T.astype(BF16)

    def scores_t(c):
        return tuple(_dot(k_scr[chunk(c), i * hd:(i + 1) * hd], qt_scr[i * hd:(i + 1) * hd, :])
                     for i in range(2))

    _flash(vt_scr.shape[0], scores_t, lambda c: vt_scr[c],
           [(m0, l0, a0), (m1, l1, a1)], [(st0, pt0, alpha0), (st1, pt1, alpha1)])

    lv = lam_ref[...]
    lam = (jnp.exp(jnp.sum(lv[0:1] * lv[1:2], axis=-1, keepdims=True))
           - jnp.exp(jnp.sum(lv[2:3] * lv[3:4], axis=-1, keepdims=True)) + lam_init)
    o = (a0[...] / l0[...] - lam * (a1[...] / l1[...])).T
    o = o * lax.rsqrt(jnp.mean(o * o, axis=-1, keepdims=True) + EPS) * sub_ref[...]
    o_ref[...] = (o * (1.0 - lam_init)).astype(o_ref.dtype)


def _diff_attention(a_lat, kv_ctx, lam_rows, subln, geo, tq, tk, lam_init):
    b, n, ctx = geo
    h = DIFF_HEADS
    nt = n // tq
    return pl.pallas_call(
        functools.partial(_diff_kernel, tk=tk, ctx=ctx, lam_init=lam_init),
        out_shape=jax.ShapeDtypeStruct((b * n, h * 256), BF16),
        grid=(b, h, nt),
        in_specs=[pl.BlockSpec((tq, 256), lambda bi, hi, t: (bi * nt + t, h + hi)),
                  pl.BlockSpec((n, 256), lambda bi, hi, t: (bi, 2 * h + hi)),
                  pl.BlockSpec((n, 256), lambda bi, hi, t: (bi, 3 * h + hi)),
                  pl.BlockSpec((ctx, 256), lambda bi, hi, t: (bi, hi)),
                  pl.BlockSpec((ctx, 256), lambda bi, hi, t: (bi, h + hi)),
                  pl.BlockSpec((8, LANES), lambda bi, hi, t: (0, 0)),
                  pl.BlockSpec((1, 256), lambda bi, hi, t: (0, 0))],
        out_specs=pl.BlockSpec((tq, 256), lambda bi, hi, t: (bi * nt + t, hi)),
        scratch_shapes=[pltpu.VMEM((ctx + n, 256), BF16), pltpu.VMEM((ctx + n, 256), BF16),
                        pltpu.VMEM(((ctx + n) // tk, 256, tk), BF16),
                        pltpu.VMEM((256, tq), BF16)]
        + [pltpu.VMEM((1, tq), F32), pltpu.VMEM((1, tq), F32), pltpu.VMEM((256, tq), F32)] * 2
        + _flash_buffers(tq, tk, DIFF_FLASH_SLOTS) * 2,
        compiler_params=_cparams(("parallel", "arbitrary", "arbitrary")),
        name="diff_attention",
    )(a_lat, a_lat, a_lat, kv_ctx, kv_ctx, lam_rows, subln.reshape(1, 256).astype(F32))


def _half_select(x, half, axis=1):
    idx = lax.broadcasted_iota(jnp.int32, x.shape, axis)
    keep = (idx < 64) if half == 0 else (idx >= 64)
    return jnp.where(keep, x, jnp.zeros_like(x))


def _swa_kernel(*refs, is_lat, tq, band):
    if is_lat:
        sink_ref, q_ref, kl_ref, vl_ref, kc_ref, vc_ref, o_ref = refs
    else:
        sink_ref, q_ref, kc_ref, vc_ref, o_ref = refs
    t = pl.program_id(1)
    per = SWA_HEADS // SWA_KV_HEADS
    if is_lat:
        n = kl_ref.shape[0]
        start = pl.multiple_of(jnp.clip(t * tq - SWA_WINDOW, 0, n - band), SWA_WINDOW)
        kpos = start + lax.broadcasted_iota(jnp.int32, (band, per * tq), 0)
        qpos = t * tq + lax.broadcasted_iota(jnp.int32, (band, per * tq), 1) % tq
        ok = jnp.abs(qpos - kpos) <= SWA_WINDOW
    for a in range(SWA_KV_HEADS // 2):
        cols = slice(a * LANES, (a + 1) * LANES)
        qs = jnp.concatenate([q_ref[:, (a * per + j) * LANES:(a * per + j + 1) * LANES]
                              for j in range(per)], axis=0)
        kc = kc_ref[:, cols]
        vct = vc_ref[:, cols].astype(F32).T.astype(BF16)
        if is_lat:
            kb = kl_ref[pl.ds(start, band), cols]
            vbt = vl_ref[pl.ds(start, band), cols].astype(F32).T.astype(BF16)
        out_t = None
        for half in range(2):
            sink = jnp.concatenate(
                [jnp.full((1, tq), sink_ref[(a * per + j) * 2 + half] * LOG2E, F32)
                 for j in range(per)], axis=1)
            st_c = _dot_nt(_half_select(kc, half, 1), qs)
            m = jnp.maximum(jnp.max(st_c, axis=0, keepdims=True), sink)
            if is_lat:
                st_b = jnp.where(ok, _dot_nt(_half_select(kb, half, 1), qs), NEG_INF)
                m = jnp.maximum(m, jnp.max(st_b, axis=0, keepdims=True))
            pt_c = jnp.exp2(st_c - m)
            l = jnp.sum(pt_c, axis=0, keepdims=True) + jnp.exp2(sink - m)
            o_t = _dot(_half_select(vct, half, 0), pt_c.astype(BF16))
            if is_lat:
                pt_b = jnp.exp2(st_b - m)
                l = l + jnp.sum(pt_b, axis=0, keepdims=True)
                o_t = o_t + _dot(_half_select(vbt, half, 0), pt_b.astype(BF16))
            o_t = o_t / l
            out_t = o_t if out_t is None else out_t + o_t
        out = out_t.T
        for j in range(per):
            o_ref[:, (a * per + j) * LANES:(a * per + j + 1) * LANES] = (
                out[j * tq:(j + 1) * tq].astype(o_ref.dtype))


def _swa_attention(a_q, a_lat, a_ctx, sink_perm, geo, tq, is_lat):
    b, n, ctx = geo
    width = SWA_HEADS * SWA_HD
    kvw = SWA_KV_HEADS * SWA_HD
    rows = n if is_lat else ctx
    nt = rows // tq
    kblk, vblk = 2 * width // kvw, 2 * width // kvw + 1
    in_specs = [pl.BlockSpec(memory_space=pltpu.SMEM),
                pl.BlockSpec((tq, width), lambda bi, t: (bi * nt + t, 1))]
    args = [sink_perm, a_q]
    if is_lat:
        in_specs += [pl.BlockSpec((n, kvw), lambda bi, t: (bi, kblk)),
                     pl.BlockSpec((n, kvw), lambda bi, t: (bi, vblk))]
        args += [a_lat, a_lat]
    in_specs += [pl.BlockSpec((ctx, kvw), lambda bi, t: (bi, kblk)),
                 pl.BlockSpec((ctx, kvw), lambda bi, t: (bi, vblk))]
    args += [a_ctx, a_ctx]
    return pl.pallas_call(
        functools.partial(_swa_kernel, is_lat=is_lat, tq=tq, band=tq + 2 * SWA_WINDOW),
        out_shape=jax.ShapeDtypeStruct((b * rows, width), BF16),
        grid=(b, nt),
        in_specs=in_specs,
        out_specs=pl.BlockSpec((tq, width), lambda bi, t: (bi * nt + t, 0)),
        compiler_params=_cparams(("parallel", "arbitrary")),
        name="swa_attention" if is_lat else "swa_ctx_attention",
    )(*args)


def _na_kernel(*refs, is_lat, rq, rk, rows):
    if is_lat:
        q_ref, kl_ref, vl_ref, kc_ref, vc_ref, bt_ref, o_ref, s_scr, p_scr = refs
    else:
        q_ref, kc_ref, vc_ref, o_ref = refs
    t = pl.program_id(2)
    w = GRID_W
    band = rk * w
    q = q_ref[...]
    kc = kc_ref[...]
    vct = vc_ref[...].astype(F32).T.astype(BF16)
    if is_lat:
        r0 = t * rq
        srow = jnp.clip(r0 - NA_WIN_R // 2, 0, rows - rk)
        start = pl.multiple_of(srow * w, w)
        kb = kl_ref[pl.ds(start, band), :]
        vbt = vl_ref[pl.ds(start, band), :].astype(F32).T.astype(BF16)
        lo = lax.broadcasted_iota(jnp.int32, (w, LANES), 1) < w

    def bias_and_mask(s_half, half):
        for kj in range(rk):
            for qp in range(rq // 2):
                blk = (slice(kj * w, (kj + 1) * w), slice(qp * LANES, (qp + 1) * LANES))
                kr = srow + kj
                ra = r0 + 2 * qp
                rsa = jnp.clip(ra - NA_WIN_R // 2, 0, rows - NA_WIN_R)
                rsb = jnp.clip(ra + 1 - NA_WIN_R // 2, 0, rows - NA_WIN_R)
                e = jnp.clip(kr - ra + NA_WIN_R - 1, 0, 2 * NA_WIN_R - 1)
                pen0 = jnp.where((kr >= rsa) & (kr < rsa + NA_WIN_R), 0.0, NEG_INF)
                pen1 = jnp.where((kr >= rsb) & (kr < rsb + NA_WIN_R), 0.0, NEG_INF)
                s_half[blk] = s_half[blk] + (bt_ref[half, e] * LOG2E + jnp.where(lo, pen0, pen1))

    def attend():
        out_t = None
        for half in range(2):
            st_c = _dot_nt(_half_select(kc, half, 1), q)
            m = jnp.max(st_c, axis=0, keepdims=True)
            if is_lat:
                s_half = s_scr.at[half]
                s_half[...] = _dot_nt(_half_select(kb, half, 1), q)
                bias_and_mask(s_half, half)
                st_b = s_half[...]
                m = jnp.maximum(m, jnp.max(st_b, axis=0, keepdims=True))
            pt_c = jnp.exp2(st_c - m)
            l = jnp.sum(pt_c, axis=0, keepdims=True)
            o_t = _dot(_half_select(vct, half, 0), pt_c.astype(BF16))
            if is_lat:
                pt_b = jnp.exp2(st_b - m)
                l = l + jnp.sum(pt_b, axis=0, keepdims=True)
                o_t = o_t + _dot(_half_select(vbt, half, 0), pt_b.astype(BF16))
            o_t = o_t / l
            out_t = o_t if out_t is None else out_t + o_t
        o_ref[...] = out_t.T.astype(o_ref.dtype)

    def attend_interior():
        nv = NA_WIN_R + 1
        out_t = None
        for half in range(2):
            st_c = _dot_nt(_half_select(kc, half, 1), q)
            m_c = jnp.max(st_c, axis=0, keepdims=True)
            s_half, p_half = s_scr.at[half], p_scr.at[half]
            s_half[...] = _dot_nt(_half_select(kb, half, 1), q)
            m_parts, l_parts = [], []
            for qp in range(rq // 2):
                strip = slice(qp * LANES, (qp + 1) * LANES)
                k0 = 2 * qp
                blocks = []
                for dk in range(nv):
                    bias = bt_ref[half, dk + NA_WIN_R // 2 - 1] * LOG2E
                    if dk == 0:
                        bias = jnp.where(lo, bias, NEG_INF)
                    elif dk == NA_WIN_R:
                        bias = jnp.where(lo, NEG_INF, bias)
                    blocks.append(s_half[(k0 + dk) * w:(k0 + dk + 1) * w, strip] + bias)
                s_v = jnp.concatenate(blocks, axis=0)
                m_s = jnp.maximum(m_c[:, strip], jnp.max(s_v, axis=0, keepdims=True))
                p_v = jnp.exp2(s_v - m_s)
                m_parts.append(m_s)
                l_parts.append(jnp.sum(p_v, axis=0, keepdims=True))
                p_half[k0 * w:(k0 + nv) * w, strip] = p_v.astype(BF16)
                if k0 > 0:
                    p_half[0:k0 * w, strip] = jnp.zeros((k0 * w, LANES), BF16)
                if k0 + nv < rk:
                    p_half[(k0 + nv) * w:rk * w, strip] = jnp.zeros(((rk - k0 - nv) * w, LANES), BF16)
            m = jnp.concatenate(m_parts, axis=1)
            pt_c = jnp.exp2(st_c - m)
            l = jnp.sum(pt_c, axis=0, keepdims=True) + jnp.concatenate(l_parts, axis=1)
            o_t = (_dot(_half_select(vct, half, 0), pt_c.astype(BF16))
                   + _dot(_half_select(vbt, half, 0), p_half[...]))
            o_t = o_t / l
            out_t = o_t if out_t is None else out_t + o_t
        o_ref[...] = out_t.T.astype(o_ref.dtype)

    if is_lat:
        interior = (r0 >= NA_WIN_R // 2) & (r0 <= rows - rq - NA_WIN_R // 2)
        pl.when(interior)(attend_interior)
        pl.when(jnp.logical_not(interior))(attend)
    else:
        attend()


def _na_attention(a_q, a_lat, a_ctx, bias_tab, geo, rq, is_lat):
    b, n, ctx = geo
    width = NA_HEADS * NA_HD
    pairs = width // LANES
    tq = rq * GRID_W
    rk = rq + NA_WIN_R
    grid_rows = n // GRID_W
    rows = n if is_lat else ctx
    assert grid_rows >= rk and rows % tq == 0
    nt = rows // tq
    qo, ko, vo = pairs, 2 * pairs, 3 * pairs
    in_specs = [pl.BlockSpec((tq, LANES), lambda bi, pi, t: (bi * nt + t, qo + pi))]
    args = [a_q]
    if is_lat:
        in_specs += [pl.BlockSpec((n, LANES), lambda bi, pi, t: (bi, ko + pi)),
                     pl.BlockSpec((n, LANES), lambda bi, pi, t: (bi, vo + pi))]
        args += [a_lat, a_lat]
    in_specs += [pl.BlockSpec((ctx, LANES), lambda bi, pi, t: (bi, ko + pi)),
                 pl.BlockSpec((ctx, LANES), lambda bi, pi, t: (bi, vo + pi))]
    args += [a_ctx, a_ctx]
    scratch = []
    if is_lat:
        in_specs.append(pl.BlockSpec((2, 2 * NA_WIN_R, GRID_W, LANES),
                                     lambda bi, pi, t: (pi, 0, 0, 0)))
        args.append(bias_tab)
        scratch += [pltpu.VMEM((2, rk * GRID_W, tq), F32),
                    pltpu.VMEM((2, rk * GRID_W, tq), BF16)]
    return pl.pallas_call(
        functools.partial(_na_kernel, is_lat=is_lat, rq=rq, rk=rk, rows=grid_rows),
        out_shape=jax.ShapeDtypeStruct((b * rows, width), BF16),
        grid=(b, pairs, nt),
        in_specs=in_specs,
        out_specs=pl.BlockSpec((tq, LANES), lambda bi, pi, t: (bi * nt + t, pi)),
        scratch_shapes=scratch,
        compiler_params=_cparams(("parallel", "arbitrary", "arbitrary")),
        name="na_attention" if is_lat else "na_ctx_attention",
    )(*args)


def _rope_tables(n, d_rot):
    t = jnp.arange(n)
    row = (t // GRID_W).astype(F32)
    col = (t % GRID_W).astype(F32)
    d_axis = d_rot // 2
    inv = ROPE_BASE ** (-jnp.arange(0, d_axis, 2, dtype=F32) / d_axis)
    ang = jnp.concatenate([row[:, None] * inv, col[:, None] * inv], axis=-1)
    cos, sin = jnp.cos(ang), jnp.sin(ang)
    reps = LANES // d_rot
    return (jnp.tile(cos, (1, 2 * reps)),
            jnp.tile(jnp.concatenate([-sin, sin], axis=-1), (1, reps)))


def _na_bias_table(rpb):
    col = jnp.arange(GRID_W)
    dc = jnp.clip(col[:, None] - col[None, :], -(NA_WIN_C - 1), NA_WIN_C - 1) + (NA_WIN_C - 1)
    cs = jnp.clip(col - NA_WIN_C // 2, 0, GRID_W - NA_WIN_C)[None, :]
    col_ok = (col[:, None] >= cs) & (col[:, None] < cs + NA_WIN_C)
    bt = jnp.where(col_ok, rpb[:, :, dc], NEG_INF)
    bt = jnp.pad(bt, ((0, 0), (1, 1), (0, 0), (0, 0)))
    return jnp.concatenate([bt[:, 1:], bt[:, :-1]], axis=-1).astype(F32)


def _tile_plan(b, n, n_ctx, d):
    tm = _largest_divisor(math.gcd(n, b * n_ctx), (512, 256, 128))
    return dict(
        tm=tm,
        tq_flash=_largest_divisor(n, (512, 256, 128)),
        tq_diff=_largest_divisor(n, (512, 256, 128)),
        tk=_largest_divisor(n + n_ctx, (768, 640, 512, 384, 256, 128)),
        tk_mla=_largest_divisor(n + n_ctx, (768, 640, 512, 384, 256, 128)),
        heads_mla=2,
        tq_swa=128,
        rq_na=8,
        rq_na_ctx=4,
    )


def kernel(x, c, ctx, c_ctx, l0_mod_w, l0_mod_b, l0_norm, l0_w_in, l0_q_norm, l0_w_qb, l0_kv_norm, l0_w_kvb, l0_w_out, l1_mod_w, l1_mod_b, l1_norm, l1_w_in, l1_sink, l1_w_out, l2_mod_w, l2_mod_b, l2_norm, l2_w_in, l2_rpb, l2_w_out, l3_mod_w, l3_mod_b, l3_norm, l3_w_in, l3_lam_q1, l3_lam_k1, l3_lam_q2, l3_lam_k2, l3_subln, l3_w_out, final_norm):
    b, n, d = x.shape
    n_ctx = ctx.shape[1]
    geo = (b, n, n_ctx)
    assert b < 8 and n % GRID_W == 0
    tp = _tile_plan(b, n, n_ctx, d)
    tm = tp["tm"]
    lat, cx = ("batch", n // tm), ("const", b)

    x_lat = x.reshape(b * n, d)
    x_ctx = ctx.reshape(b * n_ctx, d)
    c_rows = jnp.concatenate([c, c_ctx[None, :], jnp.zeros((8 - b - 1, d), F32)], axis=0)
    rope64 = _rope_tables(n, 64) + (n // tm,)
    rope128 = _rope_tables(n, 128) + (n // tm,)

    def groups(*spans):
        out = []
        for cols, code, scale in spans:
            out += [(code, scale)] * (cols // LANES)
        return tuple(out)

    def both(fn):
        return fn(x_lat, lat, True), fn(x_ctx, cx, False)

    def project_normed(h_lat, h_ctx, w, rope, grp_cfg, tn, name):
        return (_projection(h_lat, 0, d, None, None, None, w, rope, grp_cfg, tm, tn, name),
                _projection(h_ctx, 0, d, None, None, None, w, None, grp_cfg, tm, tn, name + "_ctx"))

    mods = [_modulation(c_rows, w, bias) for w, bias in
            ((l0_mod_w, l0_mod_b), (l1_mod_w, l1_mod_b), (l2_mod_w, l2_mod_b), (l3_mod_w, l3_mod_b))]

    mod = mods[0]
    r = MLA_RANK
    w_in = jnp.concatenate([l0_w_in[:, 2 * r + MLA_ROPE:], l0_w_in[:, :2 * r + MLA_ROPE],
                            jnp.zeros((d, 2 * LANES - MLA_ROPE), F32)], axis=1).astype(BF16)
    width = MLA_HEADS * MLA_V
    g0 = groups((width + 2 * r, PLAIN, 1.0), (LANES, ROPE64, 1.0), (LANES, PLAIN, 1.0))
    a_lat, a_ctx = both(lambda xs, grp, is_lat: _projection(
        xs, 0, d, l0_norm, mod, grp, w_in, rope64 if is_lat else None, g0, tm, 13 * LANES,
        "mla_in"))
    qk = MLA_NOPE + MLA_ROPE
    w_q = l0_w_qb.reshape(r, MLA_HEADS, qk)
    w_q = jnp.pad(w_q, ((0, 0), (0, 0), (0, 256 - qk))).reshape(r, MLA_HEADS * 256).astype(BF16)
    q_scale = qk ** -0.5 * LOG2E
    gq = groups((LANES, PLAIN, q_scale), (LANES, ROPE64, q_scale)) * MLA_HEADS
    gkv = groups((MLA_HEADS * 256, PLAIN, 1.0))
    w_kv = l0_w_kvb.astype(BF16)
    q_lat, kv_lat = (
        _projection(a_lat, width // r, r, l0_q_norm, None, None, w_q, rope64, gq, tm, 4096, "mla_q"),
        _projection(a_lat, width // r + 1, r, l0_kv_norm, None, None, w_kv, None, gkv, tm, 4096,
                    "mla_kv"))
    q_ctx, kv_ctx = (
        _projection(a_ctx, width // r, r, l0_q_norm, None, None, w_q, None, gq, tm, 4096,
                    "mla_q_ctx"),
        _projection(a_ctx, width // r + 1, r, l0_kv_norm, None, None, w_kv, None, gkv, tm, 4096,
                    "mla_kv_ctx"))
    kr_blk = (width + 2 * r) // LANES
    o_lat = _mla_attention(q_lat, kv_lat, kv_ctx, a_lat, a_ctx, kr_blk, geo, tp["tq_flash"],
                           tp["tk_mla"], tp["heads_mla"])
    o_ctx = _mla_ctx_attention(q_ctx, kv_ctx, a_ctx, kr_blk, geo)
    w_out = l0_w_out.astype(BF16)
    x_lat, h_lat = _out_projection(o_lat, a_lat, w_out, x_lat, mod, lat, tm, "mla_out", False,
                                   l1_norm, mods[1])
    x_ctx, h_ctx = _out_projection(o_ctx, a_ctx, w_out, x_ctx, mod, cx, tm, "mla_out", False,
                                   l1_norm, mods[1])

    mod = mods[1]
    per = SWA_HEADS // SWA_KV_HEADS
    pairs_kv = SWA_KV_HEADS // 2

    def pair_heads(w, axis):
        shape = w.shape
        w = w.reshape(shape[:axis] + (pairs_kv, 2, per, SWA_HD) + shape[axis + 1:])
        return jnp.swapaxes(w, axis + 1, axis + 2).reshape(shape)

    width = SWA_HEADS * SWA_HD
    kvw = SWA_KV_HEADS * SWA_HD
    w_q, w_k, w_v, w_g = (l1_w_in[:, :width], l1_w_in[:, width:width + kvw],
                          l1_w_in[:, width + kvw:width + 2 * kvw], l1_w_in[:, width + 2 * kvw:])
    w_in = jnp.concatenate([pair_heads(w_g, 1), pair_heads(w_q, 1), w_k, w_v],
                           axis=1).astype(BF16)
    g1 = groups((width, PLAIN, 1.0), (width, ROPE64, SWA_HD ** -0.5 * LOG2E),
                (kvw, ROPE64, 1.0), (kvw, PLAIN, 1.0))
    a_lat, a_ctx = project_normed(h_lat, h_ctx, w_in, rope64, g1, 2304, "swa_in")
    sink_perm = jnp.swapaxes(l1_sink.reshape(pairs_kv, 2, per), 1, 2).reshape(SWA_HEADS)
    tq = tp["tq_swa"]
    o_lat = _swa_attention(a_lat, a_lat, a_ctx, sink_perm, geo, tq, True)
    o_ctx = _swa_attention(a_ctx, a_lat, a_ctx, sink_perm, geo, tq, False)
    w_out = pair_heads(l1_w_out, 0).astype(BF16)
    x_lat, h_lat = _out_projection(o_lat, a_lat, w_out, x_lat, mod, lat, tm, "swa_out", True,
                                   l2_norm, mods[2])
    x_ctx, h_ctx = _out_projection(o_ctx, a_ctx, w_out, x_ctx, mod, cx, tm, "swa_out", True,
                                   l2_norm, mods[2])

    mod = mods[2]
    width = NA_HEADS * NA_HD
    w_in = jnp.concatenate([l2_w_in[:, 3 * width:], l2_w_in[:, :3 * width]], axis=1).astype(BF16)
    g2 = groups((width, PLAIN, 1.0), (width, PLAIN, NA_HD ** -0.5 * LOG2E), (2 * width, PLAIN, 1.0))
    a_lat, a_ctx = project_normed(h_lat, h_ctx, w_in, None, g2, 2048, "na_in")
    bias_tab = _na_bias_table(l2_rpb)
    o_lat = _na_attention(a_lat, a_lat, a_ctx, bias_tab, geo, tp["rq_na"], True)
    o_ctx = _na_attention(a_ctx, a_lat, a_ctx, bias_tab, geo, tp["rq_na_ctx"], False)
    w_out = l2_w_out.astype(BF16)
    x_lat, h_lat = _out_projection(o_lat, a_lat, w_out, x_lat, mod, lat, tm, "na_out", True,
                                   l3_norm, mods[3])
    _, h_ctx = _out_projection(o_ctx, a_ctx, w_out, x_ctx, mod, cx, tm, "na_out", True,
                               l3_norm, mods[3])

    mod = mods[3]
    width = DIFF_HEADS * 2 * DIFF_HD
    w_in = jnp.concatenate([l3_w_in[:, 3 * width:], l3_w_in[:, :3 * width]], axis=1).astype(BF16)
    g3 = groups((width, PLAIN, 1.0), (width, ROPE128, DIFF_HD ** -0.5 * LOG2E),
                (width, ROPE128, 1.0), (width, PLAIN, 1.0))
    a_lat = _projection(h_lat, 0, d, None, None, None, w_in, rope128, g3, tm, 2048, "diff_in")
    kv_ctx = _projection(h_ctx, 0, d, None, None, None, w_in[:, 2 * width:], None,
                         g3[2 * width // LANES:], tm, 1024, "diff_in_ctx")
    lam_init = 0.8 - 0.6 * math.exp(-0.3 * 3)
    lam_rows = jnp.concatenate([jnp.stack([l3_lam_q1, l3_lam_k1, l3_lam_q2, l3_lam_k2]),
                                jnp.zeros((4, DIFF_HD), F32)], axis=0)
    o_lat = _diff_attention(a_lat, kv_ctx, lam_rows, l3_subln, geo, tp["tq_diff"], tp["tk"],
                            lam_init)
    out = _out_projection(o_lat, a_lat, l3_w_out.astype(BF16), x_lat, mod, lat, tm, "diff_out",
                          False, final_norm)
    return out.reshape(b, n, d)
```

```python
import functools
import math

import jax
import jax.numpy as jnp
from jax import lax
from jax.experimental import pallas as pl
from jax.experimental.pallas import tpu as pltpu

F32 = jnp.float32
BF16 = jnp.bfloat16

LANES = 128
LOG2E = 1.4426950408889634
NEG_INF = -1e30
EPS = 1e-6
ROPE_BASE = 10000.0
GRID_W = 64
VMEM_LIMIT = 56 * 1024 * 1024

MLA_HEADS, MLA_NOPE, MLA_ROPE, MLA_V, MLA_RANK = 16, 128, 64, 128, 512
SWA_HEADS, SWA_KV_HEADS, SWA_HD, SWA_WINDOW = 32, 4, 64, 128
NA_HEADS, NA_HD, NA_WIN_R, NA_WIN_C = 32, 64, 8, 16
DIFF_HEADS, DIFF_HD = 8, 128

PLAIN, ROPE64, ROPE128 = 0, 1, 2
MLA_FLASH_SLOTS, DIFF_FLASH_SLOTS = 2, 2


def _cparams(sem):
    return pltpu.CompilerParams(dimension_semantics=sem, vmem_limit_bytes=VMEM_LIMIT)


def _dot_nt(a, b):
    return lax.dot_general(a, b, (((1,), (1,)), ((), ())), preferred_element_type=F32)


def _dot(a, b):
    return jnp.dot(a, b, preferred_element_type=F32)


def _largest_divisor(total, candidates):
    return next(c for c in candidates if total % c == 0)


def _mod_kernel(c_ref, w_ref, b_ref, o_ref):
    c = c_ref[...]
    s = c * jax.nn.sigmoid(c)
    o_ref[...] = _dot(s, w_ref[...]) + b_ref[...]


def _modulation(c_rows, mod_w, mod_b):
    r, d = c_rows.shape
    n = mod_w.shape[1]
    tn = math.gcd(n, 512)
    out = pl.pallas_call(
        _mod_kernel,
        out_shape=jax.ShapeDtypeStruct((r, n), F32),
        grid=(n // tn,),
        in_specs=[pl.BlockSpec((r, d), lambda j: (0, 0)),
                  pl.BlockSpec((d, tn), lambda j: (0, j)),
                  pl.BlockSpec((1, tn), lambda j: (0, j))],
        out_specs=pl.BlockSpec((r, tn), lambda j: (0, j)),
        compiler_params=_cparams(("arbitrary",)),
        name="modulation",
    )(c_rows, mod_w, mod_b.reshape(1, n))
    return out.reshape(r, 1, n)


def _mod_row(group):
    kind, val = group
    return (lambda i: i // val) if kind == "batch" else (lambda i: val)


def _rope_group(a, cos, sin, code):
    if code == ROPE64:
        lane = lax.broadcasted_iota(jnp.int32, a.shape, 1)
        first = (lane % 64) < 32
        partner = jnp.where(first, pltpu.roll(a, 96, 1), pltpu.roll(a, 32, 1))
    else:
        partner = pltpu.roll(a, 64, 1)
    return a * cos + partner * sin


def _proj_kernel(*refs, has_norm, has_mod, has_rope, tile_patterns, d_mod):
    it = iter(refs)
    x_ref = next(it)
    g_ref = next(it) if has_norm else None
    mod_ref = next(it) if has_mod else None
    w_ref = next(it)
    cos_ref, sin_ref = (next(it), next(it)) if has_rope else (None, None)
    o_ref = next(it)
    h_ref = next(it) if has_norm else x_ref
    j = pl.program_id(1)

    if has_norm:
        @pl.when(j == 0)
        def _():
            x = x_ref[...].astype(F32)
            y = x * lax.rsqrt(jnp.mean(x * x, axis=-1, keepdims=True) + EPS) * g_ref[...]
            if has_mod:
                shift = mod_ref[0, :, 0:d_mod]
                scale = mod_ref[0, :, d_mod:2 * d_mod]
                y = y * (1.0 + scale) + shift
            h_ref[...] = y.astype(BF16)

    def epilogue(pattern):
        acc = _dot(h_ref[...], w_ref[...])
        for gi, (code, scale) in enumerate(pattern):
            a = acc[:, gi * LANES:(gi + 1) * LANES]
            if code != PLAIN:
                a = _rope_group(a, cos_ref[...], sin_ref[...], code)
            if scale != 1.0:
                a = a * scale
            o_ref[:, gi * LANES:(gi + 1) * LANES] = a.astype(o_ref.dtype)

    distinct = sorted(set(tile_patterns), key=tile_patterns.index)
    if len(distinct) == 1:
        epilogue(distinct[0])
    else:
        for pat in distinct:
            tiles = [jj for jj, p in enumerate(tile_patterns) if p == pat]
            cond = functools.reduce(jnp.logical_or, [j == jj for jj in tiles])
            pl.when(cond)(functools.partial(epilogue, pat))


def _projection(x, col_blk, kd, gain, mod, group, w, rope, groups, tm, tn, name):
    m = x.shape[0]
    n_out = w.shape[1]
    tm = min(tm, m)
    if group is not None and group[0] == "const":
        name += "_ctx"
    assert m % tm == 0 and n_out % tn == 0 and tn % LANES == 0 and len(groups) == n_out // LANES
    if rope is None:
        groups = tuple((PLAIN, scale) for _, scale in groups)
    has_norm = gain is not None
    has_mod = mod is not None
    assert has_norm or (not has_mod and x.dtype == BF16)
    has_rope = any(code != PLAIN for code, _ in groups)
    gpt = tn // LANES
    tile_patterns = tuple(tuple(groups[t * gpt:(t + 1) * gpt]) for t in range(n_out // tn))

    in_specs = [pl.BlockSpec((tm, kd), lambda i, j: (i, col_blk))]
    args = [x]
    if has_norm:
        in_specs.append(pl.BlockSpec((1, kd), lambda i, j: (0, 0)))
        args.append(gain.reshape(1, kd).astype(F32))
    d_mod = 0
    if has_mod:
        d_mod = mod.shape[2] // 3
        row = _mod_row(group)
        in_specs.append(pl.BlockSpec((1, 1, 3 * d_mod), lambda i, j: (row(i), 0, 0)))
        args.append(mod)
    in_specs.append(pl.BlockSpec((kd, tn), lambda i, j: (0, j)))
    args.append(w)
    if has_rope:
        cos, sin, tiles_per_batch = rope
        in_specs += [pl.BlockSpec((tm, LANES), lambda i, j: (i % tiles_per_batch, 0))] * 2
        args += [cos, sin]
    kern = functools.partial(_proj_kernel, has_norm=has_norm, has_mod=has_mod, has_rope=has_rope,
                             tile_patterns=tile_patterns, d_mod=d_mod)
    return pl.pallas_call(
        kern,
        out_shape=jax.ShapeDtypeStruct((m, n_out), BF16),
        grid=(m // tm, n_out // tn),
        in_specs=in_specs,
        out_specs=pl.BlockSpec((tm, tn), lambda i, j: (i, j)),
        scratch_shapes=[pltpu.VMEM((tm, kd), BF16)] if has_norm else [],
        compiler_params=_cparams(("parallel", "arbitrary")),
        name=name,
    )(*args)


def _out_kernel(*refs, follow, d_mod):
    o_ref, gate_ref, w_ref, x_ref, gm_ref = refs[:5]
    g = gate_ref[...].astype(F32)
    u = (o_ref[...].astype(F32) * (g * jax.nn.sigmoid(g))).astype(BF16)
    y = x_ref[...] + gm_ref[0] * _dot(u, w_ref[...])
    if follow is None:
        refs[5][...] = y
        return
    yn = y * lax.rsqrt(jnp.mean(y * y, axis=-1, keepdims=True) + EPS) * refs[5][...]
    if follow == "final":
        refs[6][...] = yn
    else:
        nmod_ref, y_ref, h_ref = refs[6:9]
        y_ref[...] = y
        h_ref[...] = (yn * (1.0 + nmod_ref[0, :, d_mod:2 * d_mod])
                      + nmod_ref[0, :, 0:d_mod]).astype(h_ref.dtype)


def _out_projection(o, proj, w_out, x, mod, group, tm, name, in_place=True,
                    next_gain=None, next_mod=None):
    m, d = x.shape
    width = w_out.shape[0]
    tm = min(tm, m)
    if group[0] == "const":
        name += "_ctx"
    row = _mod_row(group)
    follow = None if next_gain is None else ("final" if next_mod is None else "layer")
    in_specs = [pl.BlockSpec((tm, width), lambda i: (i, 0)),
                pl.BlockSpec((tm, width), lambda i: (i, 0)),
                pl.BlockSpec((width, d), lambda i: (0, 0)),
                pl.BlockSpec((tm, d), lambda i: (i, 0)),
                pl.BlockSpec((1, 1, d), lambda i: (row(i), 0, 2))]
    args = [o, proj, w_out, x, mod]
    rows_f32 = jax.ShapeDtypeStruct((m, d), F32)
    row_spec = pl.BlockSpec((tm, d), lambda i: (i, 0))
    out_shape, out_specs = rows_f32, row_spec
    if follow is not None:
        in_specs.append(pl.BlockSpec((1, d), lambda i: (0, 0)))
        args.append(next_gain.reshape(1, d).astype(F32))
    if follow == "layer":
        in_specs.append(pl.BlockSpec((1, 1, 3 * d), lambda i: (row(i), 0, 0)))
        args.append(next_mod)
        out_shape = (rows_f32, jax.ShapeDtypeStruct((m, d), BF16))
        out_specs = (row_spec, row_spec)
    return pl.pallas_call(
        functools.partial(_out_kernel, follow=follow, d_mod=d),
        out_shape=out_shape,
        grid=(m // tm,),
        in_specs=in_specs,
        out_specs=out_specs,
        input_output_aliases={3: 0} if in_place and follow != "final" else {},
        compiler_params=_cparams(("parallel",)),
        name=name,
    )(*args)


def _flash(n_chunks, scores_t, values_t, stats, bufs):
    slots = bufs[0][0].shape[0]
    for m_scr, l_scr, acc in stats:
        m_scr[...] = jnp.full(m_scr.shape, NEG_INF, F32)
        l_scr[...] = jnp.zeros(l_scr.shape, F32)
        acc[...] = jnp.zeros(acc.shape, F32)

    def load_scores(c, slot):
        for s, (st, _, _) in zip(scores_t(c), bufs):
            st[slot] = s

    def softmax(slot):
        for (m_scr, l_scr, _), (st, pt, alpha_buf) in zip(stats, bufs):
            s = st[slot]
            m_prev = m_scr[...]
            m_new = jnp.maximum(m_prev, jnp.max(s, axis=0, keepdims=True))
            alpha = jnp.exp2(m_prev - m_new)
            p = jnp.exp2(s - m_new)
            l_scr[...] = alpha * l_scr[...] + jnp.sum(p, axis=0, keepdims=True)
            m_scr[...] = m_new
            alpha_buf[slot] = alpha
            pt[slot] = p.astype(pt.dtype)

    def weighted_values(c, slot):
        vts = values_t(c)
        if not isinstance(vts, tuple):
            vts = (vts,) * len(stats)
        for vt, (_, _, acc), (_, pt, alpha_buf) in zip(vts, stats, bufs):
            acc[...] = alpha_buf[slot] * acc[...] + _dot(vt, pt[slot])

    def step(c, k, first, last):
        if not last:
            load_scores(c + 1, (k + 1) % slots)
        if not first:
            weighted_values(c - 1, (k - 1) % slots)
        softmax(k)

    load_scores(0, 0)
    step(0, 0, True, n_chunks == 1)
    iters = max(n_chunks - 2, 0) // slots

    def body(i, carry):
        for k in range(slots):
            step(1 + slots * i + k, (1 + k) % slots, False, False)
        return carry
    lax.fori_loop(0, iters, body, 0)
    for c in range(1 + slots * iters, n_chunks):
        step(c, c % slots, False, c == n_chunks - 1)
    weighted_values(n_chunks - 1, (n_chunks - 1) % slots)


def _transpose_chunks(dst, src, tk):
    def body(c, carry):
        rows = pl.ds(pl.multiple_of(c * tk, tk), tk)
        dst[c] = src[rows, :].astype(F32).T.astype(dst.dtype)
        return carry
    lax.fori_loop(0, dst.shape[0], body, 0)


def _softmax_once(s, v):
    m = jnp.max(s, axis=-1, keepdims=True)
    p = jnp.exp2(s - m)
    return _dot(p.astype(BF16), v) / jnp.sum(p, axis=-1, keepdims=True)


def _gather_rows(dst, src_ctx, src_lat, dst_cols, src_cols, ctx, step):
    dst[0:ctx, dst_cols] = src_ctx[:, src_cols]

    def copy(c, carry):
        r = pl.multiple_of(c * step, step)
        dst[pl.ds(ctx + r, step), dst_cols] = src_lat[pl.ds(r, step), src_cols]
        return carry
    lax.fori_loop(0, src_lat.shape[0] // step, copy, 0)


def _mla_kernel(q_ref, kvl_ref, kvc_ref, krl_ref, krc_ref, o_ref,
                k_scr, v_scr, vt_scr, qt_scr, *stat_and_bufs, tk, ctx, heads):
    lo, hi = slice(0, LANES), slice(LANES, 2 * LANES)
    stats = [stat_and_bufs[3 * g:3 * g + 3] for g in range(heads)]
    bufs = [stat_and_bufs[3 * heads + 3 * g:3 * heads + 3 * g + 3] for g in range(heads)]

    @pl.when(pl.program_id(2) == 0)
    def _():
        for g in range(heads):
            kn = slice(2 * g * LANES, (2 * g + 1) * LANES)
            vv = slice((2 * g + 1) * LANES, (2 * g + 2) * LANES)
            _gather_rows(k_scr.at[g], kvc_ref, kvl_ref, lo, kn, ctx, ctx)
            _gather_rows(k_scr.at[g], krc_ref, krl_ref, hi, lo, ctx, ctx)
            _gather_rows(v_scr, kvc_ref, kvl_ref, lo, vv, ctx, ctx)
            _transpose_chunks(vt_scr.at[g], v_scr, tk)

    def chunk(c):
        return pl.ds(pl.multiple_of(c * tk, tk), tk)

    for g in range(heads):
        qt_scr[g] = q_ref[:, 2 * g * LANES:(2 * g + 2) * LANES].astype(F32).T.astype(BF16)
    _flash(vt_scr.shape[1],
           lambda c: tuple(_dot(k_scr[g, chunk(c), :], qt_scr[g]) for g in range(heads)),
           lambda c: tuple(vt_scr[g, c] for g in range(heads)),
           stats, bufs)
    for g, (_, l_scr, acc) in enumerate(stats):
        o_ref[:, g * MLA_V:(g + 1) * MLA_V] = (acc[...] / l_scr[...]).T.astype(o_ref.dtype)


def _flash_buffers(tq, tk, slots):
    return [pltpu.VMEM((slots, tk, tq), F32), pltpu.VMEM((slots, tk, tq), BF16),
            pltpu.VMEM((slots, 1, tq), F32)]


def _mla_attention(q_lat, kv_lat, kv_ctx, a_lat, a_ctx, kr_blk, geo, tq, tk, heads):
    b, n, ctx = geo
    h = MLA_HEADS
    nt = n // tq
    return pl.pallas_call(
        functools.partial(_mla_kernel, tk=tk, ctx=ctx, heads=heads),
        out_shape=jax.ShapeDtypeStruct((b * n, h * MLA_V), BF16),
        grid=(b, h // heads, nt),
        in_specs=[pl.BlockSpec((tq, heads * 256), lambda bi, hi, t: (bi * nt + t, hi)),
                  pl.BlockSpec((n, heads * 256), lambda bi, hi, t: (bi, hi)),
                  pl.BlockSpec((ctx, heads * 256), lambda bi, hi, t: (bi, hi)),
                  pl.BlockSpec((n, LANES), lambda bi, hi, t: (bi, kr_blk)),
                  pl.BlockSpec((ctx, LANES), lambda bi, hi, t: (bi, kr_blk))],
        out_specs=pl.BlockSpec((tq, heads * MLA_V), lambda bi, hi, t: (bi * nt + t, hi)),
        scratch_shapes=[pltpu.VMEM((heads, ctx + n, 256), BF16),
                        pltpu.VMEM((ctx + n, MLA_V), BF16),
                        pltpu.VMEM((heads, (ctx + n) // tk, MLA_V, tk), BF16),
                        pltpu.VMEM((heads, 256, tq), BF16)]
        + [pltpu.VMEM((1, tq), F32), pltpu.VMEM((1, tq), F32),
           pltpu.VMEM((MLA_V, tq), F32)] * heads
        + _flash_buffers(tq, tk, MLA_FLASH_SLOTS) * heads,
        compiler_params=_cparams(("parallel", "arbitrary", "arbitrary")),
        name="mla_attention",
    )(q_lat, kv_lat, kv_ctx, a_lat, a_ctx)


def _mla_ctx_kernel(q_ref, kv_ref, kr_ref, o_ref):
    q = q_ref[...]
    s = _dot_nt(q[:, 0:LANES], kv_ref[:, 0:LANES]) + _dot_nt(q[:, LANES:2 * LANES], kr_ref[...])
    o_ref[...] = _softmax_once(s, kv_ref[:, LANES:2 * LANES]).astype(o_ref.dtype)


def _mla_ctx_attention(q_ctx, kv_ctx, a_ctx, kr_blk, geo):
    b, n, ctx = geo
    h = MLA_HEADS
    return pl.pallas_call(
        _mla_ctx_kernel,
        out_shape=jax.ShapeDtypeStruct((b * ctx, h * MLA_V), BF16),
        grid=(b, h),
        in_specs=[pl.BlockSpec((ctx, 256), lambda bi, hi: (bi, hi)),
                  pl.BlockSpec((ctx, 256), lambda bi, hi: (bi, hi)),
                  pl.BlockSpec((ctx, LANES), lambda bi, hi: (bi, kr_blk))],
        out_specs=pl.BlockSpec((ctx, MLA_V), lambda bi, hi: (bi, hi)),
        compiler_params=_cparams(("parallel", "arbitrary")),
        name="mla_ctx_attention",
    )(q_ctx, kv_ctx, a_ctx)


def _diff_kernel(q_ref, kl_ref, vl_ref, kc_ref, vc_ref, lam_ref, sub_ref, o_ref,
                 k_scr, v_scr, vt_scr, qt_scr, m0, l0, a0, m1, l1, a1,
                 st0, pt0, alpha0, st1, pt1, alpha1, *, tk, ctx, lam_init):
    hd = DIFF_HD
    full = slice(0, 2 * hd)

    @pl.when(pl.program_id(2) == 0)
    def _():
        _gather_rows(k_scr, kc_ref, kl_ref, full, full, ctx, ctx)
        _gather_rows(v_scr, vc_ref, vl_ref, full, full, ctx, ctx)
        _transpose_chunks(vt_scr, v_scr, tk)

    def chunk(c):
        return pl.ds(pl.multiple_of(c * tk, tk), tk)

    qt_scr[...] = q_ref[...].astype(F32).T.astype(BF16)

    def scores_t(c):
        return tuple(_dot(k_scr[chunk(c), i * hd:(i + 1) * hd], qt_scr[i * hd:(i + 1) * hd, :])
                     for i in range(2))

    _flash(vt_scr.shape[0], scores_t, lambda c: vt_scr[c],
           [(m0, l0, a0), (m1, l1, a1)], [(st0, pt0, alpha0), (st1, pt1, alpha1)])

    lv = lam_ref[...]
    lam = (jnp.exp(jnp.sum(lv[0:1] * lv[1:2], axis=-1, keepdims=True))
           - jnp.exp(jnp.sum(lv[2:3] * lv[3:4], axis=-1, keepdims=True)) + lam_init)
    o = (a0[...] / l0[...] - lam * (a1[...] / l1[...])).T
    o = o * lax.rsqrt(jnp.mean(o * o, axis=-1, keepdims=True) + EPS) * sub_ref[...]
    o_ref[...] = (o * (1.0 - lam_init)).astype(o_ref.dtype)


def _diff_attention(a_lat, kv_ctx, lam_rows, subln, geo, tq, tk, lam_init):
    b, n, ctx = geo
    h = DIFF_HEADS
    nt = n // tq
    return pl.pallas_call(
        functools.partial(_diff_kernel, tk=tk, ctx=ctx, lam_init=lam_init),
        out_shape=jax.ShapeDtypeStruct((b * n, h * 256), BF16),
        grid=(b, h, nt),
        in_specs=[pl.BlockSpec((tq, 256), lambda bi, hi, t: (bi * nt + t, h + hi)),
                  pl.BlockSpec((n, 256), lambda bi, hi, t: (bi, 2 * h + hi)),
                  pl.BlockSpec((n, 256), lambda bi, hi, t: (bi, 3 * h + hi)),
                  pl.BlockSpec((ctx, 256), lambda bi, hi, t: (bi, hi)),
                  pl.BlockSpec((ctx, 256), lambda bi, hi, t: (bi, h + hi)),
                  pl.BlockSpec((8, LANES), lambda bi, hi, t: (0, 0)),
                  pl.BlockSpec((1, 256), lambda bi, hi, t: (0, 0))],
        out_specs=pl.BlockSpec((tq, 256), lambda bi, hi, t: (bi * nt + t, hi)),
        scratch_shapes=[pltpu.VMEM((ctx + n, 256), BF16), pltpu.VMEM((ctx + n, 256), BF16),
                        pltpu.VMEM(((ctx + n) // tk, 256, tk), BF16),
                        pltpu.VMEM((256, tq), BF16)]
        + [pltpu.VMEM((1, tq), F32), pltpu.VMEM((1, tq), F32), pltpu.VMEM((256, tq), F32)] * 2
        + _flash_buffers(tq, tk, DIFF_FLASH_SLOTS) * 2,
        compiler_params=_cparams(("parallel", "arbitrary", "arbitrary")),
        name="diff_attention",
    )(a_lat, a_lat, a_lat, kv_ctx, kv_ctx, lam_rows, subln.reshape(1, 256).astype(F32))


def _half_select(x, half, axis=1):
    idx = lax.broadcasted_iota(jnp.int32, x.shape, axis)
    keep = (idx < 64) if half == 0 else (idx >= 64)
    return jnp.where(keep, x, jnp.zeros_like(x))


def _swa_kernel(*refs, is_lat, tq, band):
    if is_lat:
        sink_ref, q_ref, kl_ref, vl_ref, kc_ref, vc_ref, o_ref = refs
    else:
        sink_ref, q_ref, kc_ref, vc_ref, o_ref = refs
    t = pl.program_id(1)
    per = SWA_HEADS // SWA_KV_HEADS
    if is_lat:
        n = kl_ref.shape[0]
        start = pl.multiple_of(jnp.clip(t * tq - SWA_WINDOW, 0, n - band), SWA_WINDOW)
        kpos = start + lax.broadcasted_iota(jnp.int32, (band, per * tq), 0)
        qpos = t * tq + lax.broadcasted_iota(jnp.int32, (band, per * tq), 1) % tq
        ok = jnp.abs(qpos - kpos) <= SWA_WINDOW
    for a in range(SWA_KV_HEADS // 2):
        cols = slice(a * LANES, (a + 1) * LANES)
        qs = jnp.concatenate([q_ref[:, (a * per + j) * LANES:(a * per + j + 1) * LANES]
                              for j in range(per)], axis=0)
        kc = kc_ref[:, cols]
        vct = vc_ref[:, cols].astype(F32).T.astype(BF16)
        if is_lat:
            kb = kl_ref[pl.ds(start, band), cols]
            vbt = vl_ref[pl.ds(start, band), cols].astype(F32).T.astype(BF16)
        out_t = None
        for half in range(2):
            sink = jnp.concatenate(
                [jnp.full((1, tq), sink_ref[(a * per + j) * 2 + half] * LOG2E, F32)
                 for j in range(per)], axis=1)
            st_c = _dot_nt(_half_select(kc, half, 1), qs)
            m = jnp.maximum(jnp.max(st_c, axis=0, keepdims=True), sink)
            if is_lat:
                st_b = jnp.where(ok, _dot_nt(_half_select(kb, half, 1), qs), NEG_INF)
                m = jnp.maximum(m, jnp.max(st_b, axis=0, keepdims=True))
            pt_c = jnp.exp2(st_c - m)
            l = jnp.sum(pt_c, axis=0, keepdims=True) + jnp.exp2(sink - m)
            o_t = _dot(_half_select(vct, half, 0), pt_c.astype(BF16))
            if is_lat:
                pt_b = jnp.exp2(st_b - m)
                l = l + jnp.sum(pt_b, axis=0, keepdims=True)
                o_t = o_t + _dot(_half_select(vbt, half, 0), pt_b.astype(BF16))
            o_t = o_t / l
            out_t = o_t if out_t is None else out_t + o_t
        out = out_t.T
        for j in range(per):
            o_ref[:, (a * per + j) * LANES:(a * per + j + 1) * LANES] = (
                out[j * tq:(j + 1) * tq].astype(o_ref.dtype))


def _swa_attention(a_q, a_lat, a_ctx, sink_perm, geo, tq, is_lat):
    b, n, ctx = geo
    width = SWA_HEADS * SWA_HD
    kvw = SWA_KV_HEADS * SWA_HD
    rows = n if is_lat else ctx
    nt = rows // tq
    kblk, vblk = 2 * width // kvw, 2 * width // kvw + 1
    in_specs = [pl.BlockSpec(memory_space=pltpu.SMEM),
                pl.BlockSpec((tq, width), lambda bi, t: (bi * nt + t, 1))]
    args = [sink_perm, a_q]
    if is_lat:
        in_specs += [pl.BlockSpec((n, kvw), lambda bi, t: (bi, kblk)),
                     pl.BlockSpec((n, kvw), lambda bi, t: (bi, vblk))]
        args += [a_lat, a_lat]
    in_specs += [pl.BlockSpec((ctx, kvw), lambda bi, t: (bi, kblk)),
                 pl.BlockSpec((ctx, kvw), lambda bi, t: (bi, vblk))]
    args += [a_ctx, a_ctx]
    return pl.pallas_call(
        functools.partial(_swa_kernel, is_lat=is_lat, tq=tq, band=tq + 2 * SWA_WINDOW),
        out_shape=jax.ShapeDtypeStruct((b * rows, width), BF16),
        grid=(b, nt),
        in_specs=in_specs,
        out_specs=pl.BlockSpec((tq, width), lambda bi, t: (bi * nt + t, 0)),
        compiler_params=_cparams(("parallel", "arbitrary")),
        name="swa_attention" if is_lat else "swa_ctx_attention",
    )(*args)


def _na_kernel(*refs, is_lat, rq, rk, rows, pps):
    if is_lat:
        q_ref, kl_ref, vl_ref, kc_ref, vc_ref, bt_ref, o_ref, s_scr, p_scr = refs
    else:
        q_ref, kc_ref, vc_ref, o_ref = refs
    t = pl.program_id(2)
    w = GRID_W
    band = rk * w
    if is_lat:
        r0 = t * rq
        srow = jnp.clip(r0 - NA_WIN_R // 2, 0, rows - rk)
        start = pl.multiple_of(srow * w, w)
        lo = lax.broadcasted_iota(jnp.int32, (w, LANES), 1) < w

    def operands(pp):
        cols = slice(pp * LANES, (pp + 1) * LANES)
        q = q_ref[:, cols]
        kc = kc_ref[:, cols]
        vct = vc_ref[:, cols].astype(F32).T.astype(BF16)
        if not is_lat:
            return cols, q, kc, vct, None, None
        kb = kl_ref[pl.ds(start, band), cols]
        vbt = vl_ref[pl.ds(start, band), cols].astype(F32).T.astype(BF16)
        return cols, q, kc, vct, kb, vbt

    def bias_and_mask(s_half, half):
        for kj in range(rk):
            for qp in range(rq // 2):
                blk = (slice(kj * w, (kj + 1) * w), slice(qp * LANES, (qp + 1) * LANES))
                kr = srow + kj
                ra = r0 + 2 * qp
                rsa = jnp.clip(ra - NA_WIN_R // 2, 0, rows - NA_WIN_R)
                rsb = jnp.clip(ra + 1 - NA_WIN_R // 2, 0, rows - NA_WIN_R)
                e = jnp.clip(kr - ra + NA_WIN_R - 1, 0, 2 * NA_WIN_R - 1)
                pen0 = jnp.where((kr >= rsa) & (kr < rsa + NA_WIN_R), 0.0, NEG_INF)
                pen1 = jnp.where((kr >= rsb) & (kr < rsb + NA_WIN_R), 0.0, NEG_INF)
                s_half[blk] = s_half[blk] + (bt_ref[half, e] * LOG2E + jnp.where(lo, pen0, pen1))

    def attend(pp):
        cols, q, kc, vct, kb, vbt = operands(pp)
        out_t = None
        for half in range(2):
            st_c = _dot_nt(_half_select(kc, half, 1), q)
            m = jnp.max(st_c, axis=0, keepdims=True)
            if is_lat:
                s_half = s_scr.at[2 * pp + half]
                s_half[...] = _dot_nt(_half_select(kb, half, 1), q)
                bias_and_mask(s_half, 2 * pp + half)
                st_b = s_half[...]
                m = jnp.maximum(m, jnp.max(st_b, axis=0, keepdims=True))
            pt_c = jnp.exp2(st_c - m)
            l = jnp.sum(pt_c, axis=0, keepdims=True)
            o_t = _dot(_half_select(vct, half, 0), pt_c.astype(BF16))
            if is_lat:
                pt_b = jnp.exp2(st_b - m)
                l = l + jnp.sum(pt_b, axis=0, keepdims=True)
                o_t = o_t + _dot(_half_select(vbt, half, 0), pt_b.astype(BF16))
            o_t = o_t / l
            out_t = o_t if out_t is None else out_t + o_t
        o_ref[:, cols] = out_t.T.astype(o_ref.dtype)

    def attend_interior(pp):
        cols, q, kc, vct, kb, vbt = operands(pp)
        nv = NA_WIN_R + 1
        out_t = None
        for half in range(2):
            st_c = _dot_nt(_half_select(kc, half, 1), q)
            m_c = jnp.max(st_c, axis=0, keepdims=True)
            s_half, p_half = s_scr.at[2 * pp + half], p_scr.at[2 * pp + half]
            s_half[...] = _dot_nt(_half_select(kb, half, 1), q)
            m_parts, l_parts = [], []
            for qp in range(rq // 2):
                strip = slice(qp * LANES, (qp + 1) * LANES)
                k0 = 2 * qp
                blocks = []
                for dk in range(nv):
                    bias = bt_ref[2 * pp + half, dk + NA_WIN_R // 2 - 1] * LOG2E
                    if dk == 0:
                        bias = jnp.where(lo, bias, NEG_INF)
                    elif dk == NA_WIN_R:
                        bias = jnp.where(lo, NEG_INF, bias)
                    blocks.append(s_half[(k0 + dk) * w:(k0 + dk + 1) * w, strip] + bias)
                s_v = jnp.concatenate(blocks, axis=0)
                m_s = jnp.maximum(m_c[:, strip], jnp.max(s_v, axis=0, keepdims=True))
                p_v = jnp.exp2(s_v - m_s)
                m_parts.append(m_s)
                l_parts.append(jnp.sum(p_v, axis=0, keepdims=True))
                p_half[k0 * w:(k0 + nv) * w, strip] = p_v.astype(BF16)
                if k0 > 0:
                    p_half[0:k0 * w, strip] = jnp.zeros((k0 * w, LANES), BF16)
                if k0 + nv < rk:
                    p_half[(k0 + nv) * w:rk * w, strip] = jnp.zeros(((rk - k0 - nv) * w, LANES), BF16)
            m = jnp.concatenate(m_parts, axis=1)
            pt_c = jnp.exp2(st_c - m)
            l = jnp.sum(pt_c, axis=0, keepdims=True) + jnp.concatenate(l_parts, axis=1)
            o_t = (_dot(_half_select(vct, half, 0), pt_c.astype(BF16))
                   + _dot(_half_select(vbt, half, 0), p_half[...]))
            o_t = o_t / l
            out_t = o_t if out_t is None else out_t + o_t
        o_ref[:, cols] = out_t.T.astype(o_ref.dtype)

    def all_pairs(fn):
        for pp in range(pps):
            fn(pp)

    if is_lat:
        interior = (r0 >= NA_WIN_R // 2) & (r0 <= rows - rq - NA_WIN_R // 2)
        pl.when(interior)(functools.partial(all_pairs, attend_interior))
        pl.when(jnp.logical_not(interior))(functools.partial(all_pairs, attend))
    else:
        all_pairs(attend)


def _na_attention(a_q, a_lat, a_ctx, bias_tab, geo, rq, pps, is_lat):
    b, n, ctx = geo
    width = NA_HEADS * NA_HD
    wb = pps * LANES
    pairs = width // wb
    tq = rq * GRID_W
    rk = rq + NA_WIN_R
    grid_rows = n // GRID_W
    rows = n if is_lat else ctx
    assert grid_rows >= rk and rows % tq == 0
    nt = rows // tq
    qo, ko, vo = pairs, 2 * pairs, 3 * pairs
    in_specs = [pl.BlockSpec((tq, wb), lambda bi, pi, t: (bi * nt + t, qo + pi))]
    args = [a_q]
    if is_lat:
        in_specs += [pl.BlockSpec((n, wb), lambda bi, pi, t: (bi, ko + pi)),
                     pl.BlockSpec((n, wb), lambda bi, pi, t: (bi, vo + pi))]
        args += [a_lat, a_lat]
    in_specs += [pl.BlockSpec((ctx, wb), lambda bi, pi, t: (bi, ko + pi)),
                 pl.BlockSpec((ctx, wb), lambda bi, pi, t: (bi, vo + pi))]
    args += [a_ctx, a_ctx]
    scratch = []
    if is_lat:
        in_specs.append(pl.BlockSpec((2 * pps, 2 * NA_WIN_R, GRID_W, LANES),
                                     lambda bi, pi, t: (pi, 0, 0, 0)))
        args.append(bias_tab)
        scratch += [pltpu.VMEM((2 * pps, rk * GRID_W, tq), F32),
                    pltpu.VMEM((2 * pps, rk * GRID_W, tq), BF16)]
    return pl.pallas_call(
        functools.partial(_na_kernel, is_lat=is_lat, rq=rq, rk=rk, rows=grid_rows, pps=pps),
        out_shape=jax.ShapeDtypeStruct((b * rows, width), BF16),
        grid=(b, pairs, nt),
        in_specs=in_specs,
        out_specs=pl.BlockSpec((tq, wb), lambda bi, pi, t: (bi * nt + t, pi)),
        scratch_shapes=scratch,
        compiler_params=_cparams(("parallel", "arbitrary", "arbitrary")),
        name="na_attention" if is_lat else "na_ctx_attention",
    )(*args)


def _rope_tables(n, d_rot):
    t = jnp.arange(n)
    row = (t // GRID_W).astype(F32)
    col = (t % GRID_W).astype(F32)
    d_axis = d_rot // 2
    inv = ROPE_BASE ** (-jnp.arange(0, d_axis, 2, dtype=F32) / d_axis)
    ang = jnp.concatenate([row[:, None] * inv, col[:, None] * inv], axis=-1)
    cos, sin = jnp.cos(ang), jnp.sin(ang)
    reps = LANES // d_rot
    return (jnp.tile(cos, (1, 2 * reps)),
            jnp.tile(jnp.concatenate([-sin, sin], axis=-1), (1, reps)))


def _na_bias_table(rpb):
    col = jnp.arange(GRID_W)
    dc = jnp.clip(col[:, None] - col[None, :], -(NA_WIN_C - 1), NA_WIN_C - 1) + (NA_WIN_C - 1)
    cs = jnp.clip(col - NA_WIN_C // 2, 0, GRID_W - NA_WIN_C)[None, :]
    col_ok = (col[:, None] >= cs) & (col[:, None] < cs + NA_WIN_C)
    bt = jnp.where(col_ok, rpb[:, :, dc], NEG_INF)
    bt = jnp.pad(bt, ((0, 0), (1, 1), (0, 0), (0, 0)))
    return jnp.concatenate([bt[:, 1:], bt[:, :-1]], axis=-1).astype(F32)


def _tile_plan(b, n, n_ctx, d):
    tm = _largest_divisor(math.gcd(n, b * n_ctx), (512, 256, 128))
    return dict(
        tm=tm,
        tq_flash=_largest_divisor(n, (512, 256, 128)),
        tq_diff=_largest_divisor(n, (512, 256, 128)),
        tk=_largest_divisor(n + n_ctx, (768, 640, 512, 384, 256, 128)),
        tk_mla=_largest_divisor(n + n_ctx, (768, 640, 512, 384, 256, 128)),
        heads_mla=2,
        tq_swa=128,
        rq_na=8,
        rq_na_ctx=4,
        pairs_na=2,
    )


def kernel(x, c, ctx, c_ctx, l0_mod_w, l0_mod_b, l0_norm, l0_w_in, l0_q_norm, l0_w_qb, l0_kv_norm, l0_w_kvb, l0_w_out, l1_mod_w, l1_mod_b, l1_norm, l1_w_in, l1_sink, l1_w_out, l2_mod_w, l2_mod_b, l2_norm, l2_w_in, l2_rpb, l2_w_out, l3_mod_w, l3_mod_b, l3_norm, l3_w_in, l3_lam_q1, l3_lam_k1, l3_lam_q2, l3_lam_k2, l3_subln, l3_w_out, final_norm):
    b, n, d = x.shape
    n_ctx = ctx.shape[1]
    geo = (b, n, n_ctx)
    assert b < 8 and n % GRID_W == 0
    tp = _tile_plan(b, n, n_ctx, d)
    tm = tp["tm"]
    lat, cx = ("batch", n // tm), ("const", b)

    x_lat = x.reshape(b * n, d)
    x_ctx = ctx.reshape(b * n_ctx, d)
    c_rows = jnp.concatenate([c, c_ctx[None, :], jnp.zeros((8 - b - 1, d), F32)], axis=0)
    rope64 = _rope_tables(n, 64) + (n // tm,)
    rope128 = _rope_tables(n, 128) + (n // tm,)

    def groups(*spans):
        out = []
        for cols, code, scale in spans:
            out += [(code, scale)] * (cols // LANES)
        return tuple(out)

    def both(fn):
        return fn(x_lat, lat, True), fn(x_ctx, cx, False)

    def project_normed(h_lat, h_ctx, w, rope, grp_cfg, tn, name):
        return (_projection(h_lat, 0, d, None, None, None, w, rope, grp_cfg, tm, tn, name),
                _projection(h_ctx, 0, d, None, None, None, w, None, grp_cfg, tm, tn, name + "_ctx"))

    mods = [_modulation(c_rows, w, bias) for w, bias in
            ((l0_mod_w, l0_mod_b), (l1_mod_w, l1_mod_b), (l2_mod_w, l2_mod_b), (l3_mod_w, l3_mod_b))]

    mod = mods[0]
    r = MLA_RANK
    w_in = jnp.concatenate([l0_w_in[:, 2 * r + MLA_ROPE:], l0_w_in[:, :2 * r + MLA_ROPE],
                            jnp.zeros((d, 2 * LANES - MLA_ROPE), F32)], axis=1).astype(BF16)
    width = MLA_HEADS * MLA_V
    g0 = groups((width + 2 * r, PLAIN, 1.0), (LANES, ROPE64, 1.0), (LANES, PLAIN, 1.0))
    a_lat, a_ctx = both(lambda xs, grp, is_lat: _projection(
        xs, 0, d, l0_norm, mod, grp, w_in, rope64 if is_lat else None, g0, tm, 13 * LANES,
        "mla_in"))
    qk = MLA_NOPE + MLA_ROPE
    w_q = l0_w_qb.reshape(r, MLA_HEADS, qk)
    w_q = jnp.pad(w_q, ((0, 0), (0, 0), (0, 256 - qk))).reshape(r, MLA_HEADS * 256).astype(BF16)
    q_scale = qk ** -0.5 * LOG2E
    gq = groups((LANES, PLAIN, q_scale), (LANES, ROPE64, q_scale)) * MLA_HEADS
    gkv = groups((MLA_HEADS * 256, PLAIN, 1.0))
    w_kv = l0_w_kvb.astype(BF16)
    q_lat, kv_lat = (
        _projection(a_lat, width // r, r, l0_q_norm, None, None, w_q, rope64, gq, tm, 4096, "mla_q"),
        _projection(a_lat, width // r + 1, r, l0_kv_norm, None, None, w_kv, None, gkv, tm, 4096,
                    "mla_kv"))
    q_ctx, kv_ctx = (
        _projection(a_ctx, width // r, r, l0_q_norm, None, None, w_q, None, gq, tm, 4096,
                    "mla_q_ctx"),
        _projection(a_ctx, width // r + 1, r, l0_kv_norm, None, None, w_kv, None, gkv, tm, 4096,
                    "mla_kv_ctx"))
    kr_blk = (width + 2 * r) // LANES
    o_lat = _mla_attention(q_lat, kv_lat, kv_ctx, a_lat, a_ctx, kr_blk, geo, tp["tq_flash"],
                           tp["tk_mla"], tp["heads_mla"])
    o_ctx = _mla_ctx_attention(q_ctx, kv_ctx, a_ctx, kr_blk, geo)
    w_out = l0_w_out.astype(BF16)
    x_lat, h_lat = _out_projection(o_lat, a_lat, w_out, x_lat, mod, lat, tm, "mla_out", False,
                                   l1_norm, mods[1])
    x_ctx, h_ctx = _out_projection(o_ctx, a_ctx, w_out, x_ctx, mod, cx, tm, "mla_out", False,
                                   l1_norm, mods[1])

    mod = mods[1]
    per = SWA_HEADS // SWA_KV_HEADS
    pairs_kv = SWA_KV_HEADS // 2

    def pair_heads(w, axis):
        shape = w.shape
        w = w.reshape(shape[:axis] + (pairs_kv, 2, per, SWA_HD) + shape[axis + 1:])
        return jnp.swapaxes(w, axis + 1, axis + 2).reshape(shape)

    width = SWA_HEADS * SWA_HD
    kvw = SWA_KV_HEADS * SWA_HD
    w_q, w_k, w_v, w_g = (l1_w_in[:, :width], l1_w_in[:, width:width + kvw],
                          l1_w_in[:, width + kvw:width + 2 * kvw], l1_w_in[:, width + 2 * kvw:])
    w_in = jnp.concatenate([pair_heads(w_g, 1), pair_heads(w_q, 1), w_k, w_v],
                           axis=1).astype(BF16)
    g1 = groups((width, PLAIN, 1.0), (width, ROPE64, SWA_HD ** -0.5 * LOG2E),
                (kvw, ROPE64, 1.0), (kvw, PLAIN, 1.0))
    a_lat, a_ctx = project_normed(h_lat, h_ctx, w_in, rope64, g1, 2304, "swa_in")
    sink_perm = jnp.swapaxes(l1_sink.reshape(pairs_kv, 2, per), 1, 2).reshape(SWA_HEADS)
    tq = tp["tq_swa"]
    o_lat = _swa_attention(a_lat, a_lat, a_ctx, sink_perm, geo, tq, True)
    o_ctx = _swa_attention(a_ctx, a_lat, a_ctx, sink_perm, geo, tq, False)
    w_out = pair_heads(l1_w_out, 0).astype(BF16)
    x_lat, h_lat = _out_projection(o_lat, a_lat, w_out, x_lat, mod, lat, tm, "swa_out", True,
                                   l2_norm, mods[2])
    x_ctx, h_ctx = _out_projection(o_ctx, a_ctx, w_out, x_ctx, mod, cx, tm, "swa_out", True,
                                   l2_norm, mods[2])

    mod = mods[2]
    width = NA_HEADS * NA_HD
    w_in = jnp.concatenate([l2_w_in[:, 3 * width:], l2_w_in[:, :3 * width]], axis=1).astype(BF16)
    g2 = groups((width, PLAIN, 1.0), (width, PLAIN, NA_HD ** -0.5 * LOG2E), (2 * width, PLAIN, 1.0))
    a_lat, a_ctx = project_normed(h_lat, h_ctx, w_in, None, g2, 2048, "na_in")
    bias_tab = _na_bias_table(l2_rpb)
    o_lat = _na_attention(a_lat, a_lat, a_ctx, bias_tab, geo, tp["rq_na"], tp["pairs_na"], True)
    o_ctx = _na_attention(a_ctx, a_lat, a_ctx, bias_tab, geo, tp["rq_na_ctx"], tp["pairs_na"], False)
    w_out = l2_w_out.astype(BF16)
    x_lat, h_lat = _out_projection(o_lat, a_lat, w_out, x_lat, mod, lat, tm, "na_out", True,
                                   l3_norm, mods[3])
    _, h_ctx = _out_projection(o_ctx, a_ctx, w_out, x_ctx, mod, cx, tm, "na_out", True,
                               l3_norm, mods[3])

    mod = mods[3]
    width = DIFF_HEADS * 2 * DIFF_HD
    w_in = jnp.concatenate([l3_w_in[:, 3 * width:], l3_w_in[:, :3 * width]], axis=1).astype(BF16)
    g3 = groups((width, PLAIN, 1.0), (width, ROPE128, DIFF_HD ** -0.5 * LOG2E),
                (width, ROPE128, 1.0), (width, PLAIN, 1.0))
    a_lat = _projection(h_lat, 0, d, None, None, None, w_in, rope128, g3, tm, 2048, "diff_in")
    kv_ctx = _projection(h_ctx, 0, d, None, None, None, w_in[:, 2 * width:], None,
                         g3[2 * width // LANES:], tm, 1024, "diff_in_ctx")
    lam_init = 0.8 - 0.6 * math.exp(-0.3 * 3)
    lam_rows = jnp.concatenate([jnp.stack([l3_lam_q1, l3_lam_k1, l3_lam_q2, l3_lam_k2]),
                                jnp.zeros((4, DIFF_HD), F32)], axis=0)
    o_lat = _diff_attention(a_lat, kv_ctx, lam_rows, l3_subln, geo, tp["tq_diff"], tp["tk"],
                            lam_init)
    out = _out_projection(o_lat, a_lat, l3_w_out.astype(BF16), x_lat, mod, lat, tm, "diff_out",
                          False, final_norm)
    return out.reshape(b, n, d)
```

```python
import functools
import math

import jax
import jax.numpy as jnp
from jax import lax
from jax.experimental import pallas as pl
from jax.experimental.pallas import tpu as pltpu

F32 = jnp.float32
BF16 = jnp.bfloat16

LANES = 128
LOG2E = 1.4426950408889634
NEG_INF = -1e30
EPS = 1e-6
ROPE_BASE = 10000.0
GRID_W = 64
VMEM_LIMIT = 56 * 1024 * 1024

MLA_HEADS, MLA_NOPE, MLA_ROPE, MLA_V, MLA_RANK = 16, 128, 64, 128, 512
SWA_HEADS, SWA_KV_HEADS, SWA_HD, SWA_WINDOW = 32, 4, 64, 128
NA_HEADS, NA_HD, NA_WIN_R, NA_WIN_C = 32, 64, 8, 16
DIFF_HEADS, DIFF_HD = 8, 128

PLAIN, ROPE64, ROPE128 = 0, 1, 2
MLA_FLASH_SLOTS, DIFF_FLASH_SLOTS = 2, 2


def _cparams(sem):
    return pltpu.CompilerParams(dimension_semantics=sem, vmem_limit_bytes=VMEM_LIMIT)


def _dot_nt(a, b):
    return lax.dot_general(a, b, (((1,), (1,)), ((), ())), preferred_element_type=F32)


def _dot(a, b):
    return jnp.dot(a, b, preferred_element_type=F32)


def _largest_divisor(total, candidates):
    return next(c for c in candidates if total % c == 0)


def _mod_kernel(c_ref, w_ref, b_ref, o_ref):
    c = c_ref[...]
    s = c * jax.nn.sigmoid(c)
    o_ref[...] = _dot(s, w_ref[...]) + b_ref[...]


def _modulation(c_rows, mod_w, mod_b):
    r, d = c_rows.shape
    n = mod_w.shape[1]
    tn = math.gcd(n, 512)
    out = pl.pallas_call(
        _mod_kernel,
        out_shape=jax.ShapeDtypeStruct((r, n), F32),
        grid=(n // tn,),
        in_specs=[pl.BlockSpec((r, d), lambda j: (0, 0)),
                  pl.BlockSpec((d, tn), lambda j: (0, j)),
                  pl.BlockSpec((1, tn), lambda j: (0, j))],
        out_specs=pl.BlockSpec((r, tn), lambda j: (0, j)),
        compiler_params=_cparams(("arbitrary",)),
        name="modulation",
    )(c_rows, mod_w, mod_b.reshape(1, n))
    return out.reshape(r, 1, n)


def _mod_row(group):
    kind, val = group
    return (lambda i: i // val) if kind == "batch" else (lambda i: val)


def _rope_group(a, cos, sin, code):
    if code == ROPE64:
        lane = lax.broadcasted_iota(jnp.int32, a.shape, 1)
        first = (lane % 64) < 32
        partner = jnp.where(first, pltpu.roll(a, 96, 1), pltpu.roll(a, 32, 1))
    else:
        partner = pltpu.roll(a, 64, 1)
    return a * cos + partner * sin


def _proj_kernel(*refs, has_norm, has_mod, has_rope, tile_patterns, d_mod):
    it = iter(refs)
    x_ref = next(it)
    g_ref = next(it) if has_norm else None
    mod_ref = next(it) if has_mod else None
    w_ref = next(it)
    cos_ref, sin_ref = (next(it), next(it)) if has_rope else (None, None)
    o_ref = next(it)
    h_ref = next(it) if has_norm else x_ref
    j = pl.program_id(1)

    if has_norm:
        @pl.when(j == 0)
        def _():
            x = x_ref[...].astype(F32)
            y = x * lax.rsqrt(jnp.mean(x * x, axis=-1, keepdims=True) + EPS) * g_ref[...]
            if has_mod:
                shift = mod_ref[0, :, 0:d_mod]
                scale = mod_ref[0, :, d_mod:2 * d_mod]
                y = y * (1.0 + scale) + shift
            h_ref[...] = y.astype(BF16)

    def epilogue(pattern):
        acc = _dot(h_ref[...], w_ref[...])
        for gi, (code, scale) in enumerate(pattern):
            a = acc[:, gi * LANES:(gi + 1) * LANES]
            if code != PLAIN:
                a = _rope_group(a, cos_ref[...], sin_ref[...], code)
            if scale != 1.0:
                a = a * scale
            o_ref[:, gi * LANES:(gi + 1) * LANES] = a.astype(o_ref.dtype)

    distinct = sorted(set(tile_patterns), key=tile_patterns.index)
    if len(distinct) == 1:
        epilogue(distinct[0])
    else:
        for pat in distinct:
            tiles = [jj for jj, p in enumerate(tile_patterns) if p == pat]
            cond = functools.reduce(jnp.logical_or, [j == jj for jj in tiles])
            pl.when(cond)(functools.partial(epilogue, pat))


def _projection(x, col_blk, kd, gain, mod, group, w, rope, groups, tm, tn, name):
    m = x.shape[0]
    n_out = w.shape[1]
    tm = min(tm, m)
    if group is not None and group[0] == "const":
        name += "_ctx"
    assert m % tm == 0 and n_out % tn == 0 and tn % LANES == 0 and len(groups) == n_out // LANES
    if rope is None:
        groups = tuple((PLAIN, scale) for _, scale in groups)
    has_norm = gain is not None
    has_mod = mod is not None
    assert has_norm or (not has_mod and x.dtype == BF16)
    has_rope = any(code != PLAIN for code, _ in groups)
    gpt = tn // LANES
    tile_patterns = tuple(tuple(groups[t * gpt:(t + 1) * gpt]) for t in range(n_out // tn))

    in_specs = [pl.BlockSpec((tm, kd), lambda i, j: (i, col_blk))]
    args = [x]
    if has_norm:
        in_specs.append(pl.BlockSpec((1, kd), lambda i, j: (0, 0)))
        args.append(gain.reshape(1, kd).astype(F32))
    d_mod = 0
    if has_mod:
        d_mod = mod.shape[2] // 3
        row = _mod_row(group)
        in_specs.append(pl.BlockSpec((1, 1, 3 * d_mod), lambda i, j: (row(i), 0, 0)))
        args.append(mod)
    in_specs.append(pl.BlockSpec((kd, tn), lambda i, j: (0, j)))
    args.append(w)
    if has_rope:
        cos, sin, tiles_per_batch = rope
        in_specs += [pl.BlockSpec((tm, LANES), lambda i, j: (i % tiles_per_batch, 0))] * 2
        args += [cos, sin]
    kern = functools.partial(_proj_kernel, has_norm=has_norm, has_mod=has_mod, has_rope=has_rope,
                             tile_patterns=tile_patterns, d_mod=d_mod)
    return pl.pallas_call(
        kern,
        out_shape=jax.ShapeDtypeStruct((m, n_out), BF16),
        grid=(m // tm, n_out // tn),
        in_specs=in_specs,
        out_specs=pl.BlockSpec((tm, tn), lambda i, j: (i, j)),
        scratch_shapes=[pltpu.VMEM((tm, kd), BF16)] if has_norm else [],
        compiler_params=_cparams(("parallel", "arbitrary")),
        name=name,
    )(*args)


def _out_kernel(*refs, follow, d_mod):
    o_ref, gate_ref, w_ref, x_ref, gm_ref = refs[:5]
    g = gate_ref[...].astype(F32)
    u = (o_ref[...].astype(F32) * (g * jax.nn.sigmoid(g))).astype(BF16)
    y = x_ref[...] + gm_ref[0] * _dot(u, w_ref[...])
    if follow is None:
        refs[5][...] = y
        return
    yn = y * lax.rsqrt(jnp.mean(y * y, axis=-1, keepdims=True) + EPS) * refs[5][...]
    if follow == "final":
        refs[6][...] = yn
    else:
        nmod_ref, y_ref, h_ref = refs[6:9]
        y_ref[...] = y
        h_ref[...] = (yn * (1.0 + nmod_ref[0, :, d_mod:2 * d_mod])
                      + nmod_ref[0, :, 0:d_mod]).astype(h_ref.dtype)


def _out_projection(o, proj, w_out, x, mod, group, tm, name, in_place=True,
                    next_gain=None, next_mod=None):
    m, d = x.shape
    width = w_out.shape[0]
    tm = min(tm, m)
    if group[0] == "const":
        name += "_ctx"
    row = _mod_row(group)
    follow = None if next_gain is None else ("final" if next_mod is None else "layer")
    in_specs = [pl.BlockSpec((tm, width), lambda i: (i, 0)),
                pl.BlockSpec((tm, width), lambda i: (i, 0)),
                pl.BlockSpec((width, d), lambda i: (0, 0)),
                pl.BlockSpec((tm, d), lambda i: (i, 0)),
                pl.BlockSpec((1, 1, d), lambda i: (row(i), 0, 2))]
    args = [o, proj, w_out, x, mod]
    rows_f32 = jax.ShapeDtypeStruct((m, d), F32)
    row_spec = pl.BlockSpec((tm, d), lambda i: (i, 0))
    out_shape, out_specs = rows_f32, row_spec
    if follow is not None:
        in_specs.append(pl.BlockSpec((1, d), lambda i: (0, 0)))
        args.append(next_gain.reshape(1, d).astype(F32))
    if follow == "layer":
        in_specs.append(pl.BlockSpec((1, 1, 3 * d), lambda i: (row(i), 0, 0)))
        args.append(next_mod)
        out_shape = (rows_f32, jax.ShapeDtypeStruct((m, d), BF16))
        out_specs = (row_spec, row_spec)
    return pl.pallas_call(
        functools.partial(_out_kernel, follow=follow, d_mod=d),
        out_shape=out_shape,
        grid=(m // tm,),
        in_specs=in_specs,
        out_specs=out_specs,
        input_output_aliases={3: 0} if in_place and follow != "final" else {},
        compiler_params=_cparams(("parallel",)),
        name=name,
    )(*args)


def _flash(n_chunks, scores_t, values_t, stats, bufs):
    slots = bufs[0][0].shape[0]
    for m_scr, l_scr, acc in stats:
        m_scr[...] = jnp.full(m_scr.shape, NEG_INF, F32)
        l_scr[...] = jnp.zeros(l_scr.shape, F32)
        acc[...] = jnp.zeros(acc.shape, F32)

    def load_scores(c, slot):
        for s, (st, _, _) in zip(scores_t(c), bufs):
            st[slot] = s

    def softmax(slot):
        for (m_scr, l_scr, _), (st, pt, alpha_buf) in zip(stats, bufs):
            s = st[slot]
            m_prev = m_scr[...]
            m_new = jnp.maximum(m_prev, jnp.max(s, axis=0, keepdims=True))
            alpha = jnp.exp2(m_prev - m_new)
            p = jnp.exp2(s - m_new)
            l_scr[...] = alpha * l_scr[...] + jnp.sum(p, axis=0, keepdims=True)
            m_scr[...] = m_new
            alpha_buf[slot] = alpha
            pt[slot] = p.astype(pt.dtype)

    def weighted_values(c, slot):
        vts = values_t(c)
        if not isinstance(vts, tuple):
            vts = (vts,) * len(stats)
        for vt, (_, _, acc), (_, pt, alpha_buf) in zip(vts, stats, bufs):
            acc[...] = alpha_buf[slot] * acc[...] + _dot(vt, pt[slot])

    def step(c, k, first, last):
        if not last:
            load_scores(c + 1, (k + 1) % slots)
        if not first:
            weighted_values(c - 1, (k - 1) % slots)
        softmax(k)

    load_scores(0, 0)
    step(0, 0, True, n_chunks == 1)
    iters = max(n_chunks - 2, 0) // slots

    def body(i, carry):
        for k in range(slots):
            step(1 + slots * i + k, (1 + k) % slots, False, False)
        return carry
    lax.fori_loop(0, iters, body, 0)
    for c in range(1 + slots * iters, n_chunks):
        step(c, c % slots, False, c == n_chunks - 1)
    weighted_values(n_chunks - 1, (n_chunks - 1) % slots)


def _transpose_chunks(dst, src, tk):
    def body(c, carry):
        rows = pl.ds(pl.multiple_of(c * tk, tk), tk)
        dst[c] = src[rows, :].astype(F32).T.astype(dst.dtype)
        return carry
    lax.fori_loop(0, dst.shape[0], body, 0)


def _softmax_once(s, v):
    m = jnp.max(s, axis=-1, keepdims=True)
    p = jnp.exp2(s - m)
    return _dot(p.astype(BF16), v) / jnp.sum(p, axis=-1, keepdims=True)


def _gather_rows(dst, src_ctx, src_lat, dst_cols, src_cols, ctx, step):
    dst[0:ctx, dst_cols] = src_ctx[:, src_cols]

    def copy(c, carry):
        r = pl.multiple_of(c * step, step)
        dst[pl.ds(ctx + r, step), dst_cols] = src_lat[pl.ds(r, step), src_cols]
        return carry
    lax.fori_loop(0, src_lat.shape[0] // step, copy, 0)


def _mla_kernel(q_ref, kvl_ref, kvc_ref, krl_ref, krc_ref, o_ref,
                k_scr, v_scr, vt_scr, qt_scr, *stat_and_bufs, tk, ctx, heads):
    lo, hi = slice(0, LANES), slice(LANES, 2 * LANES)
    stats = [stat_and_bufs[3 * g:3 * g + 3] for g in range(heads)]
    bufs = [stat_and_bufs[3 * heads + 3 * g:3 * heads + 3 * g + 3] for g in range(heads)]

    @pl.when(pl.program_id(2) == 0)
    def _():
        for g in range(heads):
            kn = slice(2 * g * LANES, (2 * g + 1) * LANES)
            vv = slice((2 * g + 1) * LANES, (2 * g + 2) * LANES)
            _gather_rows(k_scr.at[g], kvc_ref, kvl_ref, lo, kn, ctx, ctx)
            _gather_rows(k_scr.at[g], krc_ref, krl_ref, hi, lo, ctx, ctx)
            _gather_rows(v_scr, kvc_ref, kvl_ref, lo, vv, ctx, ctx)
            _transpose_chunks(vt_scr.at[g], v_scr, tk)

    def chunk(c):
        return pl.ds(pl.multiple_of(c * tk, tk), tk)

    for g in range(heads):
        qt_scr[g] = q_ref[:, 2 * g * LANES:(2 * g + 2) * LANES].astype(F32).T.astype(BF16)
    _flash(vt_scr.shape[1],
           lambda c: tuple(_dot(k_scr[g, chunk(c), :], qt_scr[g]) for g in range(heads)),
           lambda c: tuple(vt_scr[g, c] for g in range(heads)),
           stats, bufs)
    for g, (_, l_scr, acc) in enumerate(stats):
        o_ref[:, g * MLA_V:(g + 1) * MLA_V] = (acc[...] / l_scr[...]).T.astype(o_ref.dtype)


def _flash_buffers(tq, tk, slots):
    return [pltpu.VMEM((slots, tk, tq), F32), pltpu.VMEM((slots, tk, tq), BF16),
            pltpu.VMEM((slots, 1, tq), F32)]


def _mla_attention(q_lat, kv_lat, kv_ctx, a_lat, a_ctx, kr_blk, geo, tq, tk, heads):
    b, n, ctx = geo
    h = MLA_HEADS
    nt = n // tq
    return pl.pallas_call(
        functools.partial(_mla_kernel, tk=tk, ctx=ctx, heads=heads),
        out_shape=jax.ShapeDtypeStruct((b * n, h * MLA_V), BF16),
        grid=(b, h // heads, nt),
        in_specs=[pl.BlockSpec((tq, heads * 256), lambda bi, hi, t: (bi * nt + t, hi)),
                  pl.BlockSpec((n, heads * 256), lambda bi, hi, t: (bi, hi)),
                  pl.BlockSpec((ctx, heads * 256), lambda bi, hi, t: (bi, hi)),
                  pl.BlockSpec((n, LANES), lambda bi, hi, t: (bi, kr_blk)),
                  pl.BlockSpec((ctx, LANES), lambda bi, hi, t: (bi, kr_blk))],
        out_specs=pl.BlockSpec((tq, heads * MLA_V), lambda bi, hi, t: (bi * nt + t, hi)),
        scratch_shapes=[pltpu.VMEM((heads, ctx + n, 256), BF16),
                        pltpu.VMEM((ctx + n, MLA_V), BF16),
                        pltpu.VMEM((heads, (ctx + n) // tk, MLA_V, tk), BF16),
                        pltpu.VMEM((heads, 256, tq), BF16)]
        + [pltpu.VMEM((1, tq), F32), pltpu.VMEM((1, tq), F32),
           pltpu.VMEM((MLA_V, tq), F32)] * heads
        + _flash_buffers(tq, tk, MLA_FLASH_SLOTS) * heads,
        compiler_params=_cparams(("parallel", "arbitrary", "arbitrary")),
        name="mla_attention",
    )(q_lat, kv_lat, kv_ctx, a_lat, a_ctx)


def _mla_ctx_kernel(q_ref, kv_ref, kr_ref, o_ref):
    q = q_ref[...]
    s = _dot_nt(q[:, 0:LANES], kv_ref[:, 0:LANES]) + _dot_nt(q[:, LANES:2 * LANES], kr_ref[...])
    o_ref[...] = _softmax_once(s, kv_ref[:, LANES:2 * LANES]).astype(o_ref.dtype)


def _mla_ctx_attention(q_ctx, kv_ctx, a_ctx, kr_blk, geo):
    b, n, ctx = geo
    h = MLA_HEADS
    return pl.pallas_call(
        _mla_ctx_kernel,
        out_shape=jax.ShapeDtypeStruct((b * ctx, h * MLA_V), BF16),
        grid=(b, h),
        in_specs=[pl.BlockSpec((ctx, 256), lambda bi, hi: (bi, hi)),
                  pl.BlockSpec((ctx, 256), lambda bi, hi: (bi, hi)),
                  pl.BlockSpec((ctx, LANES), lambda bi, hi: (bi, kr_blk))],
        out_specs=pl.BlockSpec((ctx, MLA_V), lambda bi, hi: (bi, hi)),
        compiler_params=_cparams(("parallel", "arbitrary")),
        name="mla_ctx_attention",
    )(q_ctx, kv_ctx, a_ctx)


def _diff_kernel(q_ref, kl_ref, vl_ref, kc_ref, vc_ref, lam_ref, sub_ref, o_ref,
                 k_scr, v_scr, vt_scr, qt_scr, m0, l0, a0, m1, l1, a1,
                 st0, pt0, alpha0, st1, pt1, alpha1, *, tk, ctx, lam_init):
    hd = DIFF_HD
    full = slice(0, 2 * hd)

    @pl.when(pl.program_id(2) == 0)
    def _():
        _gather_rows(k_scr, kc_ref, kl_ref, full, full, ctx, ctx)
        _gather_rows(v_scr, vc_ref, vl_ref, full, full, ctx, ctx)
        _transpose_chunks(vt_scr, v_scr, tk)

    def chunk(c):
        return pl.ds(pl.multiple_of(c * tk, tk), tk)

    qt_scr[...] = q_ref[...].astype(F32).T.astype(BF16)

    def scores_t(c):
        return tuple(_dot(k_scr[chunk(c), i * hd:(i + 1) * hd], qt_scr[i * hd:(i + 1) * hd, :])
                     for i in range(2))

    _flash(vt_scr.shape[0], scores_t, lambda c: vt_scr[c],
           [(m0, l0, a0), (m1, l1, a1)], [(st0, pt0, alpha0), (st1, pt1, alpha1)])

    lv = lam_ref[...]
    lam = (jnp.exp(jnp.sum(lv[0:1] * lv[1:2], axis=-1, keepdims=True))
           - jnp.exp(jnp.sum(lv[2:3] * lv[3:4], axis=-1, keepdims=True)) + lam_init)
    o = (a0[...] / l0[...] - lam * (a1[...] / l1[...])).T
    o = o * lax.rsqrt(jnp.mean(o * o, axis=-1, keepdims=True) + EPS) * sub_ref[...]
    o_ref[...] = (o * (1.0 - lam_init)).astype(o_ref.dtype)


def _diff_attention(a_lat, kv_ctx, lam_rows, subln, geo, tq, tk, lam_init):
    b, n, ctx = geo
    h = DIFF_HEADS
    nt = n // tq
    return pl.pallas_call(
        functools.partial(_diff_kernel, tk=tk, ctx=ctx, lam_init=lam_init),
        out_shape=jax.ShapeDtypeStruct((b * n, h * 256), BF16),
        grid=(b, h, nt),
        in_specs=[pl.BlockSpec((tq, 256), lambda bi, hi, t: (bi * nt + t, h + hi)),
                  pl.BlockSpec((n, 256), lambda bi, hi, t: (bi, 2 * h + hi)),
                  pl.BlockSpec((n, 256), lambda bi, hi, t: (bi, 3 * h + hi)),
                  pl.BlockSpec((ctx, 256), lambda bi, hi, t: (bi, hi)),
                  pl.BlockSpec((ctx, 256), lambda bi, hi, t: (bi, h + hi)),
                  pl.BlockSpec((8, LANES), lambda bi, hi, t: (0, 0)),
                  pl.BlockSpec((1, 256), lambda bi, hi, t: (0, 0))],
        out_specs=pl.BlockSpec((tq, 256), lambda bi, hi, t: (bi * nt + t, hi)),
        scratch_shapes=[pltpu.VMEM((ctx + n, 256), BF16), pltpu.VMEM((ctx + n, 256), BF16),
                        pltpu.VMEM(((ctx + n) // tk, 256, tk), BF16),
                        pltpu.VMEM((256, tq), BF16)]
        + [pltpu.VMEM((1, tq), F32), pltpu.VMEM((1, tq), F32), pltpu.VMEM((256, tq), F32)] * 2
        + _flash_buffers(tq, tk, DIFF_FLASH_SLOTS) * 2,
        compiler_params=_cparams(("parallel", "arbitrary", "arbitrary")),
        name="diff_attention",
    )(a_lat, a_lat, a_lat, kv_ctx, kv_ctx, lam_rows, subln.reshape(1, 256).astype(F32))


def _half_select(x, half, axis=1):
    idx = lax.broadcasted_iota(jnp.int32, x.shape, axis)
    keep = (idx < 64) if half == 0 else (idx >= 64)
    return jnp.where(keep, x, jnp.zeros_like(x))


def _swa_kernel(*refs, is_lat, tq, band, subs):
    if is_lat:
        sink_ref, q_ref, kl_ref, vl_ref, kc_ref, vc_ref, o_ref = refs
    else:
        sink_ref, q_ref, kc_ref, vc_ref, o_ref = refs
    per = SWA_HEADS // SWA_KV_HEADS
    for sub in range(subs):
        _swa_tile(sink_ref, q_ref, kl_ref if is_lat else None, vl_ref if is_lat else None,
                  kc_ref, vc_ref, o_ref, pl.program_id(1) * subs + sub,
                  slice(sub * tq, (sub + 1) * tq), is_lat=is_lat, tq=tq, band=band, per=per)


def _swa_tile(sink_ref, q_ref, kl_ref, vl_ref, kc_ref, vc_ref, o_ref, t, rows, *,
              is_lat, tq, band, per):
    if is_lat:
        n = kl_ref.shape[0]
        start = pl.multiple_of(jnp.clip(t * tq - SWA_WINDOW, 0, n - band), SWA_WINDOW)
        kpos = start + lax.broadcasted_iota(jnp.int32, (band, per * tq), 0)
        qpos = t * tq + lax.broadcasted_iota(jnp.int32, (band, per * tq), 1) % tq
        ok = jnp.abs(qpos - kpos) <= SWA_WINDOW
    for a in range(SWA_KV_HEADS // 2):
        cols = slice(a * LANES, (a + 1) * LANES)
        qs = jnp.concatenate([q_ref[rows, (a * per + j) * LANES:(a * per + j + 1) * LANES]
                              for j in range(per)], axis=0)
        kc = kc_ref[:, cols]
        vct = vc_ref[:, cols].astype(F32).T.astype(BF16)
        if is_lat:
            kb = kl_ref[pl.ds(start, band), cols]
            vbt = vl_ref[pl.ds(start, band), cols].astype(F32).T.astype(BF16)
        out_t = None
        for half in range(2):
            sink = jnp.concatenate(
                [jnp.full((1, tq), sink_ref[(a * per + j) * 2 + half] * LOG2E, F32)
                 for j in range(per)], axis=1)
            st_c = _dot_nt(_half_select(kc, half, 1), qs)
            m = jnp.maximum(jnp.max(st_c, axis=0, keepdims=True), sink)
            if is_lat:
                st_b = jnp.where(ok, _dot_nt(_half_select(kb, half, 1), qs), NEG_INF)
                m = jnp.maximum(m, jnp.max(st_b, axis=0, keepdims=True))
            pt_c = jnp.exp2(st_c - m)
            l = jnp.sum(pt_c, axis=0, keepdims=True) + jnp.exp2(sink - m)
            o_t = _dot(_half_select(vct, half, 0), pt_c.astype(BF16))
            if is_lat:
                pt_b = jnp.exp2(st_b - m)
                l = l + jnp.sum(pt_b, axis=0, keepdims=True)
                o_t = o_t + _dot(_half_select(vbt, half, 0), pt_b.astype(BF16))
            o_t = o_t / l
            out_t = o_t if out_t is None else out_t + o_t
        out = out_t.T
        for j in range(per):
            o_ref[rows, (a * per + j) * LANES:(a * per + j + 1) * LANES] = (
                out[j * tq:(j + 1) * tq].astype(o_ref.dtype))


def _swa_attention(a_q, a_lat, a_ctx, sink_perm, geo, tq, is_lat):
    b, n, ctx = geo
    width = SWA_HEADS * SWA_HD
    kvw = SWA_KV_HEADS * SWA_HD
    rows = n if is_lat else ctx
    subs = 2 if (rows // tq) % 2 == 0 else 1
    nt = rows // (tq * subs)
    kblk, vblk = 2 * width // kvw, 2 * width // kvw + 1
    in_specs = [pl.BlockSpec(memory_space=pltpu.SMEM),
                pl.BlockSpec((subs * tq, width), lambda bi, t: (bi * nt + t, 1))]
    args = [sink_perm, a_q]
    if is_lat:
        in_specs += [pl.BlockSpec((n, kvw), lambda bi, t: (bi, kblk)),
                     pl.BlockSpec((n, kvw), lambda bi, t: (bi, vblk))]
        args += [a_lat, a_lat]
    in_specs += [pl.BlockSpec((ctx, kvw), lambda bi, t: (bi, kblk)),
                 pl.BlockSpec((ctx, kvw), lambda bi, t: (bi, vblk))]
    args += [a_ctx, a_ctx]
    return pl.pallas_call(
        functools.partial(_swa_kernel, is_lat=is_lat, tq=tq, band=tq + 2 * SWA_WINDOW,
                          subs=subs),
        out_shape=jax.ShapeDtypeStruct((b * rows, width), BF16),
        grid=(b, nt),
        in_specs=in_specs,
        out_specs=pl.BlockSpec((subs * tq, width), lambda bi, t: (bi * nt + t, 0)),
        compiler_params=_cparams(("parallel", "arbitrary")),
        name="swa_attention" if is_lat else "swa_ctx_attention",
    )(*args)


def _na_kernel(*refs, is_lat, rq, rk, rows, pps):
    if is_lat:
        q_ref, kl_ref, vl_ref, kc_ref, vc_ref, bt_ref, o_ref, s_scr, p_scr = refs
    else:
        q_ref, kc_ref, vc_ref, o_ref = refs
    t = pl.program_id(2)
    w = GRID_W
    band = rk * w
    if is_lat:
        r0 = t * rq
        srow = jnp.clip(r0 - NA_WIN_R // 2, 0, rows - rk)
        start = pl.multiple_of(srow * w, w)
        lo = lax.broadcasted_iota(jnp.int32, (w, LANES), 1) < w

    def operands(pp):
        cols = slice(pp * LANES, (pp + 1) * LANES)
        q = q_ref[:, cols]
        kc = kc_ref[:, cols]
        vct = vc_ref[:, cols].astype(F32).T.astype(BF16)
        if not is_lat:
            return cols, q, kc, vct, None, None
        kb = kl_ref[pl.ds(start, band), cols]
        vbt = vl_ref[pl.ds(start, band), cols].astype(F32).T.astype(BF16)
        return cols, q, kc, vct, kb, vbt

    def bias_and_mask(s_half, half):
        for kj in range(rk):
            for qp in range(rq // 2):
                blk = (slice(kj * w, (kj + 1) * w), slice(qp * LANES, (qp + 1) * LANES))
                kr = srow + kj
                ra = r0 + 2 * qp
                rsa = jnp.clip(ra - NA_WIN_R // 2, 0, rows - NA_WIN_R)
                rsb = jnp.clip(ra + 1 - NA_WIN_R // 2, 0, rows - NA_WIN_R)
                e = jnp.clip(kr - ra + NA_WIN_R - 1, 0, 2 * NA_WIN_R - 1)
                pen0 = jnp.where((kr >= rsa) & (kr < rsa + NA_WIN_R), 0.0, NEG_INF)
                pen1 = jnp.where((kr >= rsb) & (kr < rsb + NA_WIN_R), 0.0, NEG_INF)
                s_half[blk] = s_half[blk] + (bt_ref[half, e] * LOG2E + jnp.where(lo, pen0, pen1))

    def attend(pp):
        cols, q, kc, vct, kb, vbt = operands(pp)
        out_t = None
        for half in range(2):
            st_c = _dot_nt(_half_select(kc, half, 1), q)
            m = jnp.max(st_c, axis=0, keepdims=True)
            if is_lat:
                s_half = s_scr.at[2 * pp + half]
                s_half[...] = _dot_nt(_half_select(kb, half, 1), q)
                bias_and_mask(s_half, 2 * pp + half)
                st_b = s_half[...]
                m = jnp.maximum(m, jnp.max(st_b, axis=0, keepdims=True))
            pt_c = jnp.exp2(st_c - m)
            l = jnp.sum(pt_c, axis=0, keepdims=True)
            o_t = _dot(_half_select(vct, half, 0), pt_c.astype(BF16))
            if is_lat:
                pt_b = jnp.exp2(st_b - m)
                l = l + jnp.sum(pt_b, axis=0, keepdims=True)
                o_t = o_t + _dot(_half_select(vbt, half, 0), pt_b.astype(BF16))
            o_t = o_t / l
            out_t = o_t if out_t is None else out_t + o_t
        o_ref[:, cols] = out_t.T.astype(o_ref.dtype)

    def attend_interior(pp):
        cols, q, kc, vct, kb, vbt = operands(pp)
        nv = NA_WIN_R + 1
        out_t = None
        for half in range(2):
            st_c = _dot_nt(_half_select(kc, half, 1), q)
            m_c = jnp.max(st_c, axis=0, keepdims=True)
            s_half, p_half = s_scr.at[2 * pp + half], p_scr.at[2 * pp + half]
            s_half[...] = _dot_nt(_half_select(kb, half, 1), q)
            m_parts, l_parts = [], []
            for qp in range(rq // 2):
                strip = slice(qp * LANES, (qp + 1) * LANES)
                k0 = 2 * qp
                blocks = []
                for dk in range(nv):
                    bias = bt_ref[2 * pp + half, dk + NA_WIN_R // 2 - 1] * LOG2E
                    if dk == 0:
                        bias = jnp.where(lo, bias, NEG_INF)
                    elif dk == NA_WIN_R:
                        bias = jnp.where(lo, NEG_INF, bias)
                    blocks.append(s_half[(k0 + dk) * w:(k0 + dk + 1) * w, strip] + bias)
                s_v = jnp.concatenate(blocks, axis=0)
                m_s = jnp.maximum(m_c[:, strip], jnp.max(s_v, axis=0, keepdims=True))
                p_v = jnp.exp2(s_v - m_s)
                m_parts.append(m_s)
                l_parts.append(jnp.sum(p_v, axis=0, keepdims=True))
                p_half[k0 * w:(k0 + nv) * w, strip] = p_v.astype(BF16)
                if k0 > 0:
                    p_half[0:k0 * w, strip] = jnp.zeros((k0 * w, LANES), BF16)
                if k0 + nv < rk:
                    p_half[(k0 + nv) * w:rk * w, strip] = jnp.zeros(((rk - k0 - nv) * w, LANES), BF16)
            m = jnp.concatenate(m_parts, axis=1)
            pt_c = jnp.exp2(st_c - m)
            l = jnp.sum(pt_c, axis=0, keepdims=True) + jnp.concatenate(l_parts, axis=1)
            o_t = (_dot(_half_select(vct, half, 0), pt_c.astype(BF16))
                   + _dot(_half_select(vbt, half, 0), p_half[...]))
            o_t = o_t / l
            out_t = o_t if out_t is None else out_t + o_t
        o_ref[:, cols] = out_t.T.astype(o_ref.dtype)

    def all_pairs(fn):
        for pp in range(pps):
            fn(pp)

    if is_lat:
        interior = (r0 >= NA_WIN_R // 2) & (r0 <= rows - rq - NA_WIN_R // 2)
        pl.when(interior)(functools.partial(all_pairs, attend_interior))
        pl.when(jnp.logical_not(interior))(functools.partial(all_pairs, attend))
    else:
        all_pairs(attend)


def _na_attention(a_q, a_lat, a_ctx, bias_tab, geo, rq, pps, is_lat):
    b, n, ctx = geo
    width = NA_HEADS * NA_HD
    wb = pps * LANES
    pairs = width // wb
    tq = rq * GRID_W
    rk = rq + NA_WIN_R
    grid_rows = n // GRID_W
    rows = n if is_lat else ctx
    assert grid_rows >= rk and rows % tq == 0
    nt = rows // tq
    qo, ko, vo = pairs, 2 * pairs, 3 * pairs
    in_specs = [pl.BlockSpec((tq, wb), lambda bi, pi, t: (bi * nt + t, qo + pi))]
    args = [a_q]
    if is_lat:
        in_specs += [pl.BlockSpec((n, wb), lambda bi, pi, t: (bi, ko + pi)),
                     pl.BlockSpec((n, wb), lambda bi, pi, t: (bi, vo + pi))]
        args += [a_lat, a_lat]
    in_specs += [pl.BlockSpec((ctx, wb), lambda bi, pi, t: (bi, ko + pi)),
                 pl.BlockSpec((ctx, wb), lambda bi, pi, t: (bi, vo + pi))]
    args += [a_ctx, a_ctx]
    scratch = []
    if is_lat:
        in_specs.append(pl.BlockSpec((2 * pps, 2 * NA_WIN_R, GRID_W, LANES),
                                     lambda bi, pi, t: (pi, 0, 0, 0)))
        args.append(bias_tab)
        scratch += [pltpu.VMEM((2 * pps, rk * GRID_W, tq), F32),
                    pltpu.VMEM((2 * pps, rk * GRID_W, tq), BF16)]
    return pl.pallas_call(
        functools.partial(_na_kernel, is_lat=is_lat, rq=rq, rk=rk, rows=grid_rows, pps=pps),
        out_shape=jax.ShapeDtypeStruct((b * rows, width), BF16),
        grid=(b, pairs, nt),
        in_specs=in_specs,
        out_specs=pl.BlockSpec((tq, wb), lambda bi, pi, t: (bi * nt + t, pi)),
        scratch_shapes=scratch,
        compiler_params=_cparams(("parallel", "arbitrary", "arbitrary")),
        name="na_attention" if is_lat else "na_ctx_attention",
    )(*args)


def _rope_tables(n, d_rot):
    t = jnp.arange(n)
    row = (t // GRID_W).astype(F32)
    col = (t % GRID_W).astype(F32)
    d_axis = d_rot // 2
    inv = ROPE_BASE ** (-jnp.arange(0, d_axis, 2, dtype=F32) / d_axis)
    ang = jnp.concatenate([row[:, None] * inv, col[:, None] * inv], axis=-1)
    cos, sin = jnp.cos(ang), jnp.sin(ang)
    reps = LANES // d_rot
    return (jnp.tile(cos, (1, 2 * reps)),
            jnp.tile(jnp.concatenate([-sin, sin], axis=-1), (1, reps)))


def _na_bias_table(rpb):
    col = jnp.arange(GRID_W)
    dc = jnp.clip(col[:, None] - col[None, :], -(NA_WIN_C - 1), NA_WIN_C - 1) + (NA_WIN_C - 1)
    cs = jnp.clip(col - NA_WIN_C // 2, 0, GRID_W - NA_WIN_C)[None, :]
    col_ok = (col[:, None] >= cs) & (col[:, None] < cs + NA_WIN_C)
    bt = jnp.where(col_ok, rpb[:, :, dc], NEG_INF)
    bt = jnp.pad(bt, ((0, 0), (1, 1), (0, 0), (0, 0)))
    return jnp.concatenate([bt[:, 1:], bt[:, :-1]], axis=-1).astype(F32)


def _tile_plan(b, n, n_ctx, d):
    tm = _largest_divisor(math.gcd(n, b * n_ctx), (512, 256, 128))
    return dict(
        tm=tm,
        tq_flash=_largest_divisor(n, (512, 256, 128)),
        tq_diff=_largest_divisor(n, (512, 256, 128)),
        tk=_largest_divisor(n + n_ctx, (768, 640, 512, 384, 256, 128)),
        tk_mla=_largest_divisor(n + n_ctx, (768, 640, 512, 384, 256, 128)),
        heads_mla=2,
        tq_swa=128,
        rq_na=8,
        rq_na_ctx=4,
        pairs_na=2,
    )


def kernel(x, c, ctx, c_ctx, l0_mod_w, l0_mod_b, l0_norm, l0_w_in, l0_q_norm, l0_w_qb, l0_kv_norm, l0_w_kvb, l0_w_out, l1_mod_w, l1_mod_b, l1_norm, l1_w_in, l1_sink, l1_w_out, l2_mod_w, l2_mod_b, l2_norm, l2_w_in, l2_rpb, l2_w_out, l3_mod_w, l3_mod_b, l3_norm, l3_w_in, l3_lam_q1, l3_lam_k1, l3_lam_q2, l3_lam_k2, l3_subln, l3_w_out, final_norm):
    b, n, d = x.shape
    n_ctx = ctx.shape[1]
    geo = (b, n, n_ctx)
    assert b < 8 and n % GRID_W == 0
    tp = _tile_plan(b, n, n_ctx, d)
    tm = tp["tm"]
    lat, cx = ("batch", n // tm), ("const", b)

    x_lat = x.reshape(b * n, d)
    x_ctx = ctx.reshape(b * n_ctx, d)
    c_rows = jnp.concatenate([c, c_ctx[None, :], jnp.zeros((8 - b - 1, d), F32)], axis=0)
    rope64 = _rope_tables(n, 64) + (n // tm,)
    rope128 = _rope_tables(n, 128) + (n // tm,)

    def groups(*spans):
        out = []
        for cols, code, scale in spans:
            out += [(code, scale)] * (cols // LANES)
        return tuple(out)

    def both(fn):
        return fn(x_lat, lat, True), fn(x_ctx, cx, False)

    def project_normed(h_lat, h_ctx, w, rope, grp_cfg, tn, name):
        return (_projection(h_lat, 0, d, None, None, None, w, rope, grp_cfg, tm, tn, name),
                _projection(h_ctx, 0, d, None, None, None, w, None, grp_cfg, tm, tn, name + "_ctx"))

    mods = [_modulation(c_rows, w, bias) for w, bias in
            ((l0_mod_w, l0_mod_b), (l1_mod_w, l1_mod_b), (l2_mod_w, l2_mod_b), (l3_mod_w, l3_mod_b))]

    mod = mods[0]
    r = MLA_RANK
    w_in = jnp.concatenate([l0_w_in[:, 2 * r + MLA_ROPE:], l0_w_in[:, :2 * r + MLA_ROPE],
                            jnp.zeros((d, 2 * LANES - MLA_ROPE), F32)], axis=1).astype(BF16)
    width = MLA_HEADS * MLA_V
    g0 = groups((width + 2 * r, PLAIN, 1.0), (LANES, ROPE64, 1.0), (LANES, PLAIN, 1.0))
    a_lat, a_ctx = both(lambda xs, grp, is_lat: _projection(
        xs, 0, d, l0_norm, mod, grp, w_in, rope64 if is_lat else None, g0, tm, 13 * LANES,
        "mla_in"))
    qk = MLA_NOPE + MLA_ROPE
    w_q = l0_w_qb.reshape(r, MLA_HEADS, qk)
    w_q = jnp.pad(w_q, ((0, 0), (0, 0), (0, 256 - qk))).reshape(r, MLA_HEADS * 256).astype(BF16)
    q_scale = qk ** -0.5 * LOG2E
    gq = groups((LANES, PLAIN, q_scale), (LANES, ROPE64, q_scale)) * MLA_HEADS
    gkv = groups((MLA_HEADS * 256, PLAIN, 1.0))
    w_kv = l0_w_kvb.astype(BF16)
    q_lat, kv_lat = (
        _projection(a_lat, width // r, r, l0_q_norm, None, None, w_q, rope64, gq, tm, 4096, "mla_q"),
        _projection(a_lat, width // r + 1, r, l0_kv_norm, None, None, w_kv, None, gkv, tm, 4096,
                    "mla_kv"))
    q_ctx, kv_ctx = (
        _projection(a_ctx, width // r, r, l0_q_norm, None, None, w_q, None, gq, tm, 4096,
                    "mla_q_ctx"),
        _projection(a_ctx, width // r + 1, r, l0_kv_norm, None, None, w_kv, None, gkv, tm, 4096,
                    "mla_kv_ctx"))
    kr_blk = (width + 2 * r) // LANES
    o_lat = _mla_attention(q_lat, kv_lat, kv_ctx, a_lat, a_ctx, kr_blk, geo, tp["tq_flash"],
                           tp["tk_mla"], tp["heads_mla"])
    o_ctx = _mla_ctx_attention(q_ctx, kv_ctx, a_ctx, kr_blk, geo)
    w_out = l0_w_out.astype(BF16)
    x_lat, h_lat = _out_projection(o_lat, a_lat, w_out, x_lat, mod, lat, tm, "mla_out", False,
                                   l1_norm, mods[1])
    x_ctx, h_ctx = _out_projection(o_ctx, a_ctx, w_out, x_ctx, mod, cx, tm, "mla_out", False,
                                   l1_norm, mods[1])

    mod = mods[1]
    per = SWA_HEADS // SWA_KV_HEADS
    pairs_kv = SWA_KV_HEADS // 2

    def pair_heads(w, axis):
        shape = w.shape
        w = w.reshape(shape[:axis] + (pairs_kv, 2, per, SWA_HD) + shape[axis + 1:])
        return jnp.swapaxes(w, axis + 1, axis + 2).reshape(shape)

    width = SWA_HEADS * SWA_HD
    kvw = SWA_KV_HEADS * SWA_HD
    w_q, w_k, w_v, w_g = (l1_w_in[:, :width], l1_w_in[:, width:width + kvw],
                          l1_w_in[:, width + kvw:width + 2 * kvw], l1_w_in[:, width + 2 * kvw:])
    w_in = jnp.concatenate([pair_heads(w_g, 1), pair_heads(w_q, 1), w_k, w_v],
                           axis=1).astype(BF16)
    g1 = groups((width, PLAIN, 1.0), (width, ROPE64, SWA_HD ** -0.5 * LOG2E),
                (kvw, ROPE64, 1.0), (kvw, PLAIN, 1.0))
    a_lat, a_ctx = project_normed(h_lat, h_ctx, w_in, rope64, g1, 2304, "swa_in")
    sink_perm = jnp.swapaxes(l1_sink.reshape(pairs_kv, 2, per), 1, 2).reshape(SWA_HEADS)
    tq = tp["tq_swa"]
    o_lat = _swa_attention(a_lat, a_lat, a_ctx, sink_perm, geo, tq, True)
    o_ctx = _swa_attention(a_ctx, a_lat, a_ctx, sink_perm, geo, tq, False)
    w_out = pair_heads(l1_w_out, 0).astype(BF16)
    x_lat, h_lat = _out_projection(o_lat, a_lat, w_out, x_lat, mod, lat, tm, "swa_out", True,
                                   l2_norm, mods[2])
    x_ctx, h_ctx = _out_projection(o_ctx, a_ctx, w_out, x_ctx, mod, cx, tm, "swa_out", True,
                                   l2_norm, mods[2])

    mod = mods[2]
    width = NA_HEADS * NA_HD
    w_in = jnp.concatenate([l2_w_in[:, 3 * width:], l2_w_in[:, :3 * width]], axis=1).astype(BF16)
    g2 = groups((width, PLAIN, 1.0), (width, PLAIN, NA_HD ** -0.5 * LOG2E), (2 * width, PLAIN, 1.0))
    a_lat, a_ctx = project_normed(h_lat, h_ctx, w_in, None, g2, 2048, "na_in")
    bias_tab = _na_bias_table(l2_rpb)
    o_lat = _na_attention(a_lat, a_lat, a_ctx, bias_tab, geo, tp["rq_na"], tp["pairs_na"], True)
    o_ctx = _na_attention(a_ctx, a_lat, a_ctx, bias_tab, geo, tp["rq_na_ctx"], tp["pairs_na"], False)
    w_out = l2_w_out.astype(BF16)
    x_lat, h_lat = _out_projection(o_lat, a_lat, w_out, x_lat, mod, lat, tm, "na_out", True,
                                   l3_norm, mods[3])
    _, h_ctx = _out_projection(o_ctx, a_ctx, w_out, x_ctx, mod, cx, tm, "na_out", True,
                               l3_norm, mods[3])

    mod = mods[3]
    width = DIFF_HEADS * 2 * DIFF_HD
    w_in = jnp.concatenate([l3_w_in[:, 3 * width:], l3_w_in[:, :3 * width]], axis=1).astype(BF16)
    g3 = groups((width, PLAIN, 1.0), (width, ROPE128, DIFF_HD ** -0.5 * LOG2E),
                (width, ROPE128, 1.0), (width, PLAIN, 1.0))
    a_lat = _projection(h_lat, 0, d, None, None, None, w_in, rope128, g3, tm, 2048, "diff_in")
    kv_ctx = _projection(h_ctx, 0, d, None, None, None, w_in[:, 2 * width:], None,
                         g3[2 * width // LANES:], tm, 1024, "diff_in_ctx")
    lam_init = 0.8 - 0.6 * math.exp(-0.3 * 3)
    lam_rows = jnp.concatenate([jnp.stack([l3_lam_q1, l3_lam_k1, l3_lam_q2, l3_lam_k2]),
                                jnp.zeros((4, DIFF_HD), F32)], axis=0)
    o_lat = _diff_attention(a_lat, kv_ctx, lam_rows, l3_subln, geo, tp["tq_diff"], tp["tk"],
                            lam_init)
    out = _out_projection(o_lat, a_lat, l3_w_out.astype(BF16), x_lat, mod, lat, tm, "diff_out",
                          False, final_norm)
    return out.reshape(b, n, d)
```
